```python
import jax, jax.numpy as jnp
from jax import lax
import numpy as np

D_MODEL = 1024
BATCH = 32
SEQ = 256
DEPTH = 2
DEC_BATCH = 4
DEC_SEQ = 4096
PAST_LEN = 512

GRID_W = 64
D_FF = 4 * D_MODEL
EPS = 1e-6
D_RNN = D_MODEL // 2
RNN_BLOCKS = 8
RNN_BLOCK = D_RNN // RNN_BLOCKS
CONV_W = 4
LRU_C = 8.0
D_POOL = D_MODEL // 2
POOL_WINDOWS = (2, 4, 8, 16)
POOL_GROUP = D_POOL // len(POOL_WINDOWS)
N_HEADS = 16
QK_NOPE = 64
QK_ROPE = 32
V_HEAD = 64
Q_RANK = 384
KV_RANK = 256
ROPE_BASE = 10000.0
Q_BLOCK = 128

kernel_name = "hybrid_rglru_pool_mla_dit_step"


def rms_norm(x, g):
    xf = x.astype(jnp.float32)
    y = xf * lax.rsqrt(jnp.mean(xf * xf, axis=-1, keepdims=True) + EPS)
    return (y * g.astype(jnp.float32)).astype(x.dtype)


def adaln(cond, w_mod, b_mod):
    m = jax.nn.silu(cond) @ w_mod + b_mod
    return [t[:, None, :] for t in jnp.split(m, 6, axis=-1)]


def modulate(x, g, shift, scale):
    return rms_norm(x, g) * (1 + scale) + shift


def sq_relu_mlp(h, w1, w2):
    return jnp.square(jax.nn.relu(h @ w1)) @ w2


def dwconv_centred(x, w, b):
    S = x.shape[1]
    left = CONV_W // 2
    xp = jnp.pad(x, ((0, 0), (left, CONV_W - 1 - left), (0, 0)))
    return b + sum(xp[:, k:k + S] * w[k] for k in range(CONV_W))


def rglru_scan(xc, w_a, b_a, w_i, b_i, lam, h0, reverse):
    B, S, _ = xc.shape
    xf = xc.astype(jnp.float32)
    xb = xf.reshape(B, S, RNN_BLOCKS, RNN_BLOCK)
    r = jax.nn.sigmoid(jnp.einsum('bsni,nij->bsnj', xb, w_a).reshape(B, S, D_RNN) + b_a)
    i = jax.nn.sigmoid(jnp.einsum('bsni,nij->bsnj', xb, w_i).reshape(B, S, D_RNN) + b_i)
    log_a = LRU_C * r * jax.nn.log_sigmoid(lam.astype(jnp.float32))
    a = jnp.exp(log_a)
    u = jnp.sqrt(-jnp.expm1(2.0 * log_a)) * (i * xf)
    if reverse:
        a, u = jnp.flip(a, axis=1), jnp.flip(u, axis=1)
    u = u.at[:, 0].add(a[:, 0] * h0.astype(jnp.float32))

    def combine(lhs, rhs):
        a1, b1 = lhs
        a2, b2 = rhs
        return a1 * a2, a2 * b1 + b2

    _, h = lax.associative_scan(combine, (a, u), axis=1)
    h_last = h[:, -1]
    if reverse:
        h = jnp.flip(h, axis=1)
    return h, h_last


def multiscale_pool(xp, w_pool, s_pool):
    B, S, _ = xp.shape
    xf = xp.astype(jnp.float32)
    cs = jnp.pad(jnp.cumsum(xf, axis=1), ((0, 0), (1, 0), (0, 0)))
    t = jnp.arange(S)
    outs = []
    for gi, w in enumerate(POOL_WINDOWS):
        left = w // 2
        right = w - 1 - left
        lo = jnp.maximum(t - left, 0)
        hi = jnp.minimum(t + right, S - 1) + 1
        sl = slice(gi * POOL_GROUP, (gi + 1) * POOL_GROUP)
        csg = cs[..., sl]
        mean = (csg[:, hi] - csg[:, lo]) / (hi - lo).astype(jnp.float32)[None, :, None]
        outs.append(mean - xf[..., sl])
    d = jnp.stack(outs, axis=2)
    y = jnp.einsum('bsgc,gcd->bsgd', d, w_pool).reshape(B, S, D_POOL)
    return (y * s_pool).astype(xp.dtype)


def lru_pool_mixer(h, p, h0_fwd, h0_bwd):
    z = h @ p['w_in']
    xr, gate, xpool = jnp.split(z, [D_RNN, 2 * D_RNN], axis=-1)
    xc = dwconv_centred(xr, p['conv_w'], p['conv_b'])
    hf, hf_last = rglru_scan(xc, p['lru_w_a'][0], p['lru_b_a'][0], p['lru_w_i'][0], p['lru_b_i'][0],
                             p['lru_lam'][0], h0_fwd, False)
    hb, hb_last = rglru_scan(xc, p['lru_w_a'][1], p['lru_b_a'][1], p['lru_w_i'][1], p['lru_b_i'][1],
                             p['lru_lam'][1], h0_bwd, True)
    y_rnn = (hf + hb).astype(h.dtype) * jax.nn.gelu(gate)
    y_pool = multiscale_pool(xpool, p['pool_w'], p['pool_scale'])
    out = jnp.concatenate([y_rnn, y_pool], axis=-1) @ p['w_out']
    return out, jnp.stack([hf_last, hb_last], axis=1)


def axial_angles(S):
    rows = S // GRID_W
    row = jnp.broadcast_to(jnp.arange(rows)[:, None], (rows, GRID_W)).reshape(-1).astype(jnp.float32)
    col = jnp.broadcast_to(jnp.arange(GRID_W)[None, :], (rows, GRID_W)).reshape(-1).astype(jnp.float32)
    half = QK_ROPE // 2
    inv = ROPE_BASE ** (-jnp.arange(0, half, 2, dtype=jnp.float32) / half)
    return row[:, None] * inv, col[:, None] * inv


def rope_rotate(x, ang):
    x1, x2 = jnp.split(x, 2, axis=-1)
    cos, sin = jnp.cos(ang), jnp.sin(ang)
    return jnp.concatenate([x1 * cos - x2 * sin, x2 * cos + x1 * sin], axis=-1)


def rope2d(x, ang_r, ang_c):
    xf = x.astype(jnp.float32)
    half = QK_ROPE // 2
    return jnp.concatenate([rope_rotate(xf[..., :half], ang_r),
                            rope_rotate(xf[..., half:], ang_c)], axis=-1).astype(x.dtype)


def mla_project(h, p):
    B, S, _ = h.shape
    z = h @ p['w_in']
    cq, ckv, kpe = jnp.split(z, [Q_RANK, Q_RANK + KV_RANK], axis=-1)
    q = (rms_norm(cq, p['g_q']) @ p['w_qb']).reshape(B, S, N_HEADS, QK_NOPE + QK_ROPE)
    return q[..., :QK_NOPE], q[..., QK_NOPE:], rms_norm(ckv, p['g_kv']), kpe


def mla_expand(ckv, w_kvb):
    B, S, _ = ckv.shape
    kv = (ckv @ w_kvb).reshape(B, S, N_HEADS, QK_NOPE + V_HEAD)
    return kv[..., :QK_NOPE], kv[..., QK_NOPE:]


def attend_blocked(q_nope, q_pe, k_nope, k_pe, v):
    B, S, H, _ = q_nope.shape
    nb = S // Q_BLOCK
    scale = (QK_NOPE + QK_ROPE) ** -0.5

    def block(args):
        qn, qp = args
        s = (jnp.einsum('bqhd,bkhd->bhqk', qn, k_nope)
             + jnp.einsum('bqhr,bkr->bhqk', qp, k_pe)).astype(jnp.float32) * scale
        pr = jax.nn.softmax(s, axis=-1).astype(v.dtype)
        return jnp.einsum('bhqk,bkhd->bqhd', pr, v)

    def to_blocks(t):
        return jnp.moveaxis(t.reshape(B, nb, Q_BLOCK, *t.shape[2:]), 1, 0)

    o = lax.map(block, (to_blocks(q_nope), to_blocks(q_pe)))
    return jnp.moveaxis(o, 0, 1).reshape(B, S, H * V_HEAD)


def mla_context(h, p):
    q_nope, q_pe, ckv, kpe = mla_project(h, p)
    k_nope, v = mla_expand(ckv, p['w_kvb'])
    o = attend_blocked(q_nope, q_pe, k_nope, kpe, v)
    return o @ p['w_out'], ckv, kpe


def mla_latent(h, p, ctx_ckv, ctx_kpe):
    S = h.shape[1]
    q_nope, q_pe, ckv, kpe = mla_project(h, p)
    ang_r, ang_c = axial_angles(S)
    q_pe = rope2d(q_pe, ang_r[:, None, :], ang_c[:, None, :])
    kpe = rope2d(kpe, ang_r, ang_c)
    k_lat, v_lat = mla_expand(ckv, p['w_kvb'])
    k_ctx, v_ctx = mla_expand(ctx_ckv.astype(h.dtype), p['w_kvb'])
    k_nope = jnp.concatenate([k_lat, k_ctx], axis=1)
    k_pe = jnp.concatenate([kpe, ctx_kpe.astype(h.dtype)], axis=1)
    v = jnp.concatenate([v_lat, v_ctx], axis=1)
    o = attend_blocked(q_nope, q_pe, k_nope, k_pe, v)
    return o @ p['w_out']


def setup_inputs(seed: int = 0) -> dict:
    key = jax.random.key(seed)
    ks = iter(jax.random.split(key, 64))
    nrm = lambda shape, s: jax.random.normal(next(ks), shape, jnp.float32) * s
    gain = lambda n: 1.0 + nrm((n,), 0.02)
    D = D_MODEL
    a0 = jax.random.uniform(next(ks), (2, D_RNN), jnp.float32, 0.9, 0.999)
    pa = a0 ** (1.0 / LRU_C)
    lam = jnp.log(pa) - jnp.log1p(-pa)
    return {
        "x_prompt": nrm((BATCH, SEQ, D), 1.0),
        "x_sample": nrm((DEC_BATCH, DEC_SEQ, D), 1.0),
        "state_l0_lru": nrm((DEC_BATCH, 2, D_RNN), 1.0),
        "cache_l1_ckv": nrm((DEC_BATCH, PAST_LEN, KV_RANK), 1.0),
        "cache_l1_kpe": nrm((DEC_BATCH, PAST_LEN, QK_ROPE), 1.0),
        "c": nrm((DEC_BATCH, D), 1.0),
        "c_ctx": nrm((D,), 1.0),
        "l0_w_mod": nrm((D, 6 * D), D ** -0.5),
        "l0_b_mod": nrm((6 * D,), 0.02),
        "l0_g_mix": gain(D),
        "l0_g_ffn": gain(D),
        "l0_w_in": nrm((D, 2 * D_RNN + D_POOL), D ** -0.5),
        "l0_conv_w": nrm((CONV_W, D_RNN), CONV_W ** -0.5),
        "l0_conv_b": nrm((D_RNN,), 0.02),
        "l0_lru_w_a": nrm((2, RNN_BLOCKS, RNN_BLOCK, RNN_BLOCK), RNN_BLOCK ** -0.5),
        "l0_lru_b_a": nrm((2, D_RNN), 0.02),
        "l0_lru_w_i": nrm((2, RNN_BLOCKS, RNN_BLOCK, RNN_BLOCK), RNN_BLOCK ** -0.5),
        "l0_lru_b_i": nrm((2, D_RNN), 0.02),
        "l0_lru_lam": lam,
        "l0_pool_w": nrm((len(POOL_WINDOWS), POOL_GROUP, POOL_GROUP), POOL_GROUP ** -0.5),
        "l0_pool_scale": gain(D_POOL),
        "l0_w_out": nrm((D_RNN + D_POOL, D), (D_RNN + D_POOL) ** -0.5),
        "l0_ffn_w1": nrm((D, D_FF), D ** -0.5),
        "l0_ffn_w2": nrm((D_FF, D), D_FF ** -0.5),
        "l1_w_mod": nrm((D, 6 * D), D ** -0.5),
        "l1_b_mod": nrm((6 * D,), 0.02),
        "l1_g_mix": gain(D),
        "l1_g_ffn": gain(D),
        "l1_w_in": nrm((D, Q_RANK + KV_RANK + QK_ROPE), D ** -0.5),
        "l1_g_q": gain(Q_RANK),
        "l1_w_qb": nrm((Q_RANK, N_HEADS * (QK_NOPE + QK_ROPE)), Q_RANK ** -0.5),
        "l1_g_kv": gain(KV_RANK),
        "l1_w_kvb": nrm((KV_RANK, N_HEADS * (QK_NOPE + V_HEAD)), KV_RANK ** -0.5),
        "l1_w_out": nrm((N_HEADS * V_HEAD, D), (N_HEADS * V_HEAD) ** -0.5),
        "l1_ffn_w1": nrm((D, D_FF), D ** -0.5),
        "l1_ffn_w2": nrm((D_FF, D), D_FF ** -0.5),
        "g_final": gain(D),
    }


def reference(x_prompt, x_sample, state_l0_lru, cache_l1_ckv, cache_l1_kpe, c, c_ctx,
              l0_w_mod, l0_b_mod, l0_g_mix, l0_g_ffn, l0_w_in, l0_conv_w, l0_conv_b,
              l0_lru_w_a, l0_lru_b_a, l0_lru_w_i, l0_lru_b_i, l0_lru_lam, l0_pool_w, l0_pool_scale,
              l0_w_out, l0_ffn_w1, l0_ffn_w2,
              l1_w_mod, l1_b_mod, l1_g_mix, l1_g_ffn, l1_w_in, l1_g_q, l1_w_qb, l1_g_kv, l1_w_kvb,
              l1_w_out, l1_ffn_w1, l1_ffn_w2, g_final):
    layers = [
        dict(w_mod=l0_w_mod, b_mod=l0_b_mod, g_mix=l0_g_mix, g_ffn=l0_g_ffn, w_in=l0_w_in,
             conv_w=l0_conv_w, conv_b=l0_conv_b, lru_w_a=l0_lru_w_a, lru_b_a=l0_lru_b_a,
             lru_w_i=l0_lru_w_i, lru_b_i=l0_lru_b_i, lru_lam=l0_lru_lam, pool_w=l0_pool_w,
             pool_scale=l0_pool_scale, w_out=l0_w_out, ffn_w1=l0_ffn_w1, ffn_w2=l0_ffn_w2),
        dict(w_mod=l1_w_mod, b_mod=l1_b_mod, g_mix=l1_g_mix, g_ffn=l1_g_ffn, w_in=l1_w_in,
             g_q=l1_g_q, w_qb=l1_w_qb, g_kv=l1_g_kv, w_kvb=l1_w_kvb, w_out=l1_w_out,
             ffn_w1=l1_ffn_w1, ffn_w2=l1_ffn_w2),
    ]
    caches = [(state_l0_lru,), (cache_l1_ckv, cache_l1_kpe)]
    xp, xs = x_prompt, x_sample
    new_state = []
    for l in range(DEPTH):
        p = layers[l]
        mp = adaln(c_ctx[None, :], p['w_mod'], p['b_mod'])
        ms = adaln(c, p['w_mod'], p['b_mod'])
        hp = modulate(xp, p['g_mix'], mp[0], mp[1])
        hs = modulate(xs, p['g_mix'], ms[0], ms[1])
        if l % 2 == 0:
            zeros = jnp.zeros((xp.shape[0], D_RNN), jnp.float32)
            op, st = lru_pool_mixer(hp, p, zeros, zeros)
            st_in = caches[l][0]
            os_, _ = lru_pool_mixer(hs, p, st_in[:, 0], st_in[:, 1])
            new_state.append(st.astype(xp.dtype))
        else:
            op, ckv, kpe = mla_context(hp, p)
            os_ = mla_latent(hs, p, caches[l][0], caches[l][1])
            new_state.append(ckv)
            new_state.append(kpe)
        xp = xp + mp[2] * op
        xs = xs + ms[2] * os_
        xp = xp + mp[5] * sq_relu_mlp(modulate(xp, p['g_ffn'], mp[3], mp[4]), p['ffn_w1'], p['ffn_w2'])
        xs = xs + ms[5] * sq_relu_mlp(modulate(xs, p['g_ffn'], ms[3], ms[4]), p['ffn_w1'], p['ffn_w2'])
    y_prompt = rms_norm(xp, g_final)
    y_sample = rms_norm(xs, g_final)
    new_lru, new_ckv, new_kpe = new_state[0], new_state[1], new_state[2]
    return (y_prompt, y_sample, new_lru, new_ckv, new_kpe)
```

```python
import functools
import math

import jax
import jax.numpy as jnp
from jax import lax
from jax.experimental import pallas as pl
from jax.experimental.pallas import tpu as pltpu

F32 = jnp.float32
BF16 = jnp.bfloat16

D_MODEL = 1024
D_FF = 4 * D_MODEL
EPS = 1e-6
D_RNN = 512
RNN_BLOCKS = 8
RNN_BLOCK = 64
CONV_W = 4
LRU_C = 8.0
D_POOL = 512
POOL_WINDOWS = (2, 4, 8, 16)
POOL_GROUP = 128
N_HEADS = 16
QK_NOPE = 64
QK_ROPE = 32
V_HEAD = 64
Q_RANK = 384
KV_RANK = 256
ROPE_BASE = 10000.0
GRID_W = 64

LANES = 128
SUBLANES = 8
HEAD_PAD = 128
HALO = 8
VMEM_LIMIT = 52 * 1024 * 1024


def _cparams(sem):
    return pltpu.CompilerParams(dimension_semantics=sem, vmem_limit_bytes=VMEM_LIMIT)


def _dot(a, b):
    return jnp.dot(a, b, preferred_element_type=F32)


def _rms_mod(x, g, shift, scale):
    ms = jnp.mean(x * x, axis=-1, keepdims=True)
    y = x * lax.rsqrt(ms + EPS) * g
    return y * (1.0 + scale) + shift


def _mod_kernel(c_ref, w_ref, b_ref, o_ref):
    c = c_ref[...]
    s = c * jax.nn.sigmoid(c)
    o_ref[...] = _dot(s.astype(BF16), w_ref[...].astype(BF16)) + b_ref[...]


def _mod_call(cond8, w_mod, b_mod):
    n = w_mod.shape[1]
    tn = 1024
    return pl.pallas_call(
        _mod_kernel,
        grid=(n // tn,),
        in_specs=[
            pl.BlockSpec((SUBLANES, D_MODEL), lambda j: (0, 0)),
            pl.BlockSpec((D_MODEL, tn), lambda j: (0, j)),
            pl.BlockSpec((1, tn), lambda j: (0, j)),
        ],
        out_specs=pl.BlockSpec((SUBLANES, tn), lambda j: (0, j)),
        out_shape=jax.ShapeDtypeStruct((SUBLANES, n), F32),
        compiler_params=_cparams(("arbitrary",)),
        name="adaln_mod",
    )(cond8, w_mod, b_mod.reshape(1, n))


def _l0_in_kernel(x_ref, mod_ref, g_ref, w_ref, z_ref):
    shift = mod_ref[0, :, 0:D_MODEL]
    scale = mod_ref[0, :, D_MODEL:2 * D_MODEL]
    h = _rms_mod(x_ref[...], g_ref[...], shift, scale)
    z_ref[...] = _dot(h.astype(BF16), w_ref[...])


def _l0_in_call(x2d, mod3, g, w_bf, tm):
    t = x2d.shape[0]
    nb = mod3.shape[0]
    rows_per_b = t // nb
    n = w_bf.shape[1]
    return pl.pallas_call(
        _l0_in_kernel,
        grid=(t // tm,),
        in_specs=[
            pl.BlockSpec((tm, D_MODEL), lambda i: (i, 0)),
            pl.BlockSpec((1, 1, 6 * D_MODEL), lambda i: ((i * tm) // rows_per_b, 0, 0)),
            pl.BlockSpec((1, D_MODEL), lambda i: (0, 0)),
            pl.BlockSpec((D_MODEL, n), lambda i: (0, 0)),
        ],
        out_specs=pl.BlockSpec((tm, n), lambda i: (i, 0)),
        out_shape=jax.ShapeDtypeStruct((t, n), F32),
        compiler_params=_cparams(("arbitrary",)),
        name="l0_in_proj",
    )(x2d, mod3, g.reshape(1, D_MODEL), w_bf)


def _scan_tile(a, u, carry_row, reverse, a_s, u_s, ts):
    g = ts // SUBLANES
    a3 = a.reshape(g, SUBLANES, D_RNN)
    u3 = u.reshape(g, SUBLANES, D_RNN)
    row = lax.broadcasted_iota(jnp.int32, (g, SUBLANES, D_RNN), 1)
    for d in (1, 2, 4):
        if reverse:
            sh = SUBLANES - d
            m = row < SUBLANES - d
        else:
            sh = d
            m = row >= d
        a_sh = pltpu.roll(a3, sh, axis=1)
        u_sh = pltpu.roll(u3, sh, axis=1)
        u3 = u3 + a3 * jnp.where(m, u_sh, 0.0)
        a3 = a3 * jnp.where(m, a_sh, 1.0)
    a_s[...] = a3.reshape(ts, D_RNN)
    u_s[...] = u3.reshape(ts, D_RNN)
    edge = 0 if reverse else SUBLANES - 1

    def body(k, c):
        gi = (g - 1 - k) if reverse else k
        r0 = pl.multiple_of(gi * SUBLANES, SUBLANES)
        h = u_s[pl.ds(r0, SUBLANES), :] + a_s[pl.ds(r0, SUBLANES), :] * c
        u_s[pl.ds(r0, SUBLANES), :] = h
        return jnp.broadcast_to(h[edge:edge + 1, :], (SUBLANES, D_RNN))

    c0 = jnp.broadcast_to(carry_row, (SUBLANES, D_RNN))
    c_last = lax.fori_loop(0, g, body, c0, unroll=4)
    return c_last[0:1, :]


def _lru_kernel(*refs, reverse, with_pool, ts, nt, seq_len):
    if with_pool:
        (xr_ref, xrp_ref, xrn_ref, xp_ref, xpp_ref, xpn_ref, h0_ref, cw_ref, cb_ref, wa_ref, ba_ref, wi_ref,
         bi_ref, lam_ref, pw_ref, ps_ref, h_ref, st_ref, yp_ref, ext_s, a_s, u_s, carry_s, b2_s, b4_s, b8_s) = refs
    else:
        (xr_ref, xrp_ref, xrn_ref, gate_ref, hf_ref, h0_ref, cw_ref, cb_ref, wa_ref, ba_ref, wi_ref,
         bi_ref, lam_ref, y_ref, st_ref, ext_s, a_s, u_s, carry_s) = refs
    j = pl.program_id(1)
    t = (nt - 1 - j) if reverse else j

    @pl.when(j == 0)
    def _():
        carry_s[...] = h0_ref[0]

    def fill_ext(cur_ref, prev_ref, next_ref):
        ext_s[0:HALO, :] = jnp.where(t > 0, prev_ref[0], 0.0)
        ext_s[HALO:HALO + ts, :] = cur_ref[0]
        ext_s[HALO + ts:HALO + ts + HALO, :] = jnp.where(t < nt - 1, next_ref[0], 0.0)

    fill_ext(xr_ref, xrp_ref, xrn_ref)
    xc = cb_ref[...]
    for k in range(CONV_W):
        off = HALO - CONV_W // 2 + k
        xc = xc + ext_s[off:off + ts, :] * cw_ref[k:k + 1, :]

    xcb = xc.astype(BF16)
    half = D_RNN // 2
    ga = jnp.concatenate([_dot(xcb[:, :half], wa_ref[0]), _dot(xcb[:, half:], wa_ref[1])], axis=-1)
    gi = jnp.concatenate([_dot(xcb[:, :half], wi_ref[0]), _dot(xcb[:, half:], wi_ref[1])], axis=-1)
    r = jax.nn.sigmoid(ga + ba_ref[...])
    i = jax.nn.sigmoid(gi + bi_ref[...])
    log_a = (LRU_C * r) * jax.nn.log_sigmoid(lam_ref[...])
    a = jnp.exp(log_a)
    u = jnp.sqrt(-jnp.tanh(log_a) * (a * a + 1.0)) * (i * xc)

    c_last = _scan_tile(a, u, carry_s[...], reverse, a_s, u_s, ts)
    carry_s[...] = c_last
    st_ref[0] = c_last
    h = u_s[...]

    if not with_pool:
        y_ref[0] = ((hf_ref[0] + h) * jax.nn.gelu(gate_ref[0])).astype(BF16)
        return

    h_ref[0] = h

    fill_ext(xp_ref, xpp_ref, xpn_ref)
    n2 = ts + 2 * HALO
    b2_s[1:n2, :] = ext_s[1:n2, :] + ext_s[0:n2 - 1, :]
    b4_s[2:n2 - 1, :] = b2_s[3:n2, LANES:] + b2_s[1:n2 - 2, LANES:]
    b8_s[4:n2 - 3, :] = b4_s[6:n2 - 1, LANES:] + b4_s[2:n2 - 5, LANES:]
    s16 = b8_s[HALO + 4:HALO + 4 + ts, LANES:] + b8_s[HALO - 4:HALO - 4 + ts, LANES:]
    sums = (b2_s[HALO:HALO + ts, 0:LANES], b4_s[HALO:HALO + ts, 0:LANES], b8_s[HALO:HALO + ts, 0:LANES], s16)
    pos = t * ts + lax.broadcasted_iota(jnp.int32, (ts, LANES), 0)
    for gidx, w in enumerate(POOL_WINDOWS):
        left = w // 2
        right = w - 1 - left
        lo = jnp.maximum(pos - left, 0)
        hi = jnp.minimum(pos + right, seq_len - 1) + 1
        cnt = (hi - lo).astype(F32)
        xg = xp_ref[0, :, gidx * LANES:(gidx + 1) * LANES]
        dlt = sums[gidx] / cnt - xg
        yg = _dot(dlt.astype(BF16), pw_ref[gidx]) * ps_ref[:, gidx * LANES:(gidx + 1) * LANES]
        yp_ref[0, :, gidx * LANES:(gidx + 1) * LANES] = yg.astype(BF16)


def _lru_call(z3, h0, hf, prm, *, reverse, ts):
    nseq, seq_len, _ = z3.shape
    nt = seq_len // ts
    tb = ts // HALO
    nhb = seq_len // HALO
    d = 1 if reverse else 0
    with_pool = not reverse

    def tpos(j):
        return (nt - 1 - j) if reverse else j

    def cur(col):
        return pl.BlockSpec((1, ts, D_RNN), lambda s, j: (s, tpos(j), col))

    def prev(col):
        return pl.BlockSpec((1, HALO, D_RNN), lambda s, j: (s, jnp.maximum(tpos(j) * tb - 1, 0), col))

    def nxt(col):
        return pl.BlockSpec((1, HALO, D_RNN), lambda s, j: (s, jnp.minimum((tpos(j) + 1) * tb, nhb - 1), col))

    def const(shape):
        nd = len(shape)
        return pl.BlockSpec(shape, lambda s, j: (0,) * nd)

    tile_out = pl.BlockSpec((1, ts, D_RNN), lambda s, j: (s, tpos(j), 0))
    st_spec = pl.BlockSpec((1, 1, D_RNN), lambda s, j: (s, 0, 0))
    w_specs = [const((CONV_W, D_RNN)), const((1, D_RNN)), const((2, 256, 256)), const((1, D_RNN)),
               const((2, 256, 256)), const((1, D_RNN)), const((1, D_RNN))]
    w_args = [prm["conv_w"], prm["conv_b"], prm["wa"][d], prm["ba"][d], prm["wi"][d], prm["bi"][d], prm["lam"][d]]
    scratch = [pltpu.VMEM((ts + 2 * HALO, D_RNN), F32), pltpu.VMEM((ts, D_RNN), F32),
               pltpu.VMEM((ts, D_RNN), F32), pltpu.VMEM((1, D_RNN), F32)]
    kern = functools.partial(_lru_kernel, reverse=reverse, with_pool=with_pool, ts=ts, nt=nt, seq_len=seq_len)
    if with_pool:
        in_specs = [cur(0), prev(0), nxt(0), cur(2), prev(2), nxt(2), st_spec] + w_specs + [
            const((4, POOL_GROUP, POOL_GROUP)), const((1, D_POOL))]
        args = [z3, z3, z3, z3, z3, z3, h0] + w_args + [prm["pool_w"], prm["pool_scale"]]
        out_specs = [tile_out, st_spec, tile_out]
        out_shape = [jax.ShapeDtypeStruct((nseq, seq_len, D_RNN), F32),
                     jax.ShapeDtypeStruct((nseq, 1, D_RNN), F32),
                     jax.ShapeDtypeStruct((nseq, seq_len, D_POOL), BF16)]
        n2 = ts + 2 * HALO
        scratch = scratch + [pltpu.VMEM((n2, D_POOL), F32), pltpu.VMEM((n2, D_POOL - LANES), F32),
                             pltpu.VMEM((n2, D_POOL - 2 * LANES), F32)]
        name = "l0_lru_fwd_pool"
    else:
        in_specs = [cur(0), prev(0), nxt(0), cur(1), tile_out, st_spec] + w_specs
        args = [z3, z3, z3, z3, hf, h0] + w_args
        out_specs = [tile_out, st_spec]
        out_shape = [jax.ShapeDtypeStruct((nseq, seq_len, D_RNN), BF16),
                     jax.ShapeDtypeStruct((nseq, 1, D_RNN), F32)]
        name = "l0_lru_bwd"
    return pl.pallas_call(
        kern,
        grid=(nseq, nt),
        in_specs=in_specs,
        out_specs=out_specs,
        out_shape=out_shape,
        scratch_shapes=scratch,
        compiler_params=_cparams(("arbitrary", "arbitrary")),
        name=name,
    )(*args)


FF_CHUNK = 1024


def _ffn_kernel(*refs, n_y, final_norm):
    x_ref = refs[0]
    y_refs = refs[1:1 + n_y]
    mod_ref = refs[1 + n_y]
    wo_refs = refs[2 + n_y:2 + 2 * n_y]
    g_ref, w1_ref, w2_ref = refs[2 + 2 * n_y:5 + 2 * n_y]
    rest = refs[5 + 2 * n_y:]
    if final_norm:
        gf_ref, o_ref = rest
    else:
        (o_ref,) = rest

    def mod(k):
        return mod_ref[0, :, k * D_MODEL:(k + 1) * D_MODEL]

    mix = _dot(y_refs[0][...], wo_refs[0][...])
    for k in range(1, n_y):
        mix = mix + _dot(y_refs[k][...], wo_refs[k][...])
    x1 = x_ref[...] + mod(2) * mix
    hn = _rms_mod(x1, g_ref[...], mod(3), mod(4)).astype(BF16)
    acc = None
    for c in range(D_FF // FF_CHUNK):
        hc = _dot(hn, w1_ref[:, c * FF_CHUNK:(c + 1) * FF_CHUNK])
        hc = jnp.square(jnp.maximum(hc, 0.0)).astype(BF16)
        part = _dot(hc, w2_ref[c * FF_CHUNK:(c + 1) * FF_CHUNK, :])
        acc = part if acc is None else acc + part
    x2 = x1 + mod(5) * acc
    if final_norm:
        ms = jnp.mean(x2 * x2, axis=-1, keepdims=True)
        x2 = x2 * lax.rsqrt(ms + EPS) * gf_ref[...]
    o_ref[...] = x2


def _ffn_call(x2d, ys, mod3, wos, g, w1, w2, g_final, tm):
    t = x2d.shape[0]
    nb = mod3.shape[0]
    rows_per_b = t // nb
    n_y = len(ys)
    final_norm = g_final is not None

    def const(shape):
        return pl.BlockSpec(shape, lambda i: (0, 0), pipeline_mode=pl.Buffered(1))

    in_specs = [pl.BlockSpec((tm, D_MODEL), lambda i: (i, 0))]
    in_specs += [pl.BlockSpec((tm, y.shape[1]), lambda i: (i, 0)) for y in ys]
    in_specs += [pl.BlockSpec((1, 1, 6 * D_MODEL), lambda i: ((i * tm) // rows_per_b, 0, 0))]
    in_specs += [const(w.shape) for w in wos]
    in_specs += [const((1, D_MODEL)), const((D_MODEL, D_FF)), const((D_FF, D_MODEL))]
    args = [x2d, *ys, mod3, *wos, g.reshape(1, D_MODEL), w1, w2]
    if final_norm:
        in_specs.append(const((1, D_MODEL)))
        args.append(g_final.reshape(1, D_MODEL))
    return pl.pallas_call(
        functools.partial(_ffn_kernel, n_y=n_y, final_norm=final_norm),
        grid=(t // tm,),
        in_specs=in_specs,
        out_specs=pl.BlockSpec((tm, D_MODEL), lambda i: (i, 0)),
        out_shape=jax.ShapeDtypeStruct((t, D_MODEL), F32),
        compiler_params=_cparams(("arbitrary",)),
        name="mix_out_ffn",
    )(*args)


def _rope(x, cos, sa, sb):
    return x * cos + pltpu.roll(x, 8, axis=1) * sa + pltpu.roll(x, LANES - 8, axis=1) * sb


def _mla_proj_kernel(*refs, rope, emit_cache):
    x_ref, mod_ref, g_ref, win_ref, gq_ref, wqb_ref, gkv_ref, wk_ref, wv_ref = refs[:9]
    rest = refs[9:]
    if rope:
        cos_ref, sa_ref, sb_ref = rest[:3]
        rest = rest[3:]
    q_ref, k_ref, v_ref = rest[:3]
    if emit_cache:
        ckv_ref, kpe_ref = rest[3:]

    shift = mod_ref[0, :, 0:D_MODEL]
    scale = mod_ref[0, :, D_MODEL:2 * D_MODEL]
    h = _rms_mod(x_ref[...], g_ref[...], shift, scale).astype(BF16)
    z = _dot(h, win_ref[...])
    cq = z[:, :Q_RANK]
    ckv = z[:, Q_RANK:Q_RANK + KV_RANK]
    kpe = z[:, Q_RANK + KV_RANK:]
    cqn = cq * lax.rsqrt(jnp.mean(cq * cq, axis=-1, keepdims=True) + EPS) * gq_ref[...]
    ckvn = ckv * lax.rsqrt(jnp.mean(ckv * ckv, axis=-1, keepdims=True) + EPS) * gkv_ref[...]
    if emit_cache:
        ckv_ref[...] = ckvn
        kpe_ref[...] = kpe
    q = _dot(cqn.astype(BF16), wqb_ref[...])
    ckvb = ckvn.astype(BF16)
    kn = _dot(ckvb, wk_ref[...])
    v_ref[...] = _dot(ckvb, wv_ref[...]).astype(BF16)
    if rope:
        cos, sa, sb = cos_ref[...], sa_ref[...], sb_ref[...]
        kpe = _rope(kpe, cos, sa, sb)
    for hd in range(N_HEADS):
        sl = slice(hd * HEAD_PAD, (hd + 1) * HEAD_PAD)
        qh = q[:, sl]
        if rope:
            qh = _rope(qh, cos, sa, sb)
        q_ref[:, sl] = qh.astype(BF16)
        k_ref[:, sl] = (kn[:, sl] + kpe).astype(BF16)


def _mla_proj_call(x2d, mod3, prm, tables, *, seq_len, emit_cache, tm):
    t = x2d.shape[0]
    nb = mod3.shape[0]
    rows_per_b = t // nb
    rope = tables is not None

    def const(shape):
        return pl.BlockSpec(shape, lambda i: (0, 0))

    def rows(n):
        return pl.BlockSpec((tm, n), lambda i: (i, 0))

    in_specs = [rows(D_MODEL),
                pl.BlockSpec((1, 1, 6 * D_MODEL), lambda i: ((i * tm) // rows_per_b, 0, 0)),
                const((1, D_MODEL)), const(prm["w_in"].shape), const((1, Q_RANK)), const(prm["w_qb"].shape),
                const((1, KV_RANK)), const(prm["w_k"].shape), const(prm["w_v"].shape)]
    args = [x2d, mod3, prm["g_mix"], prm["w_in"], prm["g_q"], prm["w_qb"], prm["g_kv"], prm["w_k"], prm["w_v"]]
    if rope:
        tiles_per_seq = seq_len // tm
        in_specs += [pl.BlockSpec((tm, LANES), lambda i: (i % tiles_per_seq, 0))] * 3
        args += list(tables)
    out_specs = [rows(N_HEADS * HEAD_PAD), rows(N_HEADS * HEAD_PAD), rows(N_HEADS * V_HEAD)]
    out_shape = [jax.ShapeDtypeStruct((t, N_HEADS * HEAD_PAD), BF16),
                 jax.ShapeDtypeStruct((t, N_HEADS * HEAD_PAD), BF16),
                 jax.ShapeDtypeStruct((t, N_HEADS * V_HEAD), BF16)]
    if emit_cache:
        out_specs += [rows(KV_RANK), rows(LANES)]
        out_shape += [jax.ShapeDtypeStruct((t, KV_RANK), F32), jax.ShapeDtypeStruct((t, LANES), F32)]
    return pl.pallas_call(
        functools.partial(_mla_proj_kernel, rope=rope, emit_cache=emit_cache),
        grid=(t // tm,),
        in_specs=in_specs,
        out_specs=out_specs,
        out_shape=out_shape,
        compiler_params=_cparams(("arbitrary",)),
        name="l1_mla_proj",
    )(*args)


def _ctx_expand_kernel(ckv_ref, kpe_ref, wk_ref, wv_ref, k_ref, v_ref):
    ckvb = ckv_ref[...].astype(BF16)
    kn = _dot(ckvb, wk_ref[...])
    v_ref[...] = _dot(ckvb, wv_ref[...]).astype(BF16)
    kpe = kpe_ref[...]
    for hd in range(N_HEADS):
        sl = slice(hd * HEAD_PAD, (hd + 1) * HEAD_PAD)
        k_ref[:, sl] = (kn[:, sl] + kpe).astype(BF16)


def _ctx_expand_call(ckv2d, kpe2d, w_k, w_v, tm):
    t = ckv2d.shape[0]
    return pl.pallas_call(
        _ctx_expand_kernel,
        grid=(t // tm,),
        in_specs=[pl.BlockSpec((tm, KV_RANK), lambda i: (i, 0)),
                  pl.BlockSpec((tm, LANES), lambda i: (i, 0)),
                  pl.BlockSpec(w_k.shape, lambda i: (0, 0)),
                  pl.BlockSpec(w_v.shape, lambda i: (0, 0))],
        out_specs=[pl.BlockSpec((tm, N_HEADS * HEAD_PAD), lambda i: (i, 0)),
                   pl.BlockSpec((tm, N_HEADS * V_HEAD), lambda i: (i, 0))],
        out_shape=[jax.ShapeDtypeStruct((t, N_HEADS * HEAD_PAD), BF16),
                   jax.ShapeDtypeStruct((t, N_HEADS * V_HEAD), BF16)],
        compiler_params=_cparams(("arbitrary",)),
        name="l1_ctx_expand",
    )(ckv2d, kpe2d, w_k, w_v)


ATT_SCALE = (QK_NOPE + QK_ROPE) ** -0.5
_NT = (((1,), (1,)), ((), ()))


def _attn_kernel(*refs, n_kv, pairs):
    q_ref = refs[0]
    k_refs = refs[1:1 + n_kv]
    v_refs = refs[1 + n_kv:1 + 2 * n_kv]
    o_ref = refs[1 + 2 * n_kv]
    lane = lax.broadcasted_iota(jnp.int32, (q_ref.shape[1], LANES), 1)
    for p in range(pairs):
        outs = []
        for e in range(2):
            hd = 2 * p + e
            sl = slice(hd * HEAD_PAD, (hd + 1) * HEAD_PAD)
            q = q_ref[0, :, sl]
            ss = [lax.dot_general(q, k_ref[0, :, sl], _NT, preferred_element_type=F32) * ATT_SCALE
                  for k_ref in k_refs]
            m = ss[0].max(axis=-1, keepdims=True)
            for s in ss[1:]:
                m = jnp.maximum(m, s.max(axis=-1, keepdims=True))
            den = None
            acc = None
            for s, v_ref in zip(ss, v_refs):
                pr = jnp.exp(s - m)
                dsum = pr.sum(axis=-1, keepdims=True)
                den = dsum if den is None else den + dsum
                pv = _dot(pr.astype(BF16), v_ref[0, :, p * LANES:(p + 1) * LANES])
                acc = pv if acc is None else acc + pv
            outs.append(acc / den)
        o_ref[0, :, p * LANES:(p + 1) * LANES] = jnp.where(lane < V_HEAD, outs[0], outs[1]).astype(BF16)


def _attn_call(q3, ks, vs, *, tq, pairs):
    b, s, _ = q3.shape
    n_kv = len(ks)
    ngrp = (N_HEADS // 2) // pairs
    qw = 2 * pairs * HEAD_PAD
    vw = pairs * LANES
    in_specs = [pl.BlockSpec((1, tq, qw), lambda bi, g, i: (bi, i, g))]
    in_specs += [pl.BlockSpec((1, k.shape[1], qw), lambda bi, g, i: (bi, 0, g)) for k in ks]
    in_specs += [pl.BlockSpec((1, v.shape[1], vw), lambda bi, g, i: (bi, 0, g)) for v in vs]
    return pl.pallas_call(
        functools.partial(_attn_kernel, n_kv=n_kv, pairs=pairs),
        grid=(b, ngrp, s // tq),
        in_specs=in_specs,
        out_specs=pl.BlockSpec((1, tq, vw), lambda bi, g, i: (bi, i, g)),
        out_shape=jax.ShapeDtypeStruct((b, s, N_HEADS * V_HEAD), BF16),
        compiler_params=_cparams(("arbitrary", "arbitrary", "arbitrary")),
        name="l1_attention",
    )(q3, *ks, *vs)


def _block_diag_halves(w):
    halves = []
    for hf in range(2):
        m = jnp.zeros((256, 256), F32)
        for b in range(4):
            m = m.at[b * RNN_BLOCK:(b + 1) * RNN_BLOCK, b * RNN_BLOCK:(b + 1) * RNN_BLOCK].set(w[hf * 4 + b])
        halves.append(m)
    return jnp.stack(halves).astype(BF16)


def _rope_tables(seq_len):
    rows = seq_len // GRID_W
    row = jnp.broadcast_to(jnp.arange(rows)[:, None], (rows, GRID_W)).reshape(-1).astype(F32)
    col = jnp.broadcast_to(jnp.arange(GRID_W)[None, :], (rows, GRID_W)).reshape(-1).astype(F32)
    half = QK_ROPE // 2
    inv = ROPE_BASE ** (-jnp.arange(0, half, 2, dtype=F32) / half)
    ang_r, ang_c = row[:, None] * inv, col[:, None] * inv
    zeros = jnp.zeros((seq_len, 8), F32)
    cr, sr, cc, sc = jnp.cos(ang_r), jnp.sin(ang_r), jnp.cos(ang_c), jnp.sin(ang_c)
    lead = jnp.ones((seq_len, QK_NOPE), F32)
    tail = jnp.ones((seq_len, HEAD_PAD - QK_NOPE - QK_ROPE), F32)
    cos = jnp.concatenate([lead, cr, cr, cc, cc, tail], axis=1)
    sa = jnp.concatenate([0 * lead, zeros, sr, zeros, sc, 0 * tail], axis=1)
    sb = jnp.concatenate([0 * lead, -sr, zeros, -sc, zeros, 0 * tail], axis=1)
    return cos, sa, sb


def kernel(x_prompt, x_sample, state_l0_lru, cache_l1_ckv, cache_l1_kpe, c, c_ctx, l0_w_mod, l0_b_mod, l0_g_mix, l0_g_ffn, l0_w_in, l0_conv_w, l0_conv_b, l0_lru_w_a, l0_lru_b_a, l0_lru_w_i, l0_lru_b_i, l0_lru_lam, l0_pool_w, l0_pool_scale, l0_w_out, l0_ffn_w1, l0_ffn_w2, l1_w_mod, l1_b_mod, l1_g_mix, l1_g_ffn, l1_w_in, l1_g_q, l1_w_qb, l1_g_kv, l1_w_kvb, l1_w_out, l1_ffn_w1, l1_ffn_w2, g_final):
    bp, sp, d = x_prompt.shape
    bs, ss, _ = x_sample.shape
    past = cache_l1_ckv.shape[1]
    xp = x_prompt.reshape(bp * sp, d)
    xs = x_sample.reshape(bs * ss, d)

    cond8 = jnp.concatenate([c, c_ctx[None, :], jnp.zeros((SUBLANES - bs - 1, d), F32)], axis=0)
    m0 = _mod_call(cond8, l0_w_mod, l0_b_mod)
    m1 = _mod_call(cond8, l1_w_mod, l1_b_mod)
    mod_p = [m[bs:bs + 1].reshape(1, 1, 6 * d) for m in (m0, m1)]
    mod_s = [m[0:bs].reshape(bs, 1, 6 * d) for m in (m0, m1)]

    lru = dict(
        conv_w=l0_conv_w, conv_b=l0_conv_b.reshape(1, D_RNN),
        wa=[_block_diag_halves(l0_lru_w_a[i]) for i in range(2)],
        wi=[_block_diag_halves(l0_lru_w_i[i]) for i in range(2)],
        ba=[l0_lru_b_a[i].reshape(1, D_RNN) for i in range(2)],
        bi=[l0_lru_b_i[i].reshape(1, D_RNN) for i in range(2)],
        lam=[l0_lru_lam[i].reshape(1, D_RNN) for i in range(2)],
        pool_w=l0_pool_w.astype(BF16), pool_scale=l0_pool_scale.reshape(1, D_POOL))
    w_in0 = l0_w_in.astype(BF16)
    wo0 = l0_w_out.astype(BF16)
    wos0 = [wo0[:D_RNN], wo0[D_RNN:]]
    w1_0, w2_0 = l0_ffn_w1.astype(BF16), l0_ffn_w2.astype(BF16)

    def layer0(x2d, mod3, nseq, seq_len, h0f, h0b, ts):
        z = _l0_in_call(x2d, mod3, l0_g_mix, w_in0, tm=512)
        z3 = z.reshape(nseq, seq_len, 3 * D_RNN)
        hf, stf, ypool = _lru_call(z3, h0f, None, lru, reverse=False, ts=ts)
        yrnn, stb = _lru_call(z3, h0b, hf, lru, reverse=True, ts=ts)
        ys = [yrnn.reshape(-1, D_RNN), ypool.reshape(-1, D_POOL)]
        x2 = _ffn_call(x2d, ys, mod3, wos0, l0_g_ffn, w1_0, w2_0, None, tm=512)
        return x2, stf, stb

    zero_st = jnp.zeros((bp, 1, D_RNN), F32)
    xp, stf, stb = layer0(xp, mod_p[0], bp, sp, zero_st, zero_st, ts=sp)
    xs, _, _ = layer0(xs, mod_s[0], bs, ss, state_l0_lru[:, 0:1], state_l0_lru[:, 1:2], ts=512)
    new_lru = jnp.concatenate([stf, stb], axis=1)

    pad_pe = HEAD_PAD - QK_NOPE - QK_ROPE
    w_in1 = jnp.concatenate([l1_w_in[:, :Q_RANK + KV_RANK], jnp.zeros((d, QK_NOPE), F32),
                             l1_w_in[:, Q_RANK + KV_RANK:], jnp.zeros((d, pad_pe), F32)], axis=1).astype(BF16)
    wqb = l1_w_qb.reshape(Q_RANK, N_HEADS, QK_NOPE + QK_ROPE)
    wqb = jnp.pad(wqb, ((0, 0), (0, 0), (0, pad_pe))).reshape(Q_RANK, N_HEADS * HEAD_PAD).astype(BF16)
    wkvb = l1_w_kvb.reshape(KV_RANK, N_HEADS, QK_NOPE + V_HEAD)
    w_k = jnp.pad(wkvb[:, :, :QK_NOPE], ((0, 0), (0, 0), (0, HEAD_PAD - QK_NOPE)))
    w_k = w_k.reshape(KV_RANK, N_HEADS * HEAD_PAD).astype(BF16)
    w_v = wkvb[:, :, QK_NOPE:].reshape(KV_RANK, N_HEADS * V_HEAD).astype(BF16)
    mla = dict(g_mix=l1_g_mix.reshape(1, d), w_in=w_in1, g_q=l1_g_q.reshape(1, Q_RANK), w_qb=wqb,
               g_kv=l1_g_kv.reshape(1, KV_RANK), w_k=w_k, w_v=w_v)
    wo1 = [l1_w_out.astype(BF16)]
    w1_1, w2_1 = l1_ffn_w1.astype(BF16), l1_ffn_w2.astype(BF16)

    qp, kp, vp, ckv_new, kpe_new = _mla_proj_call(xp, mod_p[1], mla, None, seq_len=sp, emit_cache=True, tm=512)
    op = _attn_call(qp.reshape(bp, sp, -1), [kp.reshape(bp, sp, -1)], [vp.reshape(bp, sp, -1)],
                    tq=sp, pairs=N_HEADS // 2)
    y_prompt = _ffn_call(xp, [op.reshape(bp * sp, -1)], mod_p[1], wo1, l1_g_ffn, w1_1, w2_1, g_final, tm=512)

    tables = _rope_tables(ss)
    qs, ks, vs = _mla_proj_call(xs, mod_s[1], mla, tables, seq_len=ss, emit_cache=False, tm=512)
    kpe_ctx = jnp.pad(cache_l1_kpe.reshape(bs * past, QK_ROPE), ((0, 0), (QK_NOPE, pad_pe)))
    kc, vc = _ctx_expand_call(cache_l1_ckv.reshape(bs * past, KV_RANK), kpe_ctx, w_k, w_v, tm=512)
    os_ = _attn_call(qs.reshape(bs, ss, -1),
                     [ks.reshape(bs, ss, -1), kc.reshape(bs, past, -1)],
                     [vs.reshape(bs, ss, -1), vc.reshape(bs, past, -1)], tq=512, pairs=1)
    y_sample = _ffn_call(xs, [os_.reshape(bs * ss, -1)], mod_s[1], wo1, l1_g_ffn, w1_1, w2_1, g_final, tm=512)

    new_ckv = ckv_new.reshape(bp, sp, KV_RANK)
    new_kpe = kpe_new[:, QK_NOPE:QK_NOPE + QK_ROPE].reshape(bp, sp, QK_ROPE)
    return (y_prompt.reshape(bp, sp, d), y_sample.reshape(bs, ss, d), new_lru, new_ckv, new_kpe)
```

```python
import functools
import math

import jax
import jax.numpy as jnp
from jax import lax
from jax.experimental import pallas as pl
from jax.experimental.pallas import tpu as pltpu

F32 = jnp.float32
BF16 = jnp.bfloat16

D_MODEL = 1024
D_FF = 4 * D_MODEL
EPS = 1e-6
D_RNN = 512
RNN_BLOCKS = 8
RNN_BLOCK = 64
CONV_W = 4
LRU_C = 8.0
D_POOL = 512
POOL_WINDOWS = (2, 4, 8, 16)
POOL_GROUP = 128
N_HEADS = 16
QK_NOPE = 64
QK_ROPE = 32
V_HEAD = 64
Q_RANK = 384
KV_RANK = 256
ROPE_BASE = 10000.0
GRID_W = 64

LANES = 128
SUBLANES = 8
HEAD_PAD = 128
HALO = 8
VMEM_LIMIT = 52 * 1024 * 1024


def _cparams(sem):
    return pltpu.CompilerParams(dimension_semantics=sem, vmem_limit_bytes=VMEM_LIMIT)


def _dot(a, b):
    return jnp.dot(a, b, preferred_element_type=F32)


def _rms_mod(x, g, shift, scale):
    ms = jnp.mean(x * x, axis=-1, keepdims=True)
    y = x * lax.rsqrt(ms + EPS) * g
    return y * (1.0 + scale) + shift


def _mod_kernel(c_ref, w_ref, b_ref, o_ref):
    c = c_ref[...]
    s = c * jax.nn.sigmoid(c)
    o_ref[...] = _dot(s.astype(BF16), w_ref[...].astype(BF16)) + b_ref[...]


def _mod_call(cond8, w_mod, b_mod):
    n = w_mod.shape[1]
    tn = 1024
    return pl.pallas_call(
        _mod_kernel,
        grid=(n // tn,),
        in_specs=[
            pl.BlockSpec((SUBLANES, D_MODEL), lambda j: (0, 0)),
            pl.BlockSpec((D_MODEL, tn), lambda j: (0, j)),
            pl.BlockSpec((1, tn), lambda j: (0, j)),
        ],
        out_specs=pl.BlockSpec((SUBLANES, tn), lambda j: (0, j)),
        out_shape=jax.ShapeDtypeStruct((SUBLANES, n), F32),
        compiler_params=_cparams(("arbitrary",)),
        name="adaln_mod",
    )(cond8, w_mod, b_mod.reshape(1, n))


def _l0_in_kernel(x_ref, mod_ref, g_ref, w_ref, z_ref):
    shift = mod_ref[0, :, 0:D_MODEL]
    scale = mod_ref[0, :, D_MODEL:2 * D_MODEL]
    h = _rms_mod(x_ref[...], g_ref[...], shift, scale)
    z_ref[...] = _dot(h.astype(BF16), w_ref[...])


def _l0_in_call(x2d, mod3, g, w_bf, tm):
    t = x2d.shape[0]
    nb = mod3.shape[0]
    rows_per_b = t // nb
    n = w_bf.shape[1]
    return pl.pallas_call(
        _l0_in_kernel,
        grid=(t // tm,),
        in_specs=[
            pl.BlockSpec((tm, D_MODEL), lambda i: (i, 0)),
            pl.BlockSpec((1, 1, 6 * D_MODEL), lambda i: ((i * tm) // rows_per_b, 0, 0)),
            pl.BlockSpec((1, D_MODEL), lambda i: (0, 0)),
            pl.BlockSpec((D_MODEL, n), lambda i: (0, 0)),
        ],
        out_specs=pl.BlockSpec((tm, n), lambda i: (i, 0)),
        out_shape=jax.ShapeDtypeStruct((t, n), F32),
        compiler_params=_cparams(("arbitrary",)),
        name="l0_in_proj",
    )(x2d, mod3, g.reshape(1, D_MODEL), w_bf)


def _scan_tile(a, u, carry_row, reverse, a_s, u_s, ts):
    g = ts // SUBLANES
    a3 = a.reshape(g, SUBLANES, D_RNN)
    u3 = u.reshape(g, SUBLANES, D_RNN)
    row = lax.broadcasted_iota(jnp.int32, (g, SUBLANES, D_RNN), 1)
    for d in (1, 2, 4):
        if reverse:
            sh = SUBLANES - d
            m = row < SUBLANES - d
        else:
            sh = d
            m = row >= d
        a_sh = pltpu.roll(a3, sh, axis=1)
        u_sh = pltpu.roll(u3, sh, axis=1)
        u3 = u3 + a3 * jnp.where(m, u_sh, 0.0)
        a3 = a3 * jnp.where(m, a_sh, 1.0)
    a_s[...] = a3.reshape(ts, D_RNN)
    u_s[...] = u3.reshape(ts, D_RNN)
    edge = 0 if reverse else SUBLANES - 1

    def body(k, c):
        gi = (g - 1 - k) if reverse else k
        r0 = pl.multiple_of(gi * SUBLANES, SUBLANES)
        h = u_s[pl.ds(r0, SUBLANES), :] + a_s[pl.ds(r0, SUBLANES), :] * c
        u_s[pl.ds(r0, SUBLANES), :] = h
        return jnp.broadcast_to(h[edge:edge + 1, :], (SUBLANES, D_RNN))

    c0 = jnp.broadcast_to(carry_row, (SUBLANES, D_RNN))
    c_last = lax.fori_loop(0, g, body, c0, unroll=4)
    return c_last[0:1, :]


def _lru_kernel(*refs, reverse, with_pool, ts, nt, seq_len):
    if with_pool:
        (xr_ref, xrp_ref, xrn_ref, xp_ref, xpp_ref, xpn_ref, h0_ref, cw_ref, cb_ref, wa_ref, ba_ref, wi_ref,
         bi_ref, lam_ref, pw_ref, ps_ref, h_ref, st_ref, yp_ref, ext_s, a_s, u_s, carry_s, b2_s, b4_s, b8_s) = refs
    else:
        (xr_ref, xrp_ref, xrn_ref, gate_ref, hf_ref, h0_ref, cw_ref, cb_ref, wa_ref, ba_ref, wi_ref,
         bi_ref, lam_ref, y_ref, st_ref, ext_s, a_s, u_s, carry_s) = refs
    j = pl.program_id(1)
    t = (nt - 1 - j) if reverse else j

    @pl.when(j == 0)
    def _():
        carry_s[...] = h0_ref[0]

    def fill_ext(cur_ref, prev_ref, next_ref):
        ext_s[0:HALO, :] = jnp.where(t > 0, prev_ref[0], 0.0)
        ext_s[HALO:HALO + ts, :] = cur_ref[0]
        ext_s[HALO + ts:HALO + ts + HALO, :] = jnp.where(t < nt - 1, next_ref[0], 0.0)

    fill_ext(xr_ref, xrp_ref, xrn_ref)
    xc = cb_ref[...]
    for k in range(CONV_W):
        off = HALO - CONV_W // 2 + k
        xc = xc + ext_s[off:off + ts, :] * cw_ref[k:k + 1, :]

    xcb = xc.astype(BF16)
    half = D_RNN // 2
    ga = jnp.concatenate([_dot(xcb[:, :half], wa_ref[0]), _dot(xcb[:, half:], wa_ref[1])], axis=-1)
    gi = jnp.concatenate([_dot(xcb[:, :half], wi_ref[0]), _dot(xcb[:, half:], wi_ref[1])], axis=-1)
    r = jax.nn.sigmoid(ga + ba_ref[...])
    i = jax.nn.sigmoid(gi + bi_ref[...])
    log_a = (LRU_C * r) * jax.nn.log_sigmoid(lam_ref[...])
    a = jnp.exp(log_a)
    u = jnp.sqrt(-jnp.tanh(log_a) * (a * a + 1.0)) * (i * xc)

    c_last = _scan_tile(a, u, carry_s[...], reverse, a_s, u_s, ts)
    carry_s[...] = c_last
    st_ref[0] = c_last
    h = u_s[...]

    if not with_pool:
        y_ref[0] = ((hf_ref[0] + h) * jax.nn.gelu(gate_ref[0])).astype(BF16)
        return

    h_ref[0] = h

    fill_ext(xp_ref, xpp_ref, xpn_ref)
    n2 = ts + 2 * HALO
    b2_s[1:n2, :] = ext_s[1:n2, :] + ext_s[0:n2 - 1, :]
    b4_s[2:n2 - 1, :] = b2_s[3:n2, LANES:] + b2_s[1:n2 - 2, LANES:]
    b8_s[4:n2 - 3, :] = b4_s[6:n2 - 1, LANES:] + b4_s[2:n2 - 5, LANES:]
    s16 = b8_s[HALO + 4:HALO + 4 + ts, LANES:] + b8_s[HALO - 4:HALO - 4 + ts, LANES:]
    sums = (b2_s[HALO:HALO + ts, 0:LANES], b4_s[HALO:HALO + ts, 0:LANES], b8_s[HALO:HALO + ts, 0:LANES], s16)
    pos = t * ts + lax.broadcasted_iota(jnp.int32, (ts, LANES), 0)
    for gidx, w in enumerate(POOL_WINDOWS):
        left = w // 2
        right = w - 1 - left
        lo = jnp.maximum(pos - left, 0)
        hi = jnp.minimum(pos + right, seq_len - 1) + 1
        cnt = (hi - lo).astype(F32)
        xg = xp_ref[0, :, gidx * LANES:(gidx + 1) * LANES]
        dlt = sums[gidx] / cnt - xg
        yg = _dot(dlt.astype(BF16), pw_ref[gidx]) * ps_ref[:, gidx * LANES:(gidx + 1) * LANES]
        yp_ref[0, :, gidx * LANES:(gidx + 1) * LANES] = yg.astype(BF16)


def _lru_call(z3, h0, hf, prm, *, reverse, ts):
    nseq, seq_len, _ = z3.shape
    nt = seq_len // ts
    tb = ts // HALO
    nhb = seq_len // HALO
    d = 1 if reverse else 0
    with_pool = not reverse

    def tpos(j):
        return (nt - 1 - j) if reverse else j

    def cur(col):
        return pl.BlockSpec((1, ts, D_RNN), lambda s, j: (s, tpos(j), col))

    def prev(col):
        return pl.BlockSpec((1, HALO, D_RNN), lambda s, j: (s, jnp.maximum(tpos(j) * tb - 1, 0), col))

    def nxt(col):
        return pl.BlockSpec((1, HALO, D_RNN), lambda s, j: (s, jnp.minimum((tpos(j) + 1) * tb, nhb - 1), col))

    def const(shape):
        nd = len(shape)
        return pl.BlockSpec(shape, lambda s, j: (0,) * nd)

    tile_out = pl.BlockSpec((1, ts, D_RNN), lambda s, j: (s, tpos(j), 0))
    st_spec = pl.BlockSpec((1, 1, D_RNN), lambda s, j: (s, 0, 0))
    w_specs = [const((CONV_W, D_RNN)), const((1, D_RNN)), const((2, 256, 256)), const((1, D_RNN)),
               const((2, 256, 256)), const((1, D_RNN)), const((1, D_RNN))]
    w_args = [prm["conv_w"], prm["conv_b"], prm["wa"][d], prm["ba"][d], prm["wi"][d], prm["bi"][d], prm["lam"][d]]
    scratch = [pltpu.VMEM((ts + 2 * HALO, D_RNN), F32), pltpu.VMEM((ts, D_RNN), F32),
               pltpu.VMEM((ts, D_RNN), F32), pltpu.VMEM((1, D_RNN), F32)]
    kern = functools.partial(_lru_kernel, reverse=reverse, with_pool=with_pool, ts=ts, nt=nt, seq_len=seq_len)
    if with_pool:
        in_specs = [cur(0), prev(0), nxt(0), cur(2), prev(2), nxt(2), st_spec] + w_specs + [
            const((4, POOL_GROUP, POOL_GROUP)), const((1, D_POOL))]
        args = [z3, z3, z3, z3, z3, z3, h0] + w_args + [prm["pool_w"], prm["pool_scale"]]
        out_specs = [tile_out, st_spec, tile_out]
        out_shape = [jax.ShapeDtypeStruct((nseq, seq_len, D_RNN), F32),
                     jax.ShapeDtypeStruct((nseq, 1, D_RNN), F32),
                     jax.ShapeDtypeStruct((nseq, seq_len, D_POOL), BF16)]
        n2 = ts + 2 * HALO
        scratch = scratch + [pltpu.VMEM((n2, D_POOL), F32), pltpu.VMEM((n2, D_POOL - LANES), F32),
                             pltpu.VMEM((n2, D_POOL - 2 * LANES), F32)]
        name = "l0_lru_fwd_pool"
    else:
        in_specs = [cur(0), prev(0), nxt(0), cur(1), tile_out, st_spec] + w_specs
        args = [z3, z3, z3, z3, hf, h0] + w_args
        out_specs = [tile_out, st_spec]
        out_shape = [jax.ShapeDtypeStruct((nseq, seq_len, D_RNN), BF16),
                     jax.ShapeDtypeStruct((nseq, 1, D_RNN), F32)]
        name = "l0_lru_bwd"
    return pl.pallas_call(
        kern,
        grid=(nseq, nt),
        in_specs=in_specs,
        out_specs=out_specs,
        out_shape=out_shape,
        scratch_shapes=scratch,
        compiler_params=_cparams(("arbitrary", "arbitrary")),
        name=name,
    )(*args)


FF_CHUNK = 1024


def _ffn_kernel(*refs, n_y, final_norm):
    x_ref = refs[0]
    y_refs = refs[1:1 + n_y]
    mod_ref = refs[1 + n_y]
    wo_refs = refs[2 + n_y:2 + 2 * n_y]
    g_ref, w1_ref, w2_ref = refs[2 + 2 * n_y:5 + 2 * n_y]
    rest = refs[5 + 2 * n_y:]
    if final_norm:
        gf_ref, o_ref = rest
    else:
        (o_ref,) = rest

    def mod(k):
        return mod_ref[0, :, k * D_MODEL:(k + 1) * D_MODEL]

    mix = _dot(y_refs[0][...], wo_refs[0][...])
    for k in range(1, n_y):
        mix = mix + _dot(y_refs[k][...], wo_refs[k][...])
    x1 = x_ref[...] + mod(2) * mix
    hn = _rms_mod(x1, g_ref[...], mod(3), mod(4)).astype(BF16)
    acc = None
    for c in range(D_FF // FF_CHUNK):
        hc = _dot(hn, w1_ref[:, c * FF_CHUNK:(c + 1) * FF_CHUNK])
        hc = jnp.square(jnp.maximum(hc, 0.0)).astype(BF16)
        part = _dot(hc, w2_ref[c * FF_CHUNK:(c + 1) * FF_CHUNK, :])
        acc = part if acc is None else acc + part
    x2 = x1 + mod(5) * acc
    if final_norm:
        ms = jnp.mean(x2 * x2, axis=-1, keepdims=True)
        x2 = x2 * lax.rsqrt(ms + EPS) * gf_ref[...]
    o_ref[...] = x2


def _ffn_call(x2d, ys, mod3, wos, g, w1, w2, g_final, tm):
    t = x2d.shape[0]
    nb = mod3.shape[0]
    rows_per_b = t // nb
    n_y = len(ys)
    final_norm = g_final is not None

    def const(shape):
        return pl.BlockSpec(shape, lambda i: (0, 0), pipeline_mode=pl.Buffered(1))

    in_specs = [pl.BlockSpec((tm, D_MODEL), lambda i: (i, 0))]
    in_specs += [pl.BlockSpec((tm, y.shape[1]), lambda i: (i, 0)) for y in ys]
    in_specs += [pl.BlockSpec((1, 1, 6 * D_MODEL), lambda i: ((i * tm) // rows_per_b, 0, 0))]
    in_specs += [const(w.shape) for w in wos]
    in_specs += [const((1, D_MODEL)), const((D_MODEL, D_FF)), const((D_FF, D_MODEL))]
    args = [x2d, *ys, mod3, *wos, g.reshape(1, D_MODEL), w1, w2]
    if final_norm:
        in_specs.append(const((1, D_MODEL)))
        args.append(g_final.reshape(1, D_MODEL))
    return pl.pallas_call(
        functools.partial(_ffn_kernel, n_y=n_y, final_norm=final_norm),
        grid=(t // tm,),
        in_specs=in_specs,
        out_specs=pl.BlockSpec((tm, D_MODEL), lambda i: (i, 0)),
        out_shape=jax.ShapeDtypeStruct((t, D_MODEL), F32),
        compiler_params=_cparams(("arbitrary",)),
        name="mix_out_ffn",
    )(*args)


def _rope(x, cos, sa, sb):
    return x * cos + pltpu.roll(x, 8, axis=1) * sa + pltpu.roll(x, LANES - 8, axis=1) * sb


def _mla_proj_kernel(*refs, rope, emit_cache):
    x_ref, mod_ref, g_ref, win_ref, gq_ref, wqb_ref, gkv_ref, wk_ref, wv_ref, vones_ref = refs[:10]
    rest = refs[10:]
    if rope:
        cos_ref, sa_ref, sb_ref = rest[:3]
        rest = rest[3:]
    q_ref, k_ref, v_ref = rest[:3]
    if emit_cache:
        ckv_ref, kpe_ref = rest[3:]

    shift = mod_ref[0, :, 0:D_MODEL]
    scale = mod_ref[0, :, D_MODEL:2 * D_MODEL]
    h = _rms_mod(x_ref[...], g_ref[...], shift, scale).astype(BF16)
    z = _dot(h, win_ref[...])
    cq = z[:, :Q_RANK]
    ckv = z[:, Q_RANK:Q_RANK + KV_RANK]
    kpe = z[:, Q_RANK + KV_RANK:]
    cqn = cq * lax.rsqrt(jnp.mean(cq * cq, axis=-1, keepdims=True) + EPS) * gq_ref[...]
    ckvn = ckv * lax.rsqrt(jnp.mean(ckv * ckv, axis=-1, keepdims=True) + EPS) * gkv_ref[...]
    if emit_cache:
        ckv_ref[...] = ckvn
        kpe_ref[...] = kpe
    q = _dot(cqn.astype(BF16), wqb_ref[...]) * Q_PRESCALE
    ckvb = ckvn.astype(BF16)
    kn = _dot(ckvb, wk_ref[...])
    v_ref[...] = (_dot(ckvb, wv_ref[...]) + vones_ref[...]).astype(BF16)
    if rope:
        cos, sa, sb = cos_ref[...], sa_ref[...], sb_ref[...]
        kpe = _rope(kpe, cos, sa, sb)
    for hd in range(N_HEADS):
        sl = slice(hd * HEAD_PAD, (hd + 1) * HEAD_PAD)
        qh = q[:, sl]
        if rope:
            qh = _rope(qh, cos, sa, sb)
        q_ref[:, sl] = qh.astype(BF16)
        k_ref[:, sl] = (kn[:, sl] + kpe).astype(BF16)


def _mla_proj_call(x2d, mod3, prm, tables, *, seq_len, extra_kv_rows, emit_cache, tm):
    t = x2d.shape[0]
    nb = mod3.shape[0]
    rows_per_b = t // nb
    rope = tables is not None
    tiles_per_seq = seq_len // tm
    kv_tiles_per_seq = (seq_len + extra_kv_rows) // tm
    t_kv = (t // seq_len) * (seq_len + extra_kv_rows)
    hw = N_HEADS * HEAD_PAD

    def const(shape):
        return pl.BlockSpec(shape, lambda i: (0, 0))

    def rows(n):
        return pl.BlockSpec((tm, n), lambda i: (i, 0))

    kv_rows = pl.BlockSpec((tm, hw), lambda i: ((i // tiles_per_seq) * kv_tiles_per_seq + i % tiles_per_seq, 0))
    in_specs = [rows(D_MODEL),
                pl.BlockSpec((1, 1, 6 * D_MODEL), lambda i: ((i * tm) // rows_per_b, 0, 0)),
                const((1, D_MODEL)), const(prm["w_in"].shape), const((1, Q_RANK)), const(prm["w_qb"].shape),
                const((1, KV_RANK)), const(prm["w_k"].shape), const(prm["w_v"].shape), const((1, hw))]
    args = [x2d, mod3, prm["g_mix"], prm["w_in"], prm["g_q"], prm["w_qb"], prm["g_kv"], prm["w_k"], prm["w_v"],
            prm["v_ones"]]
    if rope:
        in_specs += [pl.BlockSpec((tm, LANES), lambda i: (i % tiles_per_seq, 0))] * 3
        args += list(tables)
    out_specs = [rows(hw), kv_rows, kv_rows]
    out_shape = [jax.ShapeDtypeStruct((t, hw), BF16),
                 jax.ShapeDtypeStruct((t_kv, hw), BF16),
                 jax.ShapeDtypeStruct((t_kv, hw), BF16)]
    if emit_cache:
        out_specs += [rows(KV_RANK), rows(LANES)]
        out_shape += [jax.ShapeDtypeStruct((t, KV_RANK), F32), jax.ShapeDtypeStruct((t, LANES), F32)]
    return pl.pallas_call(
        functools.partial(_mla_proj_kernel, rope=rope, emit_cache=emit_cache),
        grid=(t // tm,),
        in_specs=in_specs,
        out_specs=out_specs,
        out_shape=out_shape,
        compiler_params=_cparams(("arbitrary",)),
        name="l1_mla_proj",
    )(*args)


def _ctx_expand_kernel(ckv_ref, kpe_ref, wk_ref, wv_ref, vones_ref, k_in_ref, v_in_ref, k_ref, v_ref):
    del k_in_ref, v_in_ref
    ckvb = ckv_ref[...].astype(BF16)
    kn = _dot(ckvb, wk_ref[...])
    v_ref[...] = (_dot(ckvb, wv_ref[...]) + vones_ref[...]).astype(BF16)
    kpe = kpe_ref[...]
    for hd in range(N_HEADS):
        sl = slice(hd * HEAD_PAD, (hd + 1) * HEAD_PAD)
        k_ref[:, sl] = (kn[:, sl] + kpe).astype(BF16)


def _ctx_expand_call(ckv2d, kpe2d, prm, k_all, v_all, *, seq_len, past):
    hw = N_HEADS * HEAD_PAD
    nseq = ckv2d.shape[0] // past
    tiles_per_seq = (seq_len + past) // past
    out_spec = pl.BlockSpec((past, hw), lambda b: (b * tiles_per_seq + tiles_per_seq - 1, 0))
    const = lambda a: pl.BlockSpec(a.shape, lambda b: (0, 0))
    return pl.pallas_call(
        _ctx_expand_kernel,
        grid=(nseq,),
        in_specs=[pl.BlockSpec((past, KV_RANK), lambda b: (b, 0)),
                  pl.BlockSpec((past, LANES), lambda b: (b, 0)),
                  const(prm["w_k"]), const(prm["w_v"]), const(prm["v_ones"]),
                  pl.BlockSpec(memory_space=pl.ANY), pl.BlockSpec(memory_space=pl.ANY)],
        out_specs=[out_spec, out_spec],
        out_shape=[jax.ShapeDtypeStruct(k_all.shape, BF16), jax.ShapeDtypeStruct(v_all.shape, BF16)],
        input_output_aliases={5: 0, 6: 1},
        compiler_params=_cparams(("arbitrary",)),
        name="l1_ctx_expand",
    )(ckv2d, kpe2d, prm["w_k"], prm["w_v"], prm["v_ones"], k_all, v_all)


Q_PRESCALE = (QK_NOPE + QK_ROPE) ** -0.5 * math.log2(math.e)
_NT = (((1,), (1,)), ((), ()))
KEY_CHUNK = 256


def _pair_out(acc_even, acc_odd):
    lane = lax.broadcasted_iota(jnp.int32, acc_even.shape, 1)
    first = lane < V_HEAD
    num = jnp.where(first, acc_even, acc_odd)
    den = jnp.where(first, pltpu.roll(acc_even, V_HEAD, axis=1), pltpu.roll(acc_odd, V_HEAD, axis=1))
    return (num / den).astype(BF16)


def _attn_ctx_kernel(q_ref, k_ref, v_ref, o_ref, *, pairs):
    for p in range(pairs):
        accs = []
        for e in range(2):
            sl = slice((2 * p + e) * HEAD_PAD, (2 * p + e + 1) * HEAD_PAD)
            s = lax.dot_general(q_ref[0, :, sl], k_ref[0, :, sl], _NT, preferred_element_type=F32)
            pr = jnp.exp2(s - s.max(axis=-1, keepdims=True))
            accs.append(_dot(pr.astype(BF16), v_ref[0, :, sl]))
        o_ref[0, :, p * LANES:(p + 1) * LANES] = _pair_out(accs[0], accs[1])


def _attn_ctx_call(q3, k3, v3):
    b, s, _ = q3.shape
    spec = lambda w: pl.BlockSpec((1, s, w), lambda bi: (bi, 0, 0))
    hw = N_HEADS * HEAD_PAD
    return pl.pallas_call(
        functools.partial(_attn_ctx_kernel, pairs=N_HEADS // 2),
        grid=(b,),
        in_specs=[spec(hw), spec(hw), spec(hw)],
        out_specs=spec(N_HEADS * V_HEAD),
        out_shape=jax.ShapeDtypeStruct((b, s, N_HEADS * V_HEAD), BF16),
        compiler_params=_cparams(("arbitrary",)),
        name="l1_attention_ctx",
    )(q3, k3, v3)


def _attn_lat_kernel(q_ref, k_ref, v_ref, o_ref, s_even, s_odd, m_even, m_odd, p_scr):
    n = pl.program_id(0)
    n_keys = k_ref.shape[1]

    @pl.when(n == 0)
    def _():
        s_odd[...] = jnp.zeros(s_odd.shape, F32)
        m_odd[...] = jnp.zeros(m_odd.shape, F32)

    def step(s_w, m_w, s_r, m_r):
        for e in range(2):
            sl = slice(e * HEAD_PAD, (e + 1) * HEAD_PAD)
            s = lax.dot_general(q_ref[0, :, sl], k_ref[0, :, sl], _NT, preferred_element_type=F32)
            s_w[e] = s
            m_w[e] = jnp.broadcast_to(s.max(axis=-1, keepdims=True), m_w.shape[1:])
        accs = []
        for e in range(2):
            sl = slice(e * HEAD_PAD, (e + 1) * HEAD_PAD)
            m = m_r[e]
            for t in range(n_keys // LANES):
                tl = slice(t * LANES, (t + 1) * LANES)
                p_scr[e, :, tl] = jnp.exp2(s_r[e, :, tl] - m).astype(BF16)
            accs.append(_dot(p_scr[e], v_ref[0, :, sl]))
        o_ref[0] = _pair_out(accs[0], accs[1])

    @pl.when(n % 2 == 0)
    def _():
        step(s_even, m_even, s_odd, m_odd)

    @pl.when(n % 2 == 1)
    def _():
        step(s_odd, m_odd, s_even, m_even)


def _attn_lat_call(q3, k3, v3, *, tq):
    b, s, _ = q3.shape
    sk = k3.shape[1]
    npair = N_HEADS // 2
    nq = s // tq
    n_items = b * npair * nq

    def item(n):
        return n // (npair * nq), n % nq, (n // nq) % npair

    def cur(n):
        return item(jnp.minimum(n, n_items - 1))

    def lag(n):
        return item(jnp.maximum(n - 1, 0))

    qw = 2 * HEAD_PAD
    in_specs = [
        pl.BlockSpec((1, tq, qw), lambda n: cur(n)),
        pl.BlockSpec((1, sk, qw), lambda n: (cur(n)[0], 0, cur(n)[2])),
        pl.BlockSpec((1, sk, qw), lambda n: (lag(n)[0], 0, lag(n)[2])),
    ]
    return pl.pallas_call(
        _attn_lat_kernel,
        grid=(n_items + 1,),
        in_specs=in_specs,
        out_specs=pl.BlockSpec((1, tq, LANES), lambda n: lag(n)),
        out_shape=jax.ShapeDtypeStruct((b, s, N_HEADS * V_HEAD), BF16),
        scratch_shapes=[pltpu.VMEM((2, tq, sk), F32)] * 2 + [pltpu.VMEM((2, tq, LANES), F32)] * 2
        + [pltpu.VMEM((2, tq, sk), BF16)],
        compiler_params=_cparams(("arbitrary",)),
        name="l1_attention_lat",
    )(q3, k3, v3)


def _block_diag_halves(w):
    halves = []
    for hf in range(2):
        m = jnp.zeros((256, 256), F32)
        for b in range(4):
            m = m.at[b * RNN_BLOCK:(b + 1) * RNN_BLOCK, b * RNN_BLOCK:(b + 1) * RNN_BLOCK].set(w[hf * 4 + b])
        halves.append(m)
    return jnp.stack(halves).astype(BF16)


def _rope_tables(seq_len):
    rows = seq_len // GRID_W
    row = jnp.broadcast_to(jnp.arange(rows)[:, None], (rows, GRID_W)).reshape(-1).astype(F32)
    col = jnp.broadcast_to(jnp.arange(GRID_W)[None, :], (rows, GRID_W)).reshape(-1).astype(F32)
    half = QK_ROPE // 2
    inv = ROPE_BASE ** (-jnp.arange(0, half, 2, dtype=F32) / half)
    ang_r, ang_c = row[:, None] * inv, col[:, None] * inv
    zeros = jnp.zeros((seq_len, 8), F32)
    cr, sr, cc, sc = jnp.cos(ang_r), jnp.sin(ang_r), jnp.cos(ang_c), jnp.sin(ang_c)
    lead = jnp.ones((seq_len, QK_NOPE), F32)
    tail = jnp.ones((seq_len, HEAD_PAD - QK_NOPE - QK_ROPE), F32)
    cos = jnp.concatenate([lead, cr, cr, cc, cc, tail], axis=1)
    sa = jnp.concatenate([0 * lead, zeros, sr, zeros, sc, 0 * tail], axis=1)
    sb = jnp.concatenate([0 * lead, -sr, zeros, -sc, zeros, 0 * tail], axis=1)
    return cos, sa, sb


def kernel(x_prompt, x_sample, state_l0_lru, cache_l1_ckv, cache_l1_kpe, c, c_ctx, l0_w_mod, l0_b_mod, l0_g_mix, l0_g_ffn, l0_w_in, l0_conv_w, l0_conv_b, l0_lru_w_a, l0_lru_b_a, l0_lru_w_i, l0_lru_b_i, l0_lru_lam, l0_pool_w, l0_pool_scale, l0_w_out, l0_ffn_w1, l0_ffn_w2, l1_w_mod, l1_b_mod, l1_g_mix, l1_g_ffn, l1_w_in, l1_g_q, l1_w_qb, l1_g_kv, l1_w_kvb, l1_w_out, l1_ffn_w1, l1_ffn_w2, g_final):
    bp, sp, d = x_prompt.shape
    bs, ss, _ = x_sample.shape
    past = cache_l1_ckv.shape[1]
    xp = x_prompt.reshape(bp * sp, d)
    xs = x_sample.reshape(bs * ss, d)

    cond8 = jnp.concatenate([c, c_ctx[None, :], jnp.zeros((SUBLANES - bs - 1, d), F32)], axis=0)
    m0 = _mod_call(cond8, l0_w_mod, l0_b_mod)
    m1 = _mod_call(cond8, l1_w_mod, l1_b_mod)
    mod_p = [m[bs:bs + 1].reshape(1, 1, 6 * d) for m in (m0, m1)]
    mod_s = [m[0:bs].reshape(bs, 1, 6 * d) for m in (m0, m1)]

    lru = dict(
        conv_w=l0_conv_w, conv_b=l0_conv_b.reshape(1, D_RNN),
        wa=[_block_diag_halves(l0_lru_w_a[i]) for i in range(2)],
        wi=[_block_diag_halves(l0_lru_w_i[i]) for i in range(2)],
        ba=[l0_lru_b_a[i].reshape(1, D_RNN) for i in range(2)],
        bi=[l0_lru_b_i[i].reshape(1, D_RNN) for i in range(2)],
        lam=[l0_lru_lam[i].reshape(1, D_RNN) for i in range(2)],
        pool_w=l0_pool_w.astype(BF16), pool_scale=l0_pool_scale.reshape(1, D_POOL))
    w_in0 = l0_w_in.astype(BF16)
    wo0 = l0_w_out.astype(BF16)
    wos0 = [wo0[:D_RNN], wo0[D_RNN:]]
    w1_0, w2_0 = l0_ffn_w1.astype(BF16), l0_ffn_w2.astype(BF16)

    def layer0(x2d, mod3, nseq, seq_len, h0f, h0b, ts):
        z = _l0_in_call(x2d, mod3, l0_g_mix, w_in0, tm=512)
        z3 = z.reshape(nseq, seq_len, 3 * D_RNN)
        hf, stf, ypool = _lru_call(z3, h0f, None, lru, reverse=False, ts=ts)
        yrnn, stb = _lru_call(z3, h0b, hf, lru, reverse=True, ts=ts)
        ys = [yrnn.reshape(-1, D_RNN), ypool.reshape(-1, D_POOL)]
        x2 = _ffn_call(x2d, ys, mod3, wos0, l0_g_ffn, w1_0, w2_0, None, tm=512)
        return x2, stf, stb

    zero_st = jnp.zeros((bp, 1, D_RNN), F32)
    xp, stf, stb = layer0(xp, mod_p[0], bp, sp, zero_st, zero_st, ts=sp)
    xs, _, _ = layer0(xs, mod_s[0], bs, ss, state_l0_lru[:, 0:1], state_l0_lru[:, 1:2], ts=512)
    new_lru = jnp.concatenate([stf, stb], axis=1)

    pad_pe = HEAD_PAD - QK_NOPE - QK_ROPE
    w_in1 = jnp.concatenate([l1_w_in[:, :Q_RANK + KV_RANK], jnp.zeros((d, QK_NOPE), F32),
                             l1_w_in[:, Q_RANK + KV_RANK:], jnp.zeros((d, pad_pe), F32)], axis=1).astype(BF16)
    wqb = l1_w_qb.reshape(Q_RANK, N_HEADS, QK_NOPE + QK_ROPE)
    wqb = jnp.pad(wqb, ((0, 0), (0, 0), (0, pad_pe))).reshape(Q_RANK, N_HEADS * HEAD_PAD).astype(BF16)
    wkvb = l1_w_kvb.reshape(KV_RANK, N_HEADS, QK_NOPE + V_HEAD)
    w_k = jnp.pad(wkvb[:, :, :QK_NOPE], ((0, 0), (0, 0), (0, HEAD_PAD - QK_NOPE)))
    w_k = w_k.reshape(KV_RANK, N_HEADS * HEAD_PAD).astype(BF16)
    wv = wkvb[:, :, QK_NOPE:].reshape(KV_RANK, N_HEADS // 2, 2, V_HEAD)
    zv = jnp.zeros((KV_RANK, N_HEADS // 2, V_HEAD), F32)
    w_v = jnp.stack([wv[:, :, 0], zv, zv, wv[:, :, 1]], axis=2).reshape(KV_RANK, N_HEADS * HEAD_PAD).astype(BF16)
    one = jnp.ones((N_HEADS // 2, V_HEAD), F32)
    v_ones = jnp.stack([0 * one, one, one, 0 * one], axis=1).reshape(1, N_HEADS * HEAD_PAD)
    mla = dict(g_mix=l1_g_mix.reshape(1, d), w_in=w_in1, g_q=l1_g_q.reshape(1, Q_RANK), w_qb=wqb,
               g_kv=l1_g_kv.reshape(1, KV_RANK), w_k=w_k, w_v=w_v, v_ones=v_ones)
    wo1 = [l1_w_out.astype(BF16)]
    w1_1, w2_1 = l1_ffn_w1.astype(BF16), l1_ffn_w2.astype(BF16)

    qp, kp, vp, ckv_new, kpe_new = _mla_proj_call(xp, mod_p[1], mla, None, seq_len=bp * sp, extra_kv_rows=0,
                                                  emit_cache=True, tm=512)
    op = _attn_ctx_call(qp.reshape(bp, sp, -1), kp.reshape(bp, sp, -1), vp.reshape(bp, sp, -1))
    y_prompt = _ffn_call(xp, [op.reshape(bp * sp, -1)], mod_p[1], wo1, l1_g_ffn, w1_1, w2_1, g_final, tm=512)

    tables = _rope_tables(ss)
    qs, ks, vs = _mla_proj_call(xs, mod_s[1], mla, tables, seq_len=ss, extra_kv_rows=past, emit_cache=False,
                                tm=512)
    kpe_ctx = jnp.pad(cache_l1_kpe.reshape(bs * past, QK_ROPE), ((0, 0), (QK_NOPE, pad_pe)))
    ks, vs = _ctx_expand_call(cache_l1_ckv.reshape(bs * past, KV_RANK), kpe_ctx, mla, ks, vs,
                              seq_len=ss, past=past)
    os_ = _attn_lat_call(qs.reshape(bs, ss, -1), ks.reshape(bs, ss + past, -1), vs.reshape(bs, ss + past, -1),
                         tq=256)
    y_sample = _ffn_call(xs, [os_.reshape(bs * ss, -1)], mod_s[1], wo1, l1_g_ffn, w1_1, w2_1, g_final, tm=512)

    new_ckv = ckv_new.reshape(bp, sp, KV_RANK)
    new_kpe = kpe_new[:, QK_NOPE:QK_NOPE + QK_ROPE].reshape(bp, sp, QK_ROPE)
    return (y_prompt.reshape(bp, sp, d), y_sample.reshape(bs, ss, d), new_lru, new_ckv, new_kpe)
```

```python
import functools
import math

import jax
import jax.numpy as jnp
from jax import lax
from jax.experimental import pallas as pl
from jax.experimental.pallas import tpu as pltpu

F32 = jnp.float32
BF16 = jnp.bfloat16

D_MODEL = 1024
D_FF = 4 * D_MODEL
EPS = 1e-6
D_RNN = 512
RNN_BLOCKS = 8
RNN_BLOCK = 64
CONV_W = 4
LRU_C = 8.0
D_POOL = 512
POOL_WINDOWS = (2, 4, 8, 16)
POOL_GROUP = 128
N_HEADS = 16
QK_NOPE = 64
QK_ROPE = 32
V_HEAD = 64
Q_RANK = 384
KV_RANK = 256
ROPE_BASE = 10000.0
GRID_W = 64

LANES = 128
SUBLANES = 8
HEAD_PAD = 128
HALO = 8
VMEM_LIMIT = 52 * 1024 * 1024


def _cparams(sem):
    return pltpu.CompilerParams(dimension_semantics=sem, vmem_limit_bytes=VMEM_LIMIT)


def _dot(a, b):
    return jnp.dot(a, b, preferred_element_type=F32)


def _rms_mod(x, g, shift, scale):
    ms = jnp.mean(x * x, axis=-1, keepdims=True)
    y = x * lax.rsqrt(ms + EPS) * g
    return y * (1.0 + scale) + shift


def _mod_kernel(c_ref, w_ref, b_ref, o_ref):
    c = c_ref[...]
    s = c * jax.nn.sigmoid(c)
    o_ref[...] = _dot(s.astype(BF16), w_ref[...].astype(BF16)) + b_ref[...]


def _mod_call(cond8, w_mod, b_mod):
    n = w_mod.shape[1]
    tn = 1024
    return pl.pallas_call(
        _mod_kernel,
        grid=(n // tn,),
        in_specs=[
            pl.BlockSpec((SUBLANES, D_MODEL), lambda j: (0, 0)),
            pl.BlockSpec((D_MODEL, tn), lambda j: (0, j)),
            pl.BlockSpec((1, tn), lambda j: (0, j)),
        ],
        out_specs=pl.BlockSpec((SUBLANES, tn), lambda j: (0, j)),
        out_shape=jax.ShapeDtypeStruct((SUBLANES, n), F32),
        compiler_params=_cparams(("arbitrary",)),
        name="adaln_mod",
    )(cond8, w_mod, b_mod.reshape(1, n))


def _l0_in_kernel(x_ref, mod_ref, g_ref, w_ref, z_ref):
    shift = mod_ref[0, :, 0:D_MODEL]
    scale = mod_ref[0, :, D_MODEL:2 * D_MODEL]
    h = _rms_mod(x_ref[...], g_ref[...], shift, scale)
    z_ref[...] = _dot(h.astype(BF16), w_ref[...])


def _l0_in_call(x2d, mod3, g, w_bf, tm):
    t = x2d.shape[0]
    nb = mod3.shape[0]
    rows_per_b = t // nb
    n = w_bf.shape[1]
    return pl.pallas_call(
        _l0_in_kernel,
        grid=(t // tm,),
        in_specs=[
            pl.BlockSpec((tm, D_MODEL), lambda i: (i, 0)),
            pl.BlockSpec((1, 1, 6 * D_MODEL), lambda i: ((i * tm) // rows_per_b, 0, 0)),
            pl.BlockSpec((1, D_MODEL), lambda i: (0, 0)),
            pl.BlockSpec((D_MODEL, n), lambda i: (0, 0)),
        ],
        out_specs=pl.BlockSpec((tm, n), lambda i: (i, 0)),
        out_shape=jax.ShapeDtypeStruct((t, n), F32),
        compiler_params=_cparams(("arbitrary",)),
        name="l0_in_proj",
    )(x2d, mod3, g.reshape(1, D_MODEL), w_bf)


def _scan_tile(a, u, carry_row, reverse, a_s, u_s, ts):
    g = ts // SUBLANES
    a3 = a.reshape(g, SUBLANES, D_RNN)
    u3 = u.reshape(g, SUBLANES, D_RNN)
    row = lax.broadcasted_iota(jnp.int32, (g, SUBLANES, D_RNN), 1)
    for d in (1, 2, 4):
        if reverse:
            sh = SUBLANES - d
            m = row < SUBLANES - d
        else:
            sh = d
            m = row >= d
        a_sh = pltpu.roll(a3, sh, axis=1)
        u_sh = pltpu.roll(u3, sh, axis=1)
        u3 = u3 + a3 * jnp.where(m, u_sh, 0.0)
        a3 = a3 * jnp.where(m, a_sh, 1.0)
    a_s[...] = a3.reshape(ts, D_RNN)
    u_s[...] = u3.reshape(ts, D_RNN)
    edge = 0 if reverse else SUBLANES - 1

    def body(k, c):
        gi = (g - 1 - k) if reverse else k
        r0 = pl.multiple_of(gi * SUBLANES, SUBLANES)
        h = u_s[pl.ds(r0, SUBLANES), :] + a_s[pl.ds(r0, SUBLANES), :] * c
        u_s[pl.ds(r0, SUBLANES), :] = h
        return jnp.broadcast_to(h[edge:edge + 1, :], (SUBLANES, D_RNN))

    c0 = jnp.broadcast_to(carry_row, (SUBLANES, D_RNN))
    c_last = lax.fori_loop(0, g, body, c0, unroll=4)
    return c_last[0:1, :]


def _lru_kernel(*refs, reverse, with_pool, ts, nt, seq_len):
    if with_pool:
        (xr_ref, xrp_ref, xrn_ref, xp_ref, xpp_ref, xpn_ref, h0_ref, cw_ref, cb_ref, wa_ref, ba_ref, wi_ref,
         bi_ref, lam_ref, pw_ref, ps_ref, h_ref, st_ref, yp_ref, xc_ref,
         ext_s, a_s, u_s, carry_s, b2_s, b4_s, b8_s) = refs
    else:
        (xc_in_ref, gate_ref, hf_ref, h0_ref, wa_ref, ba_ref, wi_ref,
         bi_ref, lam_ref, y_ref, st_ref, a_s, u_s, carry_s) = refs
    j = pl.program_id(1)
    t = (nt - 1 - j) if reverse else j

    @pl.when(j == 0)
    def _():
        carry_s[...] = h0_ref[0]

    def fill_ext(cur_ref, prev_ref, next_ref):
        ext_s[0:HALO, :] = jnp.where(t > 0, prev_ref[0], 0.0)
        ext_s[HALO:HALO + ts, :] = cur_ref[0]
        ext_s[HALO + ts:HALO + ts + HALO, :] = jnp.where(t < nt - 1, next_ref[0], 0.0)

    if with_pool:
        fill_ext(xr_ref, xrp_ref, xrn_ref)
        xc = cb_ref[...]
        for k in range(CONV_W):
            off = HALO - CONV_W // 2 + k
            xc = xc + ext_s[off:off + ts, :] * cw_ref[k:k + 1, :]
        xc_ref[0] = xc
    else:
        xc = xc_in_ref[0]

    xcb = xc.astype(BF16)
    half = D_RNN // 2
    ga = jnp.concatenate([_dot(xcb[:, :half], wa_ref[0]), _dot(xcb[:, half:], wa_ref[1])], axis=-1)
    gi = jnp.concatenate([_dot(xcb[:, :half], wi_ref[0]), _dot(xcb[:, half:], wi_ref[1])], axis=-1)
    r = jax.nn.sigmoid(ga + ba_ref[...])
    i = jax.nn.sigmoid(gi + bi_ref[...])
    log_a = (LRU_C * r) * jax.nn.log_sigmoid(lam_ref[...])
    a = jnp.exp(log_a)
    v = -jnp.tanh(log_a) * (a * a + 1.0)
    u = jnp.where(v > 0.0, v * lax.rsqrt(v), 0.0) * (i * xc)

    c_last = _scan_tile(a, u, carry_s[...], reverse, a_s, u_s, ts)
    carry_s[...] = c_last
    st_ref[0] = c_last
    h = u_s[...]

    if not with_pool:
        y_ref[0] = ((hf_ref[0] + h) * jax.nn.gelu(gate_ref[0])).astype(BF16)
        return

    h_ref[0] = h

    fill_ext(xp_ref, xpp_ref, xpn_ref)
    n2 = ts + 2 * HALO
    b2_s[1:n2, :] = ext_s[1:n2, :] + ext_s[0:n2 - 1, :]
    b4_s[2:n2 - 1, :] = b2_s[3:n2, LANES:] + b2_s[1:n2 - 2, LANES:]
    b8_s[4:n2 - 3, :] = b4_s[6:n2 - 1, LANES:] + b4_s[2:n2 - 5, LANES:]
    s16 = b8_s[HALO + 4:HALO + 4 + ts, LANES:] + b8_s[HALO - 4:HALO - 4 + ts, LANES:]
    sums = (b2_s[HALO:HALO + ts, 0:LANES], b4_s[HALO:HALO + ts, 0:LANES], b8_s[HALO:HALO + ts, 0:LANES], s16)
    pos = t * ts + lax.broadcasted_iota(jnp.int32, (ts, LANES), 0)
    for gidx, w in enumerate(POOL_WINDOWS):
        left = w // 2
        right = w - 1 - left
        lo = jnp.maximum(pos - left, 0)
        hi = jnp.minimum(pos + right, seq_len - 1) + 1
        cnt = (hi - lo).astype(F32)
        xg = xp_ref[0, :, gidx * LANES:(gidx + 1) * LANES]
        dlt = sums[gidx] / cnt - xg
        yg = _dot(dlt.astype(BF16), pw_ref[gidx]) * ps_ref[:, gidx * LANES:(gidx + 1) * LANES]
        yp_ref[0, :, gidx * LANES:(gidx + 1) * LANES] = yg.astype(BF16)


def _lru_call(z3, h0, fwd, prm, *, reverse, ts):
    nseq, seq_len, _ = z3.shape
    nt = seq_len // ts
    tb = ts // HALO
    nhb = seq_len // HALO
    d = 1 if reverse else 0
    with_pool = not reverse

    def tpos(j):
        return (nt - 1 - j) if reverse else j

    def cur(col):
        return pl.BlockSpec((1, ts, D_RNN), lambda s, j: (s, tpos(j), col))

    def prev(col):
        return pl.BlockSpec((1, HALO, D_RNN), lambda s, j: (s, jnp.maximum(tpos(j) * tb - 1, 0), col))

    def nxt(col):
        return pl.BlockSpec((1, HALO, D_RNN), lambda s, j: (s, jnp.minimum((tpos(j) + 1) * tb, nhb - 1), col))

    def const(shape):
        nd = len(shape)
        return pl.BlockSpec(shape, lambda s, j: (0,) * nd)

    tile_out = pl.BlockSpec((1, ts, D_RNN), lambda s, j: (s, tpos(j), 0))
    st_spec = pl.BlockSpec((1, 1, D_RNN), lambda s, j: (s, 0, 0))
    conv_specs = [const((CONV_W, D_RNN)), const((1, D_RNN))]
    conv_args = [prm["conv_w"], prm["conv_b"]]
    w_specs = [const((2, 256, 256)), const((1, D_RNN)), const((2, 256, 256)), const((1, D_RNN)), const((1, D_RNN))]
    w_args = [prm["wa"][d], prm["ba"][d], prm["wi"][d], prm["bi"][d], prm["lam"][d]]
    scratch = [pltpu.VMEM((ts, D_RNN), F32), pltpu.VMEM((ts, D_RNN), F32), pltpu.VMEM((1, D_RNN), F32)]
    kern = functools.partial(_lru_kernel, reverse=reverse, with_pool=with_pool, ts=ts, nt=nt, seq_len=seq_len)
    if with_pool:
        in_specs = [cur(0), prev(0), nxt(0), cur(2), prev(2), nxt(2), st_spec] + conv_specs + w_specs + [
            const((4, POOL_GROUP, POOL_GROUP)), const((1, D_POOL))]
        args = [z3, z3, z3, z3, z3, z3, h0] + conv_args + w_args + [prm["pool_w"], prm["pool_scale"]]
        out_specs = [tile_out, st_spec, tile_out, tile_out]
        out_shape = [jax.ShapeDtypeStruct((nseq, seq_len, D_RNN), F32),
                     jax.ShapeDtypeStruct((nseq, 1, D_RNN), F32),
                     jax.ShapeDtypeStruct((nseq, seq_len, D_POOL), BF16),
                     jax.ShapeDtypeStruct((nseq, seq_len, D_RNN), F32)]
        n2 = ts + 2 * HALO
        scratch = [pltpu.VMEM((n2, D_RNN), F32)] + scratch + [
            pltpu.VMEM((n2, D_POOL), F32), pltpu.VMEM((n2, D_POOL - LANES), F32),
            pltpu.VMEM((n2, D_POOL - 2 * LANES), F32)]
        name = "l0_lru_fwd_pool"
    else:
        hf, xc = fwd
        in_specs = [tile_out, cur(1), tile_out, st_spec] + w_specs
        args = [xc, z3, hf, h0] + w_args
        out_specs = [tile_out, st_spec]
        out_shape = [jax.ShapeDtypeStruct((nseq, seq_len, D_RNN), BF16),
                     jax.ShapeDtypeStruct((nseq, 1, D_RNN), F32)]
        name = "l0_lru_bwd"
    return pl.pallas_call(
        kern,
        grid=(nseq, nt),
        in_specs=in_specs,
        out_specs=out_specs,
        out_shape=out_shape,
        scratch_shapes=scratch,
        compiler_params=_cparams(("arbitrary", "arbitrary")),
        name=name,
    )(*args)


FF_CHUNK = 1024


def _ffn_kernel(*refs, n_y, final_norm):
    x_ref = refs[0]
    y_refs = refs[1:1 + n_y]
    mod_ref = refs[1 + n_y]
    wo_refs = refs[2 + n_y:2 + 2 * n_y]
    g_ref, w1_ref, w2_ref = refs[2 + 2 * n_y:5 + 2 * n_y]
    rest = refs[5 + 2 * n_y:]
    if final_norm:
        gf_ref, o_ref = rest
    else:
        (o_ref,) = rest

    def mod(k):
        return mod_ref[0, :, k * D_MODEL:(k + 1) * D_MODEL]

    mix = _dot(y_refs[0][...], wo_refs[0][...])
    for k in range(1, n_y):
        mix = mix + _dot(y_refs[k][...], wo_refs[k][...])
    x1 = x_ref[...] + mod(2) * mix
    hn = _rms_mod(x1, g_ref[...], mod(3), mod(4)).astype(BF16)
    acc = None
    for c in range(D_FF // FF_CHUNK):
        hc = _dot(hn, w1_ref[:, c * FF_CHUNK:(c + 1) * FF_CHUNK])
        hc = jnp.square(jnp.maximum(hc, 0.0)).astype(BF16)
        part = _dot(hc, w2_ref[c * FF_CHUNK:(c + 1) * FF_CHUNK, :])
        acc = part if acc is None else acc + part
    x2 = x1 + mod(5) * acc
    if final_norm:
        ms = jnp.mean(x2 * x2, axis=-1, keepdims=True)
        x2 = x2 * lax.rsqrt(ms + EPS) * gf_ref[...]
    o_ref[...] = x2


def _ffn_call(x2d, ys, mod3, wos, g, w1, w2, g_final, tm):
    t = x2d.shape[0]
    nb = mod3.shape[0]
    rows_per_b = t // nb
    n_y = len(ys)
    final_norm = g_final is not None

    def const(shape):
        return pl.BlockSpec(shape, lambda i: (0, 0), pipeline_mode=pl.Buffered(1))

    in_specs = [pl.BlockSpec((tm, D_MODEL), lambda i: (i, 0))]
    in_specs += [pl.BlockSpec((tm, y.shape[1]), lambda i: (i, 0)) for y in ys]
    in_specs += [pl.BlockSpec((1, 1, 6 * D_MODEL), lambda i: ((i * tm) // rows_per_b, 0, 0))]
    in_specs += [const(w.shape) for w in wos]
    in_specs += [const((1, D_MODEL)), const((D_MODEL, D_FF)), const((D_FF, D_MODEL))]
    args = [x2d, *ys, mod3, *wos, g.reshape(1, D_MODEL), w1, w2]
    if final_norm:
        in_specs.append(const((1, D_MODEL)))
        args.append(g_final.reshape(1, D_MODEL))
    return pl.pallas_call(
        functools.partial(_ffn_kernel, n_y=n_y, final_norm=final_norm),
        grid=(t // tm,),
        in_specs=in_specs,
        out_specs=pl.BlockSpec((tm, D_MODEL), lambda i: (i, 0)),
        out_shape=jax.ShapeDtypeStruct((t, D_MODEL), F32),
        compiler_params=_cparams(("arbitrary",)),
        name="mix_out_ffn",
    )(*args)


def _rope(x, cos, sa, sb):
    return x * cos + pltpu.roll(x, 8, axis=1) * sa + pltpu.roll(x, LANES - 8, axis=1) * sb


def _mla_proj_kernel(*refs, rope, emit_cache):
    x_ref, mod_ref, g_ref, win_ref, gq_ref, wqb_ref, gkv_ref, wk_ref, wv_ref, vones_ref = refs[:10]
    rest = refs[10:]
    if rope:
        cos_ref, sa_ref, sb_ref = rest[:3]
        rest = rest[3:]
    q_ref, k_ref, v_ref = rest[:3]
    if emit_cache:
        ckv_ref, kpe_ref = rest[3:]

    shift = mod_ref[0, :, 0:D_MODEL]
    scale = mod_ref[0, :, D_MODEL:2 * D_MODEL]
    h = _rms_mod(x_ref[...], g_ref[...], shift, scale).astype(BF16)
    z = _dot(h, win_ref[...])
    cq = z[:, :Q_RANK]
    ckv = z[:, Q_RANK:Q_RANK + KV_RANK]
    kpe = z[:, Q_RANK + KV_RANK:]
    cqn = cq * lax.rsqrt(jnp.mean(cq * cq, axis=-1, keepdims=True) + EPS) * gq_ref[...]
    ckvn = ckv * lax.rsqrt(jnp.mean(ckv * ckv, axis=-1, keepdims=True) + EPS) * gkv_ref[...]
    if emit_cache:
        ckv_ref[...] = ckvn
        kpe_ref[...] = kpe
    q = _dot(cqn.astype(BF16), wqb_ref[...]) * Q_PRESCALE
    ckvb = ckvn.astype(BF16)
    kn = _dot(ckvb, wk_ref[...])
    v_ref[...] = (_dot(ckvb, wv_ref[...]) + vones_ref[...]).astype(BF16)
    if rope:
        cos, sa, sb = cos_ref[...], sa_ref[...], sb_ref[...]
        kpe = _rope(kpe, cos, sa, sb)
    for hd in range(N_HEADS):
        sl = slice(hd * HEAD_PAD, (hd + 1) * HEAD_PAD)
        qh = q[:, sl]
        if rope:
            qh = _rope(qh, cos, sa, sb)
        q_ref[:, sl] = qh.astype(BF16)
        k_ref[:, sl] = (kn[:, sl] + kpe).astype(BF16)


def _mla_proj_call(x2d, mod3, prm, tables, *, seq_len, emit_cache, tm):
    t = x2d.shape[0]
    nb = mod3.shape[0]
    rows_per_b = t // nb
    rope = tables is not None
    tiles_per_seq = seq_len // tm
    hw = N_HEADS * HEAD_PAD

    def const(shape):
        return pl.BlockSpec(shape, lambda i: (0, 0))

    def rows(n):
        return pl.BlockSpec((tm, n), lambda i: (i, 0))

    in_specs = [rows(D_MODEL),
                pl.BlockSpec((1, 1, 6 * D_MODEL), lambda i: ((i * tm) // rows_per_b, 0, 0)),
                const((1, D_MODEL)), const(prm["w_in"].shape), const((1, Q_RANK)), const(prm["w_qb"].shape),
                const((1, KV_RANK)), const(prm["w_k"].shape), const(prm["w_v"].shape), const((1, hw))]
    args = [x2d, mod3, prm["g_mix"], prm["w_in"], prm["g_q"], prm["w_qb"], prm["g_kv"], prm["w_k"], prm["w_v"],
            prm["v_ones"]]
    if rope:
        in_specs += [pl.BlockSpec((tm, LANES), lambda i: (i % tiles_per_seq, 0))] * 3
        args += list(tables)
    out_specs = [rows(hw), rows(hw), rows(hw)]
    out_shape = [jax.ShapeDtypeStruct((t, hw), BF16)] * 3
    if emit_cache:
        out_specs += [rows(KV_RANK), rows(LANES)]
        out_shape += [jax.ShapeDtypeStruct((t, KV_RANK), F32), jax.ShapeDtypeStruct((t, LANES), F32)]
    return pl.pallas_call(
        functools.partial(_mla_proj_kernel, rope=rope, emit_cache=emit_cache),
        grid=(t // tm,),
        in_specs=in_specs,
        out_specs=out_specs,
        out_shape=out_shape,
        compiler_params=_cparams(("arbitrary",)),
        name="l1_mla_proj",
    )(*args)


def _ctx_expand_kernel(ckv_ref, kpe_ref, wk_ref, wv_ref, vones_ref, k_ref, v_ref):
    ckvb = ckv_ref[...].astype(BF16)
    kn = _dot(ckvb, wk_ref[...])
    v_ref[...] = (_dot(ckvb, wv_ref[...]) + vones_ref[...]).astype(BF16)
    kpe = kpe_ref[...]
    for hd in range(N_HEADS):
        sl = slice(hd * HEAD_PAD, (hd + 1) * HEAD_PAD)
        k_ref[:, sl] = (kn[:, sl] + kpe).astype(BF16)


def _ctx_expand_call(ckv2d, kpe2d, prm, tm):
    hw = N_HEADS * HEAD_PAD
    t = ckv2d.shape[0]
    const = lambda a: pl.BlockSpec(a.shape, lambda i: (0, 0))
    rows = lambda n: pl.BlockSpec((tm, n), lambda i: (i, 0))
    return pl.pallas_call(
        _ctx_expand_kernel,
        grid=(t // tm,),
        in_specs=[rows(KV_RANK), rows(LANES), const(prm["w_k"]), const(prm["w_v"]), const(prm["v_ones"])],
        out_specs=[rows(hw), rows(hw)],
        out_shape=[jax.ShapeDtypeStruct((t, hw), BF16), jax.ShapeDtypeStruct((t, hw), BF16)],
        compiler_params=_cparams(("arbitrary",)),
        name="l1_ctx_expand",
    )(ckv2d, kpe2d, prm["w_k"], prm["w_v"], prm["v_ones"])


Q_PRESCALE = (QK_NOPE + QK_ROPE) ** -0.5 * math.log2(math.e)
_NT = (((1,), (1,)), ((), ()))
KEY_CHUNK = 256


def _pair_out(acc_even, acc_odd):
    lane = lax.broadcasted_iota(jnp.int32, acc_even.shape, 1)
    first = lane < V_HEAD
    num = jnp.where(first, acc_even, acc_odd)
    den = jnp.where(first, pltpu.roll(acc_even, V_HEAD, axis=1), pltpu.roll(acc_odd, V_HEAD, axis=1))
    return (num / den).astype(BF16)


def _attn_ctx_kernel(q_ref, k_ref, v_ref, o_ref, s_scr, p_scr, *, pairs):
    heads = [slice(h * HEAD_PAD, (h + 1) * HEAD_PAD) for h in range(2 * pairs)]
    for h, sl in enumerate(heads):
        s_scr[h] = lax.dot_general(q_ref[0, :, sl], k_ref[0, :, sl], _NT, preferred_element_type=F32)
    for h in range(2 * pairs):
        s = s_scr[h]
        p_scr[h] = jnp.exp2(s - s.max(axis=-1, keepdims=True)).astype(BF16)
    for p in range(pairs):
        accs = [_dot(p_scr[2 * p + e], v_ref[0, :, heads[2 * p + e]]) for e in range(2)]
        o_ref[0, :, p * LANES:(p + 1) * LANES] = _pair_out(accs[0], accs[1])


def _attn_ctx_call(q3, k3, v3):
    b, s, _ = q3.shape
    spec = lambda w: pl.BlockSpec((1, s, w), lambda bi: (bi, 0, 0))
    hw = N_HEADS * HEAD_PAD
    return pl.pallas_call(
        functools.partial(_attn_ctx_kernel, pairs=N_HEADS // 2),
        grid=(b,),
        in_specs=[spec(hw), spec(hw), spec(hw)],
        out_specs=spec(N_HEADS * V_HEAD),
        out_shape=jax.ShapeDtypeStruct((b, s, N_HEADS * V_HEAD), BF16),
        scratch_shapes=[pltpu.VMEM((N_HEADS, s, s), F32), pltpu.VMEM((N_HEADS, s, s), BF16)],
        compiler_params=_cparams(("arbitrary",)),
        name="l1_attention_ctx",
    )(q3, k3, v3)


def _attn_lat_kernel(q_ref, k1_ref, k2_ref, v1_ref, v2_ref, o_ref, s_even, s_odd, m_even, m_odd, p_scr):
    n = pl.program_id(0)
    n1, n2 = k1_ref.shape[1], k2_ref.shape[1]

    @pl.when(n == 0)
    def _():
        s_odd[...] = jnp.zeros(s_odd.shape, F32)
        m_odd[...] = jnp.zeros(m_odd.shape, F32)

    def step(s_w, m_w, s_r, m_r):
        for e in range(2):
            sl = slice(e * HEAD_PAD, (e + 1) * HEAD_PAD)
            q = q_ref[0, :, sl]
            sa = lax.dot_general(q, k1_ref[0, :, sl], _NT, preferred_element_type=F32)
            sb = lax.dot_general(q, k2_ref[0, :, sl], _NT, preferred_element_type=F32)
            s_w[e, :, 0:n1] = sa
            s_w[e, :, n1:] = sb
            m = jnp.maximum(sa.max(axis=-1, keepdims=True), sb.max(axis=-1, keepdims=True))
            m_w[e] = jnp.broadcast_to(m, m_w.shape[1:])
        accs = []
        for e in range(2):
            sl = slice(e * HEAD_PAD, (e + 1) * HEAD_PAD)
            m = m_r[e]
            for t in range((n1 + n2) // LANES):
                tl = slice(t * LANES, (t + 1) * LANES)
                p_scr[e, :, tl] = jnp.exp2(s_r[e, :, tl] - m).astype(BF16)
            accs.append(_dot(p_scr[e, :, 0:n1], v1_ref[0, :, sl]) + _dot(p_scr[e, :, n1:], v2_ref[0, :, sl]))
        o_ref[0] = _pair_out(accs[0], accs[1])

    @pl.when(n % 2 == 0)
    def _():
        step(s_even, m_even, s_odd, m_odd)

    @pl.when(n % 2 == 1)
    def _():
        step(s_odd, m_odd, s_even, m_even)


def _attn_lat_call(q3, k1, k2, v1, v2, *, tq):
    b, s, _ = q3.shape
    s1, s2 = k1.shape[1], k2.shape[1]
    sk = s1 + s2
    npair = N_HEADS // 2
    nq = s // tq
    n_items = b * npair * nq

    def item(n):
        return n // (npair * nq), n % nq, (n // nq) % npair

    def cur(n):
        return item(jnp.minimum(n, n_items - 1))

    def lag(n):
        return item(jnp.maximum(n - 1, 0))

    qw = 2 * HEAD_PAD
    in_specs = [
        pl.BlockSpec((1, tq, qw), lambda n: cur(n)),
        pl.BlockSpec((1, s1, qw), lambda n: (cur(n)[0], 0, cur(n)[2])),
        pl.BlockSpec((1, s2, qw), lambda n: (cur(n)[0], 0, cur(n)[2])),
        pl.BlockSpec((1, s1, qw), lambda n: (lag(n)[0], 0, lag(n)[2])),
        pl.BlockSpec((1, s2, qw), lambda n: (lag(n)[0], 0, lag(n)[2])),
    ]
    return pl.pallas_call(
        _attn_lat_kernel,
        grid=(n_items + 1,),
        in_specs=in_specs,
        out_specs=pl.BlockSpec((1, tq, LANES), lambda n: lag(n)),
        out_shape=jax.ShapeDtypeStruct((b, s, N_HEADS * V_HEAD), BF16),
        scratch_shapes=[pltpu.VMEM((2, tq, sk), F32)] * 2 + [pltpu.VMEM((2, tq, LANES), F32)] * 2
        + [pltpu.VMEM((2, tq, sk), BF16)],
        compiler_params=_cparams(("arbitrary",)),
        name="l1_attention_lat",
    )(q3, k1, k2, v1, v2)


def _block_diag_halves(w):
    halves = []
    for hf in range(2):
        m = jnp.zeros((256, 256), F32)
        for b in range(4):
            m = m.at[b * RNN_BLOCK:(b + 1) * RNN_BLOCK, b * RNN_BLOCK:(b + 1) * RNN_BLOCK].set(w[hf * 4 + b])
        halves.append(m)
    return jnp.stack(halves).astype(BF16)


def _rope_tables(seq_len):
    rows = seq_len // GRID_W
    row = jnp.broadcast_to(jnp.arange(rows)[:, None], (rows, GRID_W)).reshape(-1).astype(F32)
    col = jnp.broadcast_to(jnp.arange(GRID_W)[None, :], (rows, GRID_W)).reshape(-1).astype(F32)
    half = QK_ROPE // 2
    inv = ROPE_BASE ** (-jnp.arange(0, half, 2, dtype=F32) / half)
    ang_r, ang_c = row[:, None] * inv, col[:, None] * inv
    zeros = jnp.zeros((seq_len, 8), F32)
    cr, sr, cc, sc = jnp.cos(ang_r), jnp.sin(ang_r), jnp.cos(ang_c), jnp.sin(ang_c)
    lead = jnp.ones((seq_len, QK_NOPE), F32)
    tail = jnp.ones((seq_len, HEAD_PAD - QK_NOPE - QK_ROPE), F32)
    cos = jnp.concatenate([lead, cr, cr, cc, cc, tail], axis=1)
    sa = jnp.concatenate([0 * lead, zeros, sr, zeros, sc, 0 * tail], axis=1)
    sb = jnp.concatenate([0 * lead, -sr, zeros, -sc, zeros, 0 * tail], axis=1)
    return cos, sa, sb


def kernel(x_prompt, x_sample, state_l0_lru, cache_l1_ckv, cache_l1_kpe, c, c_ctx, l0_w_mod, l0_b_mod, l0_g_mix, l0_g_ffn, l0_w_in, l0_conv_w, l0_conv_b, l0_lru_w_a, l0_lru_b_a, l0_lru_w_i, l0_lru_b_i, l0_lru_lam, l0_pool_w, l0_pool_scale, l0_w_out, l0_ffn_w1, l0_ffn_w2, l1_w_mod, l1_b_mod, l1_g_mix, l1_g_ffn, l1_w_in, l1_g_q, l1_w_qb, l1_g_kv, l1_w_kvb, l1_w_out, l1_ffn_w1, l1_ffn_w2, g_final):
    bp, sp, d = x_prompt.shape
    bs, ss, _ = x_sample.shape
    past = cache_l1_ckv.shape[1]
    xp = x_prompt.reshape(bp * sp, d)
    xs = x_sample.reshape(bs * ss, d)

    cond8 = jnp.concatenate([c, c_ctx[None, :], jnp.zeros((SUBLANES - bs - 1, d), F32)], axis=0)
    m0 = _mod_call(cond8, l0_w_mod, l0_b_mod)
    m1 = _mod_call(cond8, l1_w_mod, l1_b_mod)
    mod_p = [m[bs:bs + 1].reshape(1, 1, 6 * d) for m in (m0, m1)]
    mod_s = [m[0:bs].reshape(bs, 1, 6 * d) for m in (m0, m1)]

    lru = dict(
        conv_w=l0_conv_w, conv_b=l0_conv_b.reshape(1, D_RNN),
        wa=[_block_diag_halves(l0_lru_w_a[i]) for i in range(2)],
        wi=[_block_diag_halves(l0_lru_w_i[i]) for i in range(2)],
        ba=[l0_lru_b_a[i].reshape(1, D_RNN) for i in range(2)],
        bi=[l0_lru_b_i[i].reshape(1, D_RNN) for i in range(2)],
        lam=[l0_lru_lam[i].reshape(1, D_RNN) for i in range(2)],
        pool_w=l0_pool_w.astype(BF16), pool_scale=l0_pool_scale.reshape(1, D_POOL))
    w_in0 = l0_w_in.astype(BF16)
    wo0 = l0_w_out.astype(BF16)
    wos0 = [wo0[:D_RNN], wo0[D_RNN:]]
    w1_0, w2_0 = l0_ffn_w1.astype(BF16), l0_ffn_w2.astype(BF16)

    def layer0(x2d, mod3, nseq, seq_len, h0f, h0b, ts):
        z = _l0_in_call(x2d, mod3, l0_g_mix, w_in0, tm=512)
        z3 = z.reshape(nseq, seq_len, 3 * D_RNN)
        hf, stf, ypool, xc = _lru_call(z3, h0f, None, lru, reverse=False, ts=ts)
        yrnn, stb = _lru_call(z3, h0b, (hf, xc), lru, reverse=True, ts=ts)
        ys = [yrnn.reshape(-1, D_RNN), ypool.reshape(-1, D_POOL)]
        x2 = _ffn_call(x2d, ys, mod3, wos0, l0_g_ffn, w1_0, w2_0, None, tm=512)
        return x2, stf, stb

    zero_st = jnp.zeros((bp, 1, D_RNN), F32)
    xp, stf, stb = layer0(xp, mod_p[0], bp, sp, zero_st, zero_st, ts=sp)
    xs, _, _ = layer0(xs, mod_s[0], bs, ss, state_l0_lru[:, 0:1], state_l0_lru[:, 1:2], ts=512)
    new_lru = jnp.concatenate([stf, stb], axis=1)

    pad_pe = HEAD_PAD - QK_NOPE - QK_ROPE
    w_in1 = jnp.concatenate([l1_w_in[:, :Q_RANK + KV_RANK], jnp.zeros((d, QK_NOPE), F32),
                             l1_w_in[:, Q_RANK + KV_RANK:], jnp.zeros((d, pad_pe), F32)], axis=1).astype(BF16)
    wqb = l1_w_qb.reshape(Q_RANK, N_HEADS, QK_NOPE + QK_ROPE)
    wqb = jnp.pad(wqb, ((0, 0), (0, 0), (0, pad_pe))).reshape(Q_RANK, N_HEADS * HEAD_PAD).astype(BF16)
    wkvb = l1_w_kvb.reshape(KV_RANK, N_HEADS, QK_NOPE + V_HEAD)
    w_k = jnp.pad(wkvb[:, :, :QK_NOPE], ((0, 0), (0, 0), (0, HEAD_PAD - QK_NOPE)))
    w_k = w_k.reshape(KV_RANK, N_HEADS * HEAD_PAD).astype(BF16)
    wv = wkvb[:, :, QK_NOPE:].reshape(KV_RANK, N_HEADS // 2, 2, V_HEAD)
    zv = jnp.zeros((KV_RANK, N_HEADS // 2, V_HEAD), F32)
    w_v = jnp.stack([wv[:, :, 0], zv, zv, wv[:, :, 1]], axis=2).reshape(KV_RANK, N_HEADS * HEAD_PAD).astype(BF16)
    one = jnp.ones((N_HEADS // 2, V_HEAD), F32)
    v_ones = jnp.stack([0 * one, one, one, 0 * one], axis=1).reshape(1, N_HEADS * HEAD_PAD)
    mla = dict(g_mix=l1_g_mix.reshape(1, d), w_in=w_in1, g_q=l1_g_q.reshape(1, Q_RANK), w_qb=wqb,
               g_kv=l1_g_kv.reshape(1, KV_RANK), w_k=w_k, w_v=w_v, v_ones=v_ones)
    wo1 = [l1_w_out.astype(BF16)]
    w1_1, w2_1 = l1_ffn_w1.astype(BF16), l1_ffn_w2.astype(BF16)

    qp, kp, vp, ckv_new, kpe_new = _mla_proj_call(xp, mod_p[1], mla, None, seq_len=bp * sp, emit_cache=True,
                                                  tm=512)
    op = _attn_ctx_call(qp.reshape(bp, sp, -1), kp.reshape(bp, sp, -1), vp.reshape(bp, sp, -1))
    y_prompt = _ffn_call(xp, [op.reshape(bp * sp, -1)], mod_p[1], wo1, l1_g_ffn, w1_1, w2_1, g_final, tm=512)

    tables = _rope_tables(ss)
    qs, ks, vs = _mla_proj_call(xs, mod_s[1], mla, tables, seq_len=ss, emit_cache=False, tm=512)
    kpe_ctx = jnp.pad(cache_l1_kpe.reshape(bs * past, QK_ROPE), ((0, 0), (QK_NOPE, pad_pe)))
    kc, vc = _ctx_expand_call(cache_l1_ckv.reshape(bs * past, KV_RANK), kpe_ctx, mla, tm=past)
    os_ = _attn_lat_call(qs.reshape(bs, ss, -1), ks.reshape(bs, ss, -1), kc.reshape(bs, past, -1),
                         vs.reshape(bs, ss, -1), vc.reshape(bs, past, -1), tq=256)
    y_sample = _ffn_call(xs, [os_.reshape(bs * ss, -1)], mod_s[1], wo1, l1_g_ffn, w1_1, w2_1, g_final, tm=512)

    new_ckv = ckv_new.reshape(bp, sp, KV_RANK)
    new_kpe = kpe_new[:, QK_NOPE:QK_NOPE + QK_ROPE].reshape(bp, sp, QK_ROPE)
    return (y_prompt.reshape(bp, sp, d), y_sample.reshape(bs, ss, d), new_lru, new_ckv, new_kpe)
```

```python
import functools
import math

import jax
import jax.numpy as jnp
import numpy as np
from jax import lax
from jax.experimental import pallas as pl
from jax.experimental.pallas import tpu as pltpu

F32 = jnp.float32
BF16 = jnp.bfloat16

D_MODEL = 1024
D_FF = 4 * D_MODEL
EPS = 1e-6
D_RNN = 512
RNN_BLOCKS = 8
RNN_BLOCK = 64
CONV_W = 4
LRU_C = 8.0
D_POOL = 512
POOL_WINDOWS = (2, 4, 8, 16)
POOL_GROUP = 128
N_HEADS = 16
QK_NOPE = 64
QK_ROPE = 32
V_HEAD = 64
Q_RANK = 384
KV_RANK = 256
ROPE_BASE = 10000.0
GRID_W = 64

LANES = 128
SUBLANES = 8
HEAD_PAD = 128
HALO = 8
VMEM_LIMIT = 52 * 1024 * 1024


def _cparams(sem):
    return pltpu.CompilerParams(dimension_semantics=sem, vmem_limit_bytes=VMEM_LIMIT)


def _dot(a, b):
    return jnp.dot(a, b, preferred_element_type=F32)


def _rms_mod(x, g, shift, scale):
    ms = jnp.mean(x * x, axis=-1, keepdims=True)
    y = x * lax.rsqrt(ms + EPS) * g
    return y * (1.0 + scale) + shift


def _mod_kernel(c_ref, w_ref, b_ref, o_ref):
    c = c_ref[...]
    s = c * jax.nn.sigmoid(c)
    o_ref[...] = _dot(s.astype(BF16), w_ref[...].astype(BF16)) + b_ref[...]


def _mod_call(cond8, w_mod, b_mod):
    n = w_mod.shape[1]
    tn = 1024
    return pl.pallas_call(
        _mod_kernel,
        grid=(n // tn,),
        in_specs=[
            pl.BlockSpec((SUBLANES, D_MODEL), lambda j: (0, 0)),
            pl.BlockSpec((D_MODEL, tn), lambda j: (0, j)),
            pl.BlockSpec((1, tn), lambda j: (0, j)),
        ],
        out_specs=pl.BlockSpec((SUBLANES, tn), lambda j: (0, j)),
        out_shape=jax.ShapeDtypeStruct((SUBLANES, n), F32),
        compiler_params=_cparams(("arbitrary",)),
        name="adaln_mod",
    )(cond8, w_mod, b_mod.reshape(1, n))


def _l0_in_kernel(x_ref, mod_ref, g_ref, w_ref, z_ref):
    shift = mod_ref[0, :, 0:D_MODEL]
    scale = mod_ref[0, :, D_MODEL:2 * D_MODEL]
    h = _rms_mod(x_ref[...], g_ref[...], shift, scale)
    z_ref[...] = _dot(h.astype(BF16), w_ref[...])


def _l0_in_call(x2d, mod3, g, w_bf, tm):
    t = x2d.shape[0]
    nb = mod3.shape[0]
    rows_per_b = t // nb
    n = w_bf.shape[1]
    return pl.pallas_call(
        _l0_in_kernel,
        grid=(t // tm,),
        in_specs=[
            pl.BlockSpec((tm, D_MODEL), lambda i: (i, 0)),
            pl.BlockSpec((1, 1, 6 * D_MODEL), lambda i: ((i * tm) // rows_per_b, 0, 0)),
            pl.BlockSpec((1, D_MODEL), lambda i: (0, 0)),
            pl.BlockSpec((D_MODEL, n), lambda i: (0, 0)),
        ],
        out_specs=pl.BlockSpec((tm, n), lambda i: (i, 0)),
        out_shape=jax.ShapeDtypeStruct((t, n), F32),
        compiler_params=_cparams(("arbitrary",)),
        name="l0_in_proj",
    )(x2d, mod3, g.reshape(1, D_MODEL), w_bf)


def _scan_tile(a, u, carry_row, reverse, a_s, u_s, ts):
    g = ts // SUBLANES
    a3 = a.reshape(g, SUBLANES, D_RNN)
    u3 = u.reshape(g, SUBLANES, D_RNN)
    row = lax.broadcasted_iota(jnp.int32, (g, SUBLANES, D_RNN), 1)
    for d in (1, 2, 4):
        if reverse:
            sh = SUBLANES - d
            m = row < SUBLANES - d
        else:
            sh = d
            m = row >= d
        a_sh = pltpu.roll(a3, sh, axis=1)
        u_sh = pltpu.roll(u3, sh, axis=1)
        u3 = u3 + a3 * jnp.where(m, u_sh, 0.0)
        a3 = a3 * jnp.where(m, a_sh, 1.0)
    a_s[...] = a3.reshape(ts, D_RNN)
    u_s[...] = u3.reshape(ts, D_RNN)
    edge = 0 if reverse else SUBLANES - 1

    def body(k, c):
        gi = (g - 1 - k) if reverse else k
        r0 = pl.multiple_of(gi * SUBLANES, SUBLANES)
        h = u_s[pl.ds(r0, SUBLANES), :] + a_s[pl.ds(r0, SUBLANES), :] * c
        u_s[pl.ds(r0, SUBLANES), :] = h
        return jnp.broadcast_to(h[edge:edge + 1, :], (SUBLANES, D_RNN))

    c0 = jnp.broadcast_to(carry_row, (SUBLANES, D_RNN))
    c_last = lax.fori_loop(0, g, body, c0, unroll=4)
    return c_last[0:1, :]


def _lru_kernel(*refs, reverse, with_pool, ts, nt, seq_len):
    if with_pool:
        (xr_ref, xrp_ref, xrn_ref, xp_ref, xpp_ref, xpn_ref, h0_ref, cw_ref, cb_ref, wa_ref, ba_ref, wi_ref,
         bi_ref, lam_ref, pw_ref, ps_ref, h_ref, st_ref, yp_ref, xc_ref,
         ext_s, a_s, u_s, carry_s, b2_s, b4_s, b8_s) = refs
    else:
        (xc_in_ref, gate_ref, hf_ref, h0_ref, wa_ref, ba_ref, wi_ref,
         bi_ref, lam_ref, y_ref, st_ref, a_s, u_s, carry_s) = refs
    j = pl.program_id(1)
    t = (nt - 1 - j) if reverse else j

    @pl.when(j == 0)
    def _():
        carry_s[...] = h0_ref[0]

    def fill_ext(cur_ref, prev_ref, next_ref):
        ext_s[0:HALO, :] = jnp.where(t > 0, prev_ref[0], 0.0)
        ext_s[HALO:HALO + ts, :] = cur_ref[0]
        ext_s[HALO + ts:HALO + ts + HALO, :] = jnp.where(t < nt - 1, next_ref[0], 0.0)

    if with_pool:
        fill_ext(xr_ref, xrp_ref, xrn_ref)
        xc = cb_ref[...]
        for k in range(CONV_W):
            off = HALO - CONV_W // 2 + k
            xc = xc + ext_s[off:off + ts, :] * cw_ref[k:k + 1, :]
        xc_ref[0] = xc
    else:
        xc = xc_in_ref[0]

    xcb = xc.astype(BF16)
    half = D_RNN // 2
    ga = jnp.concatenate([_dot(xcb[:, :half], wa_ref[0]), _dot(xcb[:, half:], wa_ref[1])], axis=-1)
    gi = jnp.concatenate([_dot(xcb[:, :half], wi_ref[0]), _dot(xcb[:, half:], wi_ref[1])], axis=-1)
    r = jax.nn.sigmoid(ga + ba_ref[...])
    i = jax.nn.sigmoid(gi + bi_ref[...])
    log_a = (LRU_C * r) * jax.nn.log_sigmoid(lam_ref[...])
    a = jnp.exp(log_a)
    v = -jnp.tanh(log_a) * (a * a + 1.0)
    u = jnp.where(v > 0.0, v * lax.rsqrt(v), 0.0) * (i * xc)

    c_last = _scan_tile(a, u, carry_s[...], reverse, a_s, u_s, ts)
    carry_s[...] = c_last
    st_ref[0] = c_last
    h = u_s[...]

    if not with_pool:
        y_ref[0] = ((hf_ref[0] + h) * jax.nn.gelu(gate_ref[0])).astype(BF16)
        return

    h_ref[0] = h

    fill_ext(xp_ref, xpp_ref, xpn_ref)
    n2 = ts + 2 * HALO
    b2_s[1:n2, :] = ext_s[1:n2, :] + ext_s[0:n2 - 1, :]
    b4_s[2:n2 - 1, :] = b2_s[3:n2, LANES:] + b2_s[1:n2 - 2, LANES:]
    b8_s[4:n2 - 3, :] = b4_s[6:n2 - 1, LANES:] + b4_s[2:n2 - 5, LANES:]
    s16 = b8_s[HALO + 4:HALO + 4 + ts, LANES:] + b8_s[HALO - 4:HALO - 4 + ts, LANES:]
    sums = (b2_s[HALO:HALO + ts, 0:LANES], b4_s[HALO:HALO + ts, 0:LANES], b8_s[HALO:HALO + ts, 0:LANES], s16)
    pos = t * ts + lax.broadcasted_iota(jnp.int32, (ts, LANES), 0)
    for gidx, w in enumerate(POOL_WINDOWS):
        left = w // 2
        right = w - 1 - left
        lo = jnp.maximum(pos - left, 0)
        hi = jnp.minimum(pos + right, seq_len - 1) + 1
        cnt = (hi - lo).astype(F32)
        xg = xp_ref[0, :, gidx * LANES:(gidx + 1) * LANES]
        dlt = sums[gidx] / cnt - xg
        yg = _dot(dlt.astype(BF16), pw_ref[gidx]) * ps_ref[:, gidx * LANES:(gidx + 1) * LANES]
        yp_ref[0, :, gidx * LANES:(gidx + 1) * LANES] = yg.astype(BF16)


def _lru_call(z3, h0, fwd, prm, *, reverse, ts):
    nseq, seq_len, _ = z3.shape
    nt = seq_len // ts
    tb = ts // HALO
    nhb = seq_len // HALO
    d = 1 if reverse else 0
    with_pool = not reverse

    def tpos(j):
        return (nt - 1 - j) if reverse else j

    def cur(col):
        return pl.BlockSpec((1, ts, D_RNN), lambda s, j: (s, tpos(j), col))

    def prev(col):
        return pl.BlockSpec((1, HALO, D_RNN), lambda s, j: (s, jnp.maximum(tpos(j) * tb - 1, 0), col))

    def nxt(col):
        return pl.BlockSpec((1, HALO, D_RNN), lambda s, j: (s, jnp.minimum((tpos(j) + 1) * tb, nhb - 1), col))

    def const(shape):
        nd = len(shape)
        return pl.BlockSpec(shape, lambda s, j: (0,) * nd)

    tile_out = pl.BlockSpec((1, ts, D_RNN), lambda s, j: (s, tpos(j), 0))
    st_spec = pl.BlockSpec((1, 1, D_RNN), lambda s, j: (s, 0, 0))
    conv_specs = [const((CONV_W, D_RNN)), const((1, D_RNN))]
    conv_args = [prm["conv_w"], prm["conv_b"]]
    w_specs = [const((2, 256, 256)), const((1, D_RNN)), const((2, 256, 256)), const((1, D_RNN)), const((1, D_RNN))]
    w_args = [prm["wa"][d], prm["ba"][d], prm["wi"][d], prm["bi"][d], prm["lam"][d]]
    scratch = [pltpu.VMEM((ts, D_RNN), F32), pltpu.VMEM((ts, D_RNN), F32), pltpu.VMEM((1, D_RNN), F32)]
    kern = functools.partial(_lru_kernel, reverse=reverse, with_pool=with_pool, ts=ts, nt=nt, seq_len=seq_len)
    if with_pool:
        in_specs = [cur(0), prev(0), nxt(0), cur(2), prev(2), nxt(2), st_spec] + conv_specs + w_specs + [
            const((4, POOL_GROUP, POOL_GROUP)), const((1, D_POOL))]
        args = [z3, z3, z3, z3, z3, z3, h0] + conv_args + w_args + [prm["pool_w"], prm["pool_scale"]]
        out_specs = [tile_out, st_spec, tile_out, tile_out]
        out_shape = [jax.ShapeDtypeStruct((nseq, seq_len, D_RNN), F32),
                     jax.ShapeDtypeStruct((nseq, 1, D_RNN), F32),
                     jax.ShapeDtypeStruct((nseq, seq_len, D_POOL), BF16),
                     jax.ShapeDtypeStruct((nseq, seq_len, D_RNN), F32)]
        n2 = ts + 2 * HALO
        scratch = [pltpu.VMEM((n2, D_RNN), F32)] + scratch + [
            pltpu.VMEM((n2, D_POOL), F32), pltpu.VMEM((n2, D_POOL - LANES), F32),
            pltpu.VMEM((n2, D_POOL - 2 * LANES), F32)]
        name = "l0_lru_fwd_pool"
    else:
        hf, xc = fwd
        in_specs = [tile_out, cur(1), tile_out, st_spec] + w_specs
        args = [xc, z3, hf, h0] + w_args
        out_specs = [tile_out, st_spec]
        out_shape = [jax.ShapeDtypeStruct((nseq, seq_len, D_RNN), BF16),
                     jax.ShapeDtypeStruct((nseq, 1, D_RNN), F32)]
        name = "l0_lru_bwd"
    return pl.pallas_call(
        kern,
        grid=(nseq, nt),
        in_specs=in_specs,
        out_specs=out_specs,
        out_shape=out_shape,
        scratch_shapes=scratch,
        compiler_params=_cparams(("arbitrary", "arbitrary")),
        name=name,
    )(*args)


FF_CHUNK = 1024


def _ffn_kernel(*refs, n_y, final_norm):
    x_ref = refs[0]
    y_refs = refs[1:1 + n_y]
    mod_ref = refs[1 + n_y]
    wo_refs = refs[2 + n_y:2 + 2 * n_y]
    g_ref, w1_ref, w2_ref = refs[2 + 2 * n_y:5 + 2 * n_y]
    rest = refs[5 + 2 * n_y:]
    if final_norm:
        gf_ref, o_ref = rest
    else:
        (o_ref,) = rest

    def mod(k):
        return mod_ref[0, :, k * D_MODEL:(k + 1) * D_MODEL]

    mix = _dot(y_refs[0][...], wo_refs[0][...])
    for k in range(1, n_y):
        mix = mix + _dot(y_refs[k][...], wo_refs[k][...])
    x1 = x_ref[...] + mod(2) * mix
    hn = _rms_mod(x1, g_ref[...], mod(3), mod(4)).astype(BF16)
    acc = None
    for c in range(D_FF // FF_CHUNK):
        hc = _dot(hn, w1_ref[:, c * FF_CHUNK:(c + 1) * FF_CHUNK])
        hc = jnp.square(jnp.maximum(hc, 0.0)).astype(BF16)
        part = _dot(hc, w2_ref[c * FF_CHUNK:(c + 1) * FF_CHUNK, :])
        acc = part if acc is None else acc + part
    x2 = x1 + mod(5) * acc
    if final_norm:
        ms = jnp.mean(x2 * x2, axis=-1, keepdims=True)
        x2 = x2 * lax.rsqrt(ms + EPS) * gf_ref[...]
    o_ref[...] = x2


def _ffn_call(x2d, ys, mod3, wos, g, w1, w2, g_final, tm):
    t = x2d.shape[0]
    nb = mod3.shape[0]
    rows_per_b = t // nb
    n_y = len(ys)
    final_norm = g_final is not None

    def const(shape):
        return pl.BlockSpec(shape, lambda i: (0, 0), pipeline_mode=pl.Buffered(1))

    in_specs = [pl.BlockSpec((tm, D_MODEL), lambda i: (i, 0))]
    in_specs += [pl.BlockSpec((tm, y.shape[1]), lambda i: (i, 0)) for y in ys]
    in_specs += [pl.BlockSpec((1, 1, 6 * D_MODEL), lambda i: ((i * tm) // rows_per_b, 0, 0))]
    in_specs += [const(w.shape) for w in wos]
    in_specs += [const((1, D_MODEL)), const((D_MODEL, D_FF)), const((D_FF, D_MODEL))]
    args = [x2d, *ys, mod3, *wos, g.reshape(1, D_MODEL), w1, w2]
    if final_norm:
        in_specs.append(const((1, D_MODEL)))
        args.append(g_final.reshape(1, D_MODEL))
    return pl.pallas_call(
        functools.partial(_ffn_kernel, n_y=n_y, final_norm=final_norm),
        grid=(t // tm,),
        in_specs=in_specs,
        out_specs=pl.BlockSpec((tm, D_MODEL), lambda i: (i, 0)),
        out_shape=jax.ShapeDtypeStruct((t, D_MODEL), F32),
        compiler_params=_cparams(("arbitrary",)),
        name="mix_out_ffn",
    )(*args)


def _rope(x, cos, sa, sb):
    return x * cos + pltpu.roll(x, 8, axis=1) * sa + pltpu.roll(x, LANES - 8, axis=1) * sb


def _mla_proj_kernel(*refs, rope, emit_cache):
    x_ref, mod_ref, g_ref, win_ref, gq_ref, wqb_ref, gkv_ref, wk_ref, wv_ref, vones_ref = refs[:10]
    rest = refs[10:]
    if rope:
        cos_ref, sa_ref, sb_ref = rest[:3]
        rest = rest[3:]
    q_ref, k_ref, v_ref = rest[:3]
    if emit_cache:
        ckv_ref, kpe_ref = rest[3:]

    shift = mod_ref[0, :, 0:D_MODEL]
    scale = mod_ref[0, :, D_MODEL:2 * D_MODEL]
    h = _rms_mod(x_ref[...], g_ref[...], shift, scale).astype(BF16)
    z = _dot(h, win_ref[...])
    cq = z[:, :Q_RANK]
    ckv = z[:, Q_RANK:Q_RANK + KV_RANK]
    kpe = z[:, Q_RANK + KV_RANK:]
    cqn = cq * lax.rsqrt(jnp.mean(cq * cq, axis=-1, keepdims=True) + EPS) * gq_ref[...]
    ckvn = ckv * lax.rsqrt(jnp.mean(ckv * ckv, axis=-1, keepdims=True) + EPS) * gkv_ref[...]
    if emit_cache:
        ckv_ref[...] = ckvn
        kpe_ref[...] = kpe
    q = _dot(cqn.astype(BF16), wqb_ref[...]) * Q_PRESCALE
    ckvb = ckvn.astype(BF16)
    kn = _dot(ckvb, wk_ref[...])
    v_ref[...] = (_dot(ckvb, wv_ref[...]) + vones_ref[...]).astype(BF16)
    if rope:
        cos, sa, sb = cos_ref[...], sa_ref[...], sb_ref[...]
        kpe = _rope(kpe, cos, sa, sb)
    for hd in range(N_HEADS):
        sl = slice(hd * HEAD_PAD, (hd + 1) * HEAD_PAD)
        qh = q[:, sl]
        if rope:
            qh = _rope(qh, cos, sa, sb)
        q_ref[:, sl] = qh.astype(BF16)
        k_ref[:, sl] = (kn[:, sl] + kpe).astype(BF16)


def _mla_proj_call(x2d, mod3, prm, tables, *, seq_len, emit_cache, tm):
    t = x2d.shape[0]
    nb = mod3.shape[0]
    rows_per_b = t // nb
    rope = tables is not None
    tiles_per_seq = seq_len // tm
    hw = N_HEADS * HEAD_PAD

    def const(shape):
        return pl.BlockSpec(shape, lambda i: (0, 0))

    def rows(n):
        return pl.BlockSpec((tm, n), lambda i: (i, 0))

    in_specs = [rows(D_MODEL),
                pl.BlockSpec((1, 1, 6 * D_MODEL), lambda i: ((i * tm) // rows_per_b, 0, 0)),
                const((1, D_MODEL)), const(prm["w_in"].shape), const((1, Q_RANK)), const(prm["w_qb"].shape),
                const((1, KV_RANK)), const(prm["w_k"].shape), const(prm["w_v"].shape), const((1, hw))]
    args = [x2d, mod3, prm["g_mix"], prm["w_in"], prm["g_q"], prm["w_qb"], prm["g_kv"], prm["w_k"], prm["w_v"],
            prm["v_ones"]]
    if rope:
        in_specs += [pl.BlockSpec((tm, LANES), lambda i: (i % tiles_per_seq, 0))] * 3
        args += list(tables)
    out_specs = [rows(hw), rows(hw), rows(hw)]
    out_shape = [jax.ShapeDtypeStruct((t, hw), BF16)] * 3
    if emit_cache:
        out_specs += [rows(KV_RANK), rows(LANES)]
        out_shape += [jax.ShapeDtypeStruct((t, KV_RANK), F32), jax.ShapeDtypeStruct((t, LANES), F32)]
    return pl.pallas_call(
        functools.partial(_mla_proj_kernel, rope=rope, emit_cache=emit_cache),
        grid=(t // tm,),
        in_specs=in_specs,
        out_specs=out_specs,
        out_shape=out_shape,
        compiler_params=_cparams(("arbitrary",)),
        name="l1_mla_proj",
    )(*args)


def _ctx_expand_kernel(ckv_ref, kpe_ref, wk_ref, wv_ref, vones_ref, k_ref, v_ref):
    ckvb = ckv_ref[...].astype(BF16)
    kn = _dot(ckvb, wk_ref[...])
    v_ref[...] = (_dot(ckvb, wv_ref[...]) + vones_ref[...]).astype(BF16)
    kpe = kpe_ref[...]
    for hd in range(N_HEADS):
        sl = slice(hd * HEAD_PAD, (hd + 1) * HEAD_PAD)
        k_ref[:, sl] = (kn[:, sl] + kpe).astype(BF16)


def _ctx_expand_call(ckv2d, kpe2d, prm, tm):
    hw = N_HEADS * HEAD_PAD
    t = ckv2d.shape[0]
    const = lambda a: pl.BlockSpec(a.shape, lambda i: (0, 0))
    rows = lambda n: pl.BlockSpec((tm, n), lambda i: (i, 0))
    return pl.pallas_call(
        _ctx_expand_kernel,
        grid=(t // tm,),
        in_specs=[rows(KV_RANK), rows(LANES), const(prm["w_k"]), const(prm["w_v"]), const(prm["v_ones"])],
        out_specs=[rows(hw), rows(hw)],
        out_shape=[jax.ShapeDtypeStruct((t, hw), BF16), jax.ShapeDtypeStruct((t, hw), BF16)],
        compiler_params=_cparams(("arbitrary",)),
        name="l1_ctx_expand",
    )(ckv2d, kpe2d, prm["w_k"], prm["w_v"], prm["v_ones"])


Q_PRESCALE = (QK_NOPE + QK_ROPE) ** -0.5 * math.log2(math.e)
_NT = (((1,), (1,)), ((), ()))
ROW_SUB = 256


def _pair_out(acc_even, acc_odd):
    lane = lax.broadcasted_iota(jnp.int32, acc_even.shape, 1)
    first = lane < V_HEAD
    num = jnp.where(first, acc_even, acc_odd)
    den = jnp.where(first, pltpu.roll(acc_even, V_HEAD, axis=1), pltpu.roll(acc_odd, V_HEAD, axis=1))
    return (num / den).astype(BF16)


def _attn_ctx_kernel(q_ref, k_ref, v_ref, o_ref, s_scr, p_scr, *, pairs):
    heads = [slice(h * HEAD_PAD, (h + 1) * HEAD_PAD) for h in range(2 * pairs)]
    for h, sl in enumerate(heads):
        s_scr[h] = lax.dot_general(q_ref[0, :, sl], k_ref[0, :, sl], _NT, preferred_element_type=F32)
    for h in range(2 * pairs):
        s = s_scr[h]
        p_scr[h] = jnp.exp2(s - s.max(axis=-1, keepdims=True)).astype(BF16)
    for p in range(pairs):
        accs = [_dot(p_scr[2 * p + e], v_ref[0, :, heads[2 * p + e]]) for e in range(2)]
        o_ref[0, :, p * LANES:(p + 1) * LANES] = _pair_out(accs[0], accs[1])


def _attn_ctx_call(q3, k3, v3):
    b, s, _ = q3.shape
    spec = lambda w: pl.BlockSpec((1, s, w), lambda bi: (bi, 0, 0))
    hw = N_HEADS * HEAD_PAD
    return pl.pallas_call(
        functools.partial(_attn_ctx_kernel, pairs=N_HEADS // 2),
        grid=(b,),
        in_specs=[spec(hw), spec(hw), spec(hw)],
        out_specs=spec(N_HEADS * V_HEAD),
        out_shape=jax.ShapeDtypeStruct((b, s, N_HEADS * V_HEAD), BF16),
        scratch_shapes=[pltpu.VMEM((N_HEADS, s, s), F32), pltpu.VMEM((N_HEADS, s, s), BF16)],
        compiler_params=_cparams(("arbitrary",)),
        name="l1_attention_ctx",
    )(q3, k3, v3)


def _attn_lat_kernel(q_ref, k1_ref, k2_ref, v1c_ref, v2c_ref, v1l_ref, v2l_ref, o_ref,
                     s0, s1, m0, m1, p0, p1, acc0_save):
    n = pl.program_id(0)
    n1, n2 = k1_ref.shape[1], k2_ref.shape[1]
    h0 = slice(0, HEAD_PAD)
    h1 = slice(HEAD_PAD, 2 * HEAD_PAD)

    @pl.when(n == 0)
    def _():
        s1[...] = jnp.zeros(s1.shape, F32)
        m1[...] = jnp.zeros(m1.shape, F32)
        acc0_save[...] = jnp.ones(acc0_save.shape, F32)

    row_blocks = [slice(r, r + ROW_SUB) for r in range(0, q_ref.shape[1], ROW_SUB)]

    def scores(sl, s_w, m_w):
        for rows in row_blocks:
            q = q_ref[0, rows, sl]
            sa = lax.dot_general(q, k1_ref[0, :, sl], _NT, preferred_element_type=F32)
            sb = lax.dot_general(q, k2_ref[0, :, sl], _NT, preferred_element_type=F32)
            s_w[rows, 0:n1] = sa
            s_w[rows, n1:] = sb
            m = jnp.maximum(sa.max(axis=-1, keepdims=True), sb.max(axis=-1, keepdims=True))
            m_w[rows, :] = jnp.broadcast_to(m, (ROW_SUB, LANES))

    def values(sl, s_r, m_r, p_scr, va_ref, vb_ref):
        for rows in row_blocks:
            m = m_r[rows, :]
            for t in range((n1 + n2) // LANES):
                tl = slice(t * LANES, (t + 1) * LANES)
                p_scr[rows, tl] = jnp.exp2(s_r[rows, tl] - m).astype(BF16)
        return _dot(p_scr[:, 0:n1], va_ref[0, :, sl]) + _dot(p_scr[:, n1:], vb_ref[0, :, sl])

    @pl.when(n >= 0)
    def _():
        scores(h0, s0, m0)
        acc1 = values(h1, s1, m1, p1, v1l_ref, v2l_ref)
        o_ref[0] = _pair_out(acc0_save[...], acc1)

    @pl.when(n < pl.num_programs(0))
    def _():
        scores(h1, s1, m1)
        acc0_save[...] = values(h0, s0, m0, p0, v1c_ref, v2c_ref)


def _attn_lat_call(q3, k1, k2, v1, v2, *, tq):
    b, s, _ = q3.shape
    s1, s2 = k1.shape[1], k2.shape[1]
    sk = s1 + s2
    npair = N_HEADS // 2
    nq = s // tq
    n_items = b * npair * nq

    def item(n):
        return n // (npair * nq), n % nq, (n // nq) % npair

    def cur(n):
        return item(jnp.minimum(n, n_items - 1))

    def lag(n):
        return item(jnp.maximum(n - 1, 0))

    qw = 2 * HEAD_PAD
    kv_cur = lambda rows: pl.BlockSpec((1, rows, qw), lambda n: (cur(n)[0], 0, cur(n)[2]))
    kv_lag = lambda rows: pl.BlockSpec((1, rows, qw), lambda n: (lag(n)[0], 0, lag(n)[2]))
    in_specs = [pl.BlockSpec((1, tq, qw), lambda n: cur(n)),
                kv_cur(s1), kv_cur(s2), kv_cur(s1), kv_cur(s2), kv_lag(s1), kv_lag(s2)]
    return pl.pallas_call(
        _attn_lat_kernel,
        grid=(n_items + 1,),
        in_specs=in_specs,
        out_specs=pl.BlockSpec((1, tq, LANES), lambda n: lag(n)),
        out_shape=jax.ShapeDtypeStruct((b, s, N_HEADS * V_HEAD), BF16),
        scratch_shapes=[pltpu.VMEM((tq, sk), F32)] * 2 + [pltpu.VMEM((tq, LANES), F32)] * 2
        + [pltpu.VMEM((tq, sk), BF16)] * 2 + [pltpu.VMEM((tq, LANES), F32)],
        compiler_params=_cparams(("arbitrary",)),
        name="l1_attention_lat",
    )(q3, k1, k2, v1, v2, v1, v2)


def _block_diag_halves(w):
    w4 = w.astype(BF16).reshape(2, 4, RNN_BLOCK, RNN_BLOCK)
    eye = jnp.eye(4, dtype=BF16)
    return (w4[:, :, :, None, :] * eye[None, :, None, :, None]).reshape(2, 4 * RNN_BLOCK, 4 * RNN_BLOCK)


def _rope_tables(seq_len):
    rows = seq_len // GRID_W
    row = np.repeat(np.arange(rows, dtype=np.float64), GRID_W)
    col = np.tile(np.arange(GRID_W, dtype=np.float64), rows)
    half = QK_ROPE // 2
    inv = ROPE_BASE ** (-np.arange(0, half, 2, dtype=np.float64) / half)
    ang_r, ang_c = row[:, None] * inv, col[:, None] * inv
    zeros = np.zeros((seq_len, 8))
    cr, sr, cc, sc = np.cos(ang_r), np.sin(ang_r), np.cos(ang_c), np.sin(ang_c)
    lead = np.ones((seq_len, QK_NOPE))
    tail = np.ones((seq_len, HEAD_PAD - QK_NOPE - QK_ROPE))
    cos = np.concatenate([lead, cr, cr, cc, cc, tail], axis=1)
    sa = np.concatenate([0 * lead, zeros, sr, zeros, sc, 0 * tail], axis=1)
    sb = np.concatenate([0 * lead, -sr, zeros, -sc, zeros, 0 * tail], axis=1)
    return tuple(jnp.asarray(t, dtype=F32) for t in (cos, sa, sb))


def kernel(x_prompt, x_sample, state_l0_lru, cache_l1_ckv, cache_l1_kpe, c, c_ctx, l0_w_mod, l0_b_mod, l0_g_mix, l0_g_ffn, l0_w_in, l0_conv_w, l0_conv_b, l0_lru_w_a, l0_lru_b_a, l0_lru_w_i, l0_lru_b_i, l0_lru_lam, l0_pool_w, l0_pool_scale, l0_w_out, l0_ffn_w1, l0_ffn_w2, l1_w_mod, l1_b_mod, l1_g_mix, l1_g_ffn, l1_w_in, l1_g_q, l1_w_qb, l1_g_kv, l1_w_kvb, l1_w_out, l1_ffn_w1, l1_ffn_w2, g_final):
    bp, sp, d = x_prompt.shape
    bs, ss, _ = x_sample.shape
    past = cache_l1_ckv.shape[1]
    xp = x_prompt.reshape(bp * sp, d)
    xs = x_sample.reshape(bs * ss, d)

    cond8 = jnp.concatenate([c, c_ctx[None, :], jnp.zeros((SUBLANES - bs - 1, d), F32)], axis=0)
    m0 = _mod_call(cond8, l0_w_mod, l0_b_mod)
    m1 = _mod_call(cond8, l1_w_mod, l1_b_mod)
    mod_p = [m[bs:bs + 1].reshape(1, 1, 6 * d) for m in (m0, m1)]
    mod_s = [m[0:bs].reshape(bs, 1, 6 * d) for m in (m0, m1)]

    lru = dict(
        conv_w=l0_conv_w, conv_b=l0_conv_b.reshape(1, D_RNN),
        wa=[_block_diag_halves(l0_lru_w_a[i]) for i in range(2)],
        wi=[_block_diag_halves(l0_lru_w_i[i]) for i in range(2)],
        ba=[l0_lru_b_a[i].reshape(1, D_RNN) for i in range(2)],
        bi=[l0_lru_b_i[i].reshape(1, D_RNN) for i in range(2)],
        lam=[l0_lru_lam[i].reshape(1, D_RNN) for i in range(2)],
        pool_w=l0_pool_w.astype(BF16), pool_scale=l0_pool_scale.reshape(1, D_POOL))
    w_in0 = l0_w_in.astype(BF16)
    wo0 = l0_w_out.astype(BF16)
    wos0 = [wo0[:D_RNN], wo0[D_RNN:]]
    w1_0, w2_0 = l0_ffn_w1.astype(BF16), l0_ffn_w2.astype(BF16)

    def layer0(x2d, mod3, nseq, seq_len, h0f, h0b, ts):
        z = _l0_in_call(x2d, mod3, l0_g_mix, w_in0, tm=512)
        z3 = z.reshape(nseq, seq_len, 3 * D_RNN)
        hf, stf, ypool, xc = _lru_call(z3, h0f, None, lru, reverse=False, ts=ts)
        yrnn, stb = _lru_call(z3, h0b, (hf, xc), lru, reverse=True, ts=ts)
        ys = [yrnn.reshape(-1, D_RNN), ypool.reshape(-1, D_POOL)]
        x2 = _ffn_call(x2d, ys, mod3, wos0, l0_g_ffn, w1_0, w2_0, None, tm=512)
        return x2, stf, stb

    zero_st = jnp.zeros((bp, 1, D_RNN), F32)
    xp, stf, stb = layer0(xp, mod_p[0], bp, sp, zero_st, zero_st, ts=sp)
    xs, _, _ = layer0(xs, mod_s[0], bs, ss, state_l0_lru[:, 0:1], state_l0_lru[:, 1:2], ts=512)
    new_lru = jnp.concatenate([stf, stb], axis=1)

    pad_pe = HEAD_PAD - QK_NOPE - QK_ROPE
    w_in1b = l1_w_in.astype(BF16)
    w_in1 = jnp.concatenate([w_in1b[:, :Q_RANK + KV_RANK], jnp.zeros((d, QK_NOPE), BF16),
                             w_in1b[:, Q_RANK + KV_RANK:], jnp.zeros((d, pad_pe), BF16)], axis=1)
    wqb = l1_w_qb.astype(BF16).reshape(Q_RANK, N_HEADS, QK_NOPE + QK_ROPE)
    wqb = jnp.pad(wqb, ((0, 0), (0, 0), (0, pad_pe))).reshape(Q_RANK, N_HEADS * HEAD_PAD)
    wkvb = l1_w_kvb.astype(BF16).reshape(KV_RANK, N_HEADS, QK_NOPE + V_HEAD)
    w_k = jnp.pad(wkvb[:, :, :QK_NOPE], ((0, 0), (0, 0), (0, HEAD_PAD - QK_NOPE)))
    w_k = w_k.reshape(KV_RANK, N_HEADS * HEAD_PAD)
    wv = wkvb[:, :, QK_NOPE:].reshape(KV_RANK, N_HEADS // 2, 2, V_HEAD)
    zv = jnp.zeros((KV_RANK, N_HEADS // 2, V_HEAD), BF16)
    w_v = jnp.stack([wv[:, :, 0], zv, zv, wv[:, :, 1]], axis=2).reshape(KV_RANK, N_HEADS * HEAD_PAD)
    one = jnp.ones((N_HEADS // 2, V_HEAD), F32)
    v_ones = jnp.stack([0 * one, one, one, 0 * one], axis=1).reshape(1, N_HEADS * HEAD_PAD)
    mla = dict(g_mix=l1_g_mix.reshape(1, d), w_in=w_in1, g_q=l1_g_q.reshape(1, Q_RANK), w_qb=wqb,
               g_kv=l1_g_kv.reshape(1, KV_RANK), w_k=w_k, w_v=w_v, v_ones=v_ones)
    wo1 = [l1_w_out.astype(BF16)]
    w1_1, w2_1 = l1_ffn_w1.astype(BF16), l1_ffn_w2.astype(BF16)

    qp, kp, vp, ckv_new, kpe_new = _mla_proj_call(xp, mod_p[1], mla, None, seq_len=bp * sp, emit_cache=True,
                                                  tm=512)
    op = _attn_ctx_call(qp.reshape(bp, sp, -1), kp.reshape(bp, sp, -1), vp.reshape(bp, sp, -1))
    y_prompt = _ffn_call(xp, [op.reshape(bp * sp, -1)], mod_p[1], wo1, l1_g_ffn, w1_1, w2_1, g_final, tm=512)

    tables = _rope_tables(ss)
    qs, ks, vs = _mla_proj_call(xs, mod_s[1], mla, tables, seq_len=ss, emit_cache=False, tm=512)
    kpe_ctx = jnp.pad(cache_l1_kpe.reshape(bs * past, QK_ROPE), ((0, 0), (QK_NOPE, pad_pe)))
    kc, vc = _ctx_expand_call(cache_l1_ckv.reshape(bs * past, KV_RANK), kpe_ctx, mla, tm=past)
    os_ = _attn_lat_call(qs.reshape(bs, ss, -1), ks.reshape(bs, ss, -1), kc.reshape(bs, past, -1),
                         vs.reshape(bs, ss, -1), vc.reshape(bs, past, -1), tq=512)
    y_sample = _ffn_call(xs, [os_.reshape(bs * ss, -1)], mod_s[1], wo1, l1_g_ffn, w1_1, w2_1, g_final, tm=512)

    new_ckv = ckv_new.reshape(bp, sp, KV_RANK)
    new_kpe = kpe_new[:, QK_NOPE:QK_NOPE + QK_ROPE].reshape(bp, sp, QK_ROPE)
    return (y_prompt.reshape(bp, sp, d), y_sample.reshape(bs, ss, d), new_lru, new_ckv, new_kpe)
```

```python
import functools
import math

import jax
import jax.numpy as jnp
import numpy as np
from jax import lax
from jax.experimental import pallas as pl
from jax.experimental.pallas import tpu as pltpu

F32 = jnp.float32
BF16 = jnp.bfloat16

D_MODEL = 1024
D_FF = 4 * D_MODEL
EPS = 1e-6
D_RNN = 512
RNN_BLOCKS = 8
RNN_BLOCK = 64
CONV_W = 4
LRU_C = 8.0
D_POOL = 512
POOL_WINDOWS = (2, 4, 8, 16)
POOL_GROUP = 128
N_HEADS = 16
QK_NOPE = 64
QK_ROPE = 32
V_HEAD = 64
Q_RANK = 384
KV_RANK = 256
ROPE_BASE = 10000.0
GRID_W = 64

LANES = 128
SUBLANES = 8
HEAD_PAD = 128
HALO = 8
VMEM_LIMIT = 52 * 1024 * 1024


def _cparams(sem):
    return pltpu.CompilerParams(dimension_semantics=sem, vmem_limit_bytes=VMEM_LIMIT)


def _dot(a, b):
    return jnp.dot(a, b, preferred_element_type=F32)


def _sigmoid(x):
    return 0.5 * jnp.tanh(0.5 * x) + 0.5


def _rms_mod(x, g, shift, scale):
    ms = jnp.mean(x * x, axis=-1, keepdims=True)
    y = x * lax.rsqrt(ms + EPS) * g
    return y * (1.0 + scale) + shift


def _mod_kernel(c_ref, w_ref, b_ref, o_ref):
    c = c_ref[...]
    s = c * jax.nn.sigmoid(c)
    o_ref[...] = _dot(s.astype(BF16), w_ref[...].astype(BF16)) + b_ref[...]


def _mod_call(cond8, w_mod, b_mod):
    n = w_mod.shape[1]
    tn = 1024
    return pl.pallas_call(
        _mod_kernel,
        grid=(n // tn,),
        in_specs=[
            pl.BlockSpec((SUBLANES, D_MODEL), lambda j: (0, 0)),
            pl.BlockSpec((D_MODEL, tn), lambda j: (0, j)),
            pl.BlockSpec((1, tn), lambda j: (0, j)),
        ],
        out_specs=pl.BlockSpec((SUBLANES, tn), lambda j: (0, j)),
        out_shape=jax.ShapeDtypeStruct((SUBLANES, n), F32),
        compiler_params=_cparams(("arbitrary",)),
        name="adaln_mod",
    )(cond8, w_mod, b_mod.reshape(1, n))


def _lru_coeffs(xc, wa_ref, ba_ref, wi_ref, bi_ref, lam_ref):
    xcb = xc.astype(BF16)
    half = D_RNN // 2
    ga = jnp.concatenate([_dot(xcb[:, :half], wa_ref[0]), _dot(xcb[:, half:], wa_ref[1])], axis=-1)
    gi = jnp.concatenate([_dot(xcb[:, :half], wi_ref[0]), _dot(xcb[:, half:], wi_ref[1])], axis=-1)
    r = _sigmoid(ga + ba_ref[...])
    i = _sigmoid(gi + bi_ref[...])
    log_a = (LRU_C * r) * jax.nn.log_sigmoid(lam_ref[...])
    a = jnp.exp(log_a)
    v = -jnp.tanh(log_a) * (a * a + 1.0)
    u = jnp.where(v > 0.0, v * lax.rsqrt(v), 0.0) * (i * xc)
    return a, u


def _scan_tile(a, u, carry_row, reverse, a_s, u_s, ts):
    g = ts // SUBLANES
    a3 = a.reshape(g, SUBLANES, D_RNN)
    u3 = u.reshape(g, SUBLANES, D_RNN)
    row = lax.broadcasted_iota(jnp.int32, (g, SUBLANES, D_RNN), 1)
    for d in (1, 2, 4):
        if reverse:
            sh = SUBLANES - d
            m = row < SUBLANES - d
        else:
            sh = d
            m = row >= d
        a_sh = pltpu.roll(a3, sh, axis=1)
        u_sh = pltpu.roll(u3, sh, axis=1)
        u3 = u3 + a3 * jnp.where(m, u_sh, 0.0)
        a3 = a3 * jnp.where(m, a_sh, 1.0)
    a_s[...] = a3.reshape(ts, D_RNN)
    u_s[...] = u3.reshape(ts, D_RNN)
    edge = 0 if reverse else SUBLANES - 1

    def body(k, c):
        gi = (g - 1 - k) if reverse else k
        r0 = pl.multiple_of(gi * SUBLANES, SUBLANES)
        h = u_s[pl.ds(r0, SUBLANES), :] + a_s[pl.ds(r0, SUBLANES), :] * c
        u_s[pl.ds(r0, SUBLANES), :] = h
        return jnp.broadcast_to(h[edge:edge + 1, :], (SUBLANES, D_RNN))

    c0 = jnp.broadcast_to(carry_row, (SUBLANES, D_RNN))
    c_last = lax.fori_loop(0, g, body, c0, unroll=4)
    return c_last[0:1, :]


def _lru_kernel(*refs, reverse, with_pool, ts, nt, seq_len):
    if with_pool:
        (x_ref, xprev_ref, xnext_ref, mod_ref, g_ref, win_ref, h0_ref, cw_ref, cb_ref, wa_ref, ba_ref, wi_ref,
         bi_ref, lam_ref, pw_ref, ps_ref, h_ref, st_ref, yp_ref, xc_ref, gate_out_ref,
         xe_s, ext_s, a_s, u_s, carry_s, b2_s, b4_s, b8_s) = refs
    else:
        (xc_in_ref, gate_ref, hf_ref, h0_ref, wa_ref, ba_ref, wi_ref,
         bi_ref, lam_ref, y_ref, st_ref, a_s, u_s, carry_s) = refs
    j = pl.program_id(1)
    t = (nt - 1 - j) if reverse else j
    n2 = ts + 2 * HALO

    @pl.when(j == 0)
    def _():
        carry_s[...] = h0_ref[0]

    if with_pool:
        xe_s[0:HALO, :] = xprev_ref[0]
        xe_s[HALO:HALO + ts, :] = x_ref[0]
        xe_s[HALO + ts:n2, :] = xnext_ref[0]
        shift = mod_ref[0, :, 0:D_MODEL]
        scale = mod_ref[0, :, D_MODEL:2 * D_MODEL]
        hx = _rms_mod(xe_s[...], g_ref[...], shift, scale).astype(BF16)
        z = _dot(hx, win_ref[...])
        row = lax.broadcasted_iota(jnp.int32, (n2, D_RNN), 0)
        inside = ((row >= HALO) | (t > 0)) & ((row < HALO + ts) | (t < nt - 1))
        gate_out_ref[0] = z[HALO:HALO + ts, D_RNN:2 * D_RNN]
        ext_s[...] = jnp.where(inside, z[:, 0:D_RNN], 0.0)

        xc = cb_ref[...]
        for k in range(CONV_W):
            off = HALO - CONV_W // 2 + k
            xc = xc + ext_s[off:off + ts, :] * cw_ref[k:k + 1, :]
        xc_ref[0] = xc
        ext_s[...] = jnp.where(inside, z[:, 2 * D_RNN:], 0.0)
    else:
        xc = xc_in_ref[0]

    a, u = _lru_coeffs(xc, wa_ref, ba_ref, wi_ref, bi_ref, lam_ref)
    c_last = _scan_tile(a, u, carry_s[...], reverse, a_s, u_s, ts)
    carry_s[...] = c_last
    st_ref[0] = c_last
    h = u_s[...]

    if not with_pool:
        y_ref[0] = ((hf_ref[0] + h) * jax.nn.gelu(gate_ref[0])).astype(BF16)
        return

    h_ref[0] = h

    b2_s[1:n2, :] = ext_s[1:n2, :] + ext_s[0:n2 - 1, :]
    b4_s[2:n2 - 1, :] = b2_s[3:n2, LANES:] + b2_s[1:n2 - 2, LANES:]
    b8_s[4:n2 - 3, :] = b4_s[6:n2 - 1, LANES:] + b4_s[2:n2 - 5, LANES:]
    s16 = b8_s[HALO + 4:HALO + 4 + ts, LANES:] + b8_s[HALO - 4:HALO - 4 + ts, LANES:]
    sums = (b2_s[HALO:HALO + ts, 0:LANES], b4_s[HALO:HALO + ts, 0:LANES], b8_s[HALO:HALO + ts, 0:LANES], s16)
    pos = t * ts + lax.broadcasted_iota(jnp.int32, (ts, LANES), 0)
    for gidx, w in enumerate(POOL_WINDOWS):
        left = w // 2
        right = w - 1 - left
        lo = jnp.maximum(pos - left, 0)
        hi = jnp.minimum(pos + right, seq_len - 1) + 1
        cnt = (hi - lo).astype(F32)
        xg = ext_s[HALO:HALO + ts, gidx * LANES:(gidx + 1) * LANES]
        dlt = sums[gidx] / cnt - xg
        yg = _dot(dlt.astype(BF16), pw_ref[gidx]) * ps_ref[:, gidx * LANES:(gidx + 1) * LANES]
        yp_ref[0, :, gidx * LANES:(gidx + 1) * LANES] = yg.astype(BF16)


def _lru_call(x3, mod3, h0, fwd, prm, *, reverse, ts):
    nseq, seq_len = (x3 if not reverse else fwd[0]).shape[:2]
    nt = seq_len // ts
    tb = ts // HALO
    nhb = seq_len // HALO
    d = 1 if reverse else 0
    with_pool = not reverse

    def tpos(j):
        return (nt - 1 - j) if reverse else j

    def const(shape):
        nd = len(shape)
        return pl.BlockSpec(shape, lambda s, j: (0,) * nd)

    tile_out = pl.BlockSpec((1, ts, D_RNN), lambda s, j: (s, tpos(j), 0))
    st_spec = pl.BlockSpec((1, 1, D_RNN), lambda s, j: (s, 0, 0))
    conv_specs = [const((CONV_W, D_RNN)), const((1, D_RNN))]
    conv_args = [prm["conv_w"], prm["conv_b"]]
    w_specs = [const((2, 256, 256)), const((1, D_RNN)), const((2, 256, 256)), const((1, D_RNN)), const((1, D_RNN))]
    w_args = [prm["wa"][d], prm["ba"][d], prm["wi"][d], prm["bi"][d], prm["lam"][d]]
    scratch = [pltpu.VMEM((ts, D_RNN), F32), pltpu.VMEM((ts, D_RNN), F32), pltpu.VMEM((1, D_RNN), F32)]
    kern = functools.partial(_lru_kernel, reverse=reverse, with_pool=with_pool, ts=ts, nt=nt, seq_len=seq_len)
    if with_pool:
        nb = mod3.shape[0]
        x_specs = [
            pl.BlockSpec((1, ts, D_MODEL), lambda s, j: (s, j, 0)),
            pl.BlockSpec((1, HALO, D_MODEL), lambda s, j: (s, jnp.maximum(j * tb - 1, 0), 0)),
            pl.BlockSpec((1, HALO, D_MODEL), lambda s, j: (s, jnp.minimum((j + 1) * tb, nhb - 1), 0)),
            pl.BlockSpec((1, 1, 6 * D_MODEL), lambda s, j: ((s * nb) // nseq, 0, 0)),
            const((1, D_MODEL)), const(prm["w_in"].shape)]
        in_specs = x_specs + [st_spec] + conv_specs + w_specs + [
            const((4, POOL_GROUP, POOL_GROUP)), const((1, D_POOL))]
        args = [x3, x3, x3, mod3, prm["g_mix"], prm["w_in"], h0] + conv_args + w_args + [
            prm["pool_w"], prm["pool_scale"]]
        out_specs = [tile_out, st_spec, tile_out, tile_out, tile_out]
        out_shape = [jax.ShapeDtypeStruct((nseq, seq_len, D_RNN), F32),
                     jax.ShapeDtypeStruct((nseq, 1, D_RNN), F32),
                     jax.ShapeDtypeStruct((nseq, seq_len, D_POOL), BF16),
                     jax.ShapeDtypeStruct((nseq, seq_len, D_RNN), F32),
                     jax.ShapeDtypeStruct((nseq, seq_len, D_RNN), F32)]
        n2 = ts + 2 * HALO
        scratch = [pltpu.VMEM((n2, D_MODEL), F32), pltpu.VMEM((n2, D_RNN), F32)] + scratch + [
            pltpu.VMEM((n2, D_POOL), F32), pltpu.VMEM((n2, D_POOL - LANES), F32),
            pltpu.VMEM((n2, D_POOL - 2 * LANES), F32)]
        name = "l0_in_lru_fwd_pool"
    else:
        hf, xc, gate = fwd
        in_specs = [tile_out, tile_out, tile_out, st_spec] + w_specs
        args = [xc, gate, hf, h0] + w_args
        out_specs = [tile_out, st_spec]
        out_shape = [jax.ShapeDtypeStruct((nseq, seq_len, D_RNN), BF16),
                     jax.ShapeDtypeStruct((nseq, 1, D_RNN), F32)]
        name = "l0_lru_bwd"
    return pl.pallas_call(
        kern,
        grid=(nseq, nt),
        in_specs=in_specs,
        out_specs=out_specs,
        out_shape=out_shape,
        scratch_shapes=scratch,
        compiler_params=_cparams(("arbitrary", "arbitrary")),
        name=name,
    )(*args)


FF_CHUNK = 1024


def _residual_ffn(x, mix, mod_ref, g_ref, w1_ref, w2_ref):
    def mod(k):
        return mod_ref[0, :, k * D_MODEL:(k + 1) * D_MODEL]

    x1 = x + mod(2) * mix
    hn = _rms_mod(x1, g_ref[...], mod(3), mod(4)).astype(BF16)
    acc = None
    for c in range(D_FF // FF_CHUNK):
        hc = _dot(hn, w1_ref[:, c * FF_CHUNK:(c + 1) * FF_CHUNK])
        hc = jnp.square(jnp.maximum(hc, 0.0)).astype(BF16)
        part = _dot(hc, w2_ref[c * FF_CHUNK:(c + 1) * FF_CHUNK, :])
        acc = part if acc is None else acc + part
    return x1 + mod(5) * acc


def _ffn_kernel(*refs, n_y, final_norm):
    x_ref = refs[0]
    y_refs = refs[1:1 + n_y]
    mod_ref = refs[1 + n_y]
    wo_refs = refs[2 + n_y:2 + 2 * n_y]
    g_ref, w1_ref, w2_ref = refs[2 + 2 * n_y:5 + 2 * n_y]
    rest = refs[5 + 2 * n_y:]
    if final_norm:
        gf_ref, o_ref = rest
    else:
        (o_ref,) = rest

    mix = _dot(y_refs[0][...], wo_refs[0][...])
    for k in range(1, n_y):
        mix = mix + _dot(y_refs[k][...], wo_refs[k][...])
    x2 = _residual_ffn(x_ref[...], mix, mod_ref, g_ref, w1_ref, w2_ref)
    if final_norm:
        ms = jnp.mean(x2 * x2, axis=-1, keepdims=True)
        x2 = x2 * lax.rsqrt(ms + EPS) * gf_ref[...]
    o_ref[...] = x2


def _ffn_call(x2d, ys, mod3, wos, g, w1, w2, g_final, tm):
    t = x2d.shape[0]
    nb = mod3.shape[0]
    rows_per_b = t // nb
    n_y = len(ys)
    final_norm = g_final is not None

    def const(shape):
        return pl.BlockSpec(shape, lambda i: (0, 0), pipeline_mode=pl.Buffered(1))

    in_specs = [pl.BlockSpec((tm, D_MODEL), lambda i: (i, 0))]
    in_specs += [pl.BlockSpec((tm, y.shape[1]), lambda i: (i, 0)) for y in ys]
    in_specs += [pl.BlockSpec((1, 1, 6 * D_MODEL), lambda i: ((i * tm) // rows_per_b, 0, 0))]
    in_specs += [const(w.shape) for w in wos]
    in_specs += [const((1, D_MODEL)), const((D_MODEL, D_FF)), const((D_FF, D_MODEL))]
    args = [x2d, *ys, mod3, *wos, g.reshape(1, D_MODEL), w1, w2]
    if final_norm:
        in_specs.append(const((1, D_MODEL)))
        args.append(g_final.reshape(1, D_MODEL))
    return pl.pallas_call(
        functools.partial(_ffn_kernel, n_y=n_y, final_norm=final_norm),
        grid=(t // tm,),
        in_specs=in_specs,
        out_specs=pl.BlockSpec((tm, D_MODEL), lambda i: (i, 0)),
        out_shape=jax.ShapeDtypeStruct((t, D_MODEL), F32),
        compiler_params=_cparams(("arbitrary",)),
        name="mix_out_ffn",
    )(*args)


def _rope(x, cos, sa, sb):
    return x * cos + pltpu.roll(x, 8, axis=1) * sa + pltpu.roll(x, LANES - 8, axis=1) * sb


def _mla_proj_kernel(*refs, rope, emit_cache):
    x_ref, mod_ref, g_ref, win_ref, gq_ref, wqb_ref, gkv_ref, wk_ref, wv_ref, vones_ref = refs[:10]
    rest = refs[10:]
    if rope:
        cos_ref, sa_ref, sb_ref = rest[:3]
        rest = rest[3:]
    q_ref, k_ref, v_ref = rest[:3]
    if emit_cache:
        ckv_ref, kpe_ref = rest[3:]

    shift = mod_ref[0, :, 0:D_MODEL]
    scale = mod_ref[0, :, D_MODEL:2 * D_MODEL]
    h = _rms_mod(x_ref[...], g_ref[...], shift, scale).astype(BF16)
    z = _dot(h, win_ref[...])
    cq = z[:, :Q_RANK]
    ckv = z[:, Q_RANK:Q_RANK + KV_RANK]
    kpe = z[:, Q_RANK + KV_RANK:]
    cqn = cq * lax.rsqrt(jnp.mean(cq * cq, axis=-1, keepdims=True) + EPS) * gq_ref[...]
    ckvn = ckv * lax.rsqrt(jnp.mean(ckv * ckv, axis=-1, keepdims=True) + EPS) * gkv_ref[...]
    if emit_cache:
        ckv_ref[...] = ckvn
        kpe_ref[...] = kpe
    q = _dot(cqn.astype(BF16), wqb_ref[...]) * Q_PRESCALE
    ckvb = ckvn.astype(BF16)
    kn = _dot(ckvb, wk_ref[...])
    v_ref[...] = (_dot(ckvb, wv_ref[...]) + vones_ref[...]).astype(BF16)
    if rope:
        cos, sa, sb = cos_ref[...], sa_ref[...], sb_ref[...]
        kpe = _rope(kpe, cos, sa, sb)
    for hd in range(N_HEADS):
        sl = slice(hd * HEAD_PAD, (hd + 1) * HEAD_PAD)
        qh = q[:, sl]
        if rope:
            qh = _rope(qh, cos, sa, sb)
        q_ref[:, sl] = qh.astype(BF16)
        k_ref[:, sl] = (kn[:, sl] + kpe).astype(BF16)


def _mla_proj_call(x2d, mod3, prm, tables, *, seq_len, emit_cache, tm):
    t = x2d.shape[0]
    nb = mod3.shape[0]
    rows_per_b = t // nb
    rope = tables is not None
    tiles_per_seq = seq_len // tm
    hw = N_HEADS * HEAD_PAD

    def const(shape):
        return pl.BlockSpec(shape, lambda i: (0, 0))

    def rows(n):
        return pl.BlockSpec((tm, n), lambda i: (i, 0))

    in_specs = [rows(D_MODEL),
                pl.BlockSpec((1, 1, 6 * D_MODEL), lambda i: ((i * tm) // rows_per_b, 0, 0)),
                const((1, D_MODEL)), const(prm["w_in"].shape), const((1, Q_RANK)), const(prm["w_qb"].shape),
                const((1, KV_RANK)), const(prm["w_k"].shape), const(prm["w_v"].shape), const((1, hw))]
    args = [x2d, mod3, prm["g_mix"], prm["w_in"], prm["g_q"], prm["w_qb"], prm["g_kv"], prm["w_k"], prm["w_v"],
            prm["v_ones"]]
    if rope:
        in_specs += [pl.BlockSpec((tm, LANES), lambda i: (i % tiles_per_seq, 0))] * 3
        args += list(tables)
    out_specs = [rows(hw), rows(hw), rows(hw)]
    out_shape = [jax.ShapeDtypeStruct((t, hw), BF16)] * 3
    if emit_cache:
        out_specs += [rows(KV_RANK), rows(LANES)]
        out_shape += [jax.ShapeDtypeStruct((t, KV_RANK), F32), jax.ShapeDtypeStruct((t, LANES), F32)]
    return pl.pallas_call(
        functools.partial(_mla_proj_kernel, rope=rope, emit_cache=emit_cache),
        grid=(t // tm,),
        in_specs=in_specs,
        out_specs=out_specs,
        out_shape=out_shape,
        compiler_params=_cparams(("arbitrary",)),
        name="l1_mla_proj",
    )(*args)


def _ctx_expand_kernel(ckv_ref, kpe_ref, wk_ref, wv_ref, vones_ref, k_ref, v_ref):
    ckvb = ckv_ref[...].astype(BF16)
    kn = _dot(ckvb, wk_ref[...])
    v_ref[...] = (_dot(ckvb, wv_ref[...]) + vones_ref[...]).astype(BF16)
    kpe = kpe_ref[...]
    for hd in range(N_HEADS):
        sl = slice(hd * HEAD_PAD, (hd + 1) * HEAD_PAD)
        k_ref[:, sl] = (kn[:, sl] + kpe).astype(BF16)


def _ctx_expand_call(ckv2d, kpe2d, prm, tm):
    hw = N_HEADS * HEAD_PAD
    t = ckv2d.shape[0]
    const = lambda a: pl.BlockSpec(a.shape, lambda i: (0, 0))
    rows = lambda n: pl.BlockSpec((tm, n), lambda i: (i, 0))
    return pl.pallas_call(
        _ctx_expand_kernel,
        grid=(t // tm,),
        in_specs=[rows(KV_RANK), rows(LANES), const(prm["w_k"]), const(prm["w_v"]), const(prm["v_ones"])],
        out_specs=[rows(hw), rows(hw)],
        out_shape=[jax.ShapeDtypeStruct((t, hw), BF16), jax.ShapeDtypeStruct((t, hw), BF16)],
        compiler_params=_cparams(("arbitrary",)),
        name="l1_ctx_expand",
    )(ckv2d, kpe2d, prm["w_k"], prm["w_v"], prm["v_ones"])


Q_PRESCALE = (QK_NOPE + QK_ROPE) ** -0.5 * math.log2(math.e)
_NT = (((1,), (1,)), ((), ()))
ROW_SUB = 256


def _pair_out(acc_even, acc_odd):
    lane = lax.broadcasted_iota(jnp.int32, acc_even.shape, 1)
    first = lane < V_HEAD
    num = jnp.where(first, acc_even, acc_odd)
    den = jnp.where(first, pltpu.roll(acc_even, V_HEAD, axis=1), pltpu.roll(acc_odd, V_HEAD, axis=1))
    return (num / den).astype(BF16)


def _attn_ctx_kernel(q_ref, k_ref, v_ref, o_ref, s_scr, p_scr, *, pairs):
    heads = [slice(h * HEAD_PAD, (h + 1) * HEAD_PAD) for h in range(2 * pairs)]
    for h, sl in enumerate(heads):
        s_scr[h] = lax.dot_general(q_ref[0, :, sl], k_ref[0, :, sl], _NT, preferred_element_type=F32)
    for h in range(2 * pairs):
        s = s_scr[h]
        p_scr[h] = jnp.exp2(s - s.max(axis=-1, keepdims=True)).astype(BF16)
    for p in range(pairs):
        accs = [_dot(p_scr[2 * p + e], v_ref[0, :, heads[2 * p + e]]) for e in range(2)]
        o_ref[0, :, p * LANES:(p + 1) * LANES] = _pair_out(accs[0], accs[1])


def _attn_ctx_call(q3, k3, v3):
    b, s, _ = q3.shape
    spec = lambda w: pl.BlockSpec((1, s, w), lambda bi: (bi, 0, 0))
    hw = N_HEADS * HEAD_PAD
    return pl.pallas_call(
        functools.partial(_attn_ctx_kernel, pairs=N_HEADS // 2),
        grid=(b,),
        in_specs=[spec(hw), spec(hw), spec(hw)],
        out_specs=spec(N_HEADS * V_HEAD),
        out_shape=jax.ShapeDtypeStruct((b, s, N_HEADS * V_HEAD), BF16),
        scratch_shapes=[pltpu.VMEM((N_HEADS, s, s), F32), pltpu.VMEM((N_HEADS, s, s), BF16)],
        compiler_params=_cparams(("arbitrary",)),
        name="l1_attention_ctx",
    )(q3, k3, v3)


def _attn_lat_kernel(q_ref, k1_ref, k2_ref, v1c_ref, v2c_ref, v1l_ref, v2l_ref, o_ref,
                     s0, s1, m0, m1, p0, p1, acc0_save):
    n = pl.program_id(0)
    n1, n2 = k1_ref.shape[1], k2_ref.shape[1]
    h0 = slice(0, HEAD_PAD)
    h1 = slice(HEAD_PAD, 2 * HEAD_PAD)

    @pl.when(n == 0)
    def _():
        s1[...] = jnp.zeros(s1.shape, F32)
        m1[...] = jnp.zeros(m1.shape, F32)
        acc0_save[...] = jnp.ones(acc0_save.shape, F32)

    row_blocks = [slice(r, r + ROW_SUB) for r in range(0, q_ref.shape[1], ROW_SUB)]

    def scores(sl, s_w, m_w):
        for rows in row_blocks:
            q = q_ref[0, rows, sl]
            sa = lax.dot_general(q, k1_ref[0, :, sl], _NT, preferred_element_type=F32)
            sb = lax.dot_general(q, k2_ref[0, :, sl], _NT, preferred_element_type=F32)
            s_w[rows, 0:n1] = sa
            s_w[rows, n1:] = sb
            m = jnp.maximum(sa.max(axis=-1, keepdims=True), sb.max(axis=-1, keepdims=True))
            m_w[rows, :] = jnp.broadcast_to(m, (ROW_SUB, LANES))

    def values(sl, s_r, m_r, p_scr, va_ref, vb_ref):
        for rows in row_blocks:
            m = m_r[rows, :]
            for t in range((n1 + n2) // LANES):
                tl = slice(t * LANES, (t + 1) * LANES)
                p_scr[rows, tl] = jnp.exp2(s_r[rows, tl] - m).astype(BF16)
        return _dot(p_scr[:, 0:n1], va_ref[0, :, sl]) + _dot(p_scr[:, n1:], vb_ref[0, :, sl])

    @pl.when(n >= 0)
    def _():
        scores(h0, s0, m0)
        acc1 = values(h1, s1, m1, p1, v1l_ref, v2l_ref)
        o_ref[0] = _pair_out(acc0_save[...], acc1)

    @pl.when(n < pl.num_programs(0))
    def _():
        scores(h1, s1, m1)
        acc0_save[...] = values(h0, s0, m0, p0, v1c_ref, v2c_ref)


def _attn_lat_call(q3, k1, k2, v1, v2, *, tq):
    b, s, _ = q3.shape
    s1, s2 = k1.shape[1], k2.shape[1]
    sk = s1 + s2
    npair = N_HEADS // 2
    nq = s // tq
    n_items = b * npair * nq

    def item(n):
        return n // (npair * nq), n % nq, (n // nq) % npair

    def cur(n):
        return item(jnp.minimum(n, n_items - 1))

    def lag(n):
        return item(jnp.maximum(n - 1, 0))

    qw = 2 * HEAD_PAD
    kv_cur = lambda rows: pl.BlockSpec((1, rows, qw), lambda n: (cur(n)[0], 0, cur(n)[2]))
    kv_lag = lambda rows: pl.BlockSpec((1, rows, qw), lambda n: (lag(n)[0], 0, lag(n)[2]))
    in_specs = [pl.BlockSpec((1, tq, qw), lambda n: cur(n)),
                kv_cur(s1), kv_cur(s2), kv_cur(s1), kv_cur(s2), kv_lag(s1), kv_lag(s2)]
    return pl.pallas_call(
        _attn_lat_kernel,
        grid=(n_items + 1,),
        in_specs=in_specs,
        out_specs=pl.BlockSpec((1, tq, LANES), lambda n: lag(n)),
        out_shape=jax.ShapeDtypeStruct((b, s, N_HEADS * V_HEAD), BF16),
        scratch_shapes=[pltpu.VMEM((tq, sk), F32)] * 2 + [pltpu.VMEM((tq, LANES), F32)] * 2
        + [pltpu.VMEM((tq, sk), BF16)] * 2 + [pltpu.VMEM((tq, LANES), F32)],
        compiler_params=_cparams(("arbitrary",)),
        name="l1_attention_lat",
    )(q3, k1, k2, v1, v2, v1, v2)


def _block_diag_halves(w):
    w4 = w.astype(BF16).reshape(2, 4, RNN_BLOCK, RNN_BLOCK)
    eye = jnp.eye(4, dtype=BF16)
    return (w4[:, :, :, None, :] * eye[None, :, None, :, None]).reshape(2, 4 * RNN_BLOCK, 4 * RNN_BLOCK)


def _rope_tables(seq_len):
    rows = seq_len // GRID_W
    row = np.repeat(np.arange(rows, dtype=np.float64), GRID_W)
    col = np.tile(np.arange(GRID_W, dtype=np.float64), rows)
    half = QK_ROPE // 2
    inv = ROPE_BASE ** (-np.arange(0, half, 2, dtype=np.float64) / half)
    ang_r, ang_c = row[:, None] * inv, col[:, None] * inv
    zeros = np.zeros((seq_len, 8))
    cr, sr, cc, sc = np.cos(ang_r), np.sin(ang_r), np.cos(ang_c), np.sin(ang_c)
    lead = np.ones((seq_len, QK_NOPE))
    tail = np.ones((seq_len, HEAD_PAD - QK_NOPE - QK_ROPE))
    cos = np.concatenate([lead, cr, cr, cc, cc, tail], axis=1)
    sa = np.concatenate([0 * lead, zeros, sr, zeros, sc, 0 * tail], axis=1)
    sb = np.concatenate([0 * lead, -sr, zeros, -sc, zeros, 0 * tail], axis=1)
    return tuple(jnp.asarray(t, dtype=F32) for t in (cos, sa, sb))


def kernel(x_prompt, x_sample, state_l0_lru, cache_l1_ckv, cache_l1_kpe, c, c_ctx, l0_w_mod, l0_b_mod, l0_g_mix, l0_g_ffn, l0_w_in, l0_conv_w, l0_conv_b, l0_lru_w_a, l0_lru_b_a, l0_lru_w_i, l0_lru_b_i, l0_lru_lam, l0_pool_w, l0_pool_scale, l0_w_out, l0_ffn_w1, l0_ffn_w2, l1_w_mod, l1_b_mod, l1_g_mix, l1_g_ffn, l1_w_in, l1_g_q, l1_w_qb, l1_g_kv, l1_w_kvb, l1_w_out, l1_ffn_w1, l1_ffn_w2, g_final):
    bp, sp, d = x_prompt.shape
    bs, ss, _ = x_sample.shape
    past = cache_l1_ckv.shape[1]
    xp = x_prompt.reshape(bp * sp, d)
    xs = x_sample.reshape(bs * ss, d)

    cond8 = jnp.concatenate([c, c_ctx[None, :], jnp.zeros((SUBLANES - bs - 1, d), F32)], axis=0)
    m0 = _mod_call(cond8, l0_w_mod, l0_b_mod)
    m1 = _mod_call(cond8, l1_w_mod, l1_b_mod)
    mod_p = [m[bs:bs + 1].reshape(1, 1, 6 * d) for m in (m0, m1)]
    mod_s = [m[0:bs].reshape(bs, 1, 6 * d) for m in (m0, m1)]

    lru = dict(
        conv_w=l0_conv_w, conv_b=l0_conv_b.reshape(1, D_RNN),
        wa=[_block_diag_halves(l0_lru_w_a[i]) for i in range(2)],
        wi=[_block_diag_halves(l0_lru_w_i[i]) for i in range(2)],
        ba=[l0_lru_b_a[i].reshape(1, D_RNN) for i in range(2)],
        bi=[l0_lru_b_i[i].reshape(1, D_RNN) for i in range(2)],
        lam=[l0_lru_lam[i].reshape(1, D_RNN) for i in range(2)],
        pool_w=l0_pool_w.astype(BF16), pool_scale=l0_pool_scale.reshape(1, D_POOL),
        g_mix=l0_g_mix.reshape(1, d), w_in=l0_w_in.astype(BF16))
    wo0 = l0_w_out.astype(BF16)
    wos0 = [wo0[:D_RNN], wo0[D_RNN:]]
    w1_0, w2_0 = l0_ffn_w1.astype(BF16), l0_ffn_w2.astype(BF16)

    def layer0(x2d, mod3, nseq, seq_len, h0f, h0b, ts):
        x3 = x2d.reshape(nseq, seq_len, d)
        hf, stf, ypool, xc, gate = _lru_call(x3, mod3, h0f, None, lru, reverse=False, ts=ts)
        yrnn, stb = _lru_call(None, None, h0b, (hf, xc, gate), lru, reverse=True, ts=ts)
        ys = [yrnn.reshape(-1, D_RNN), ypool.reshape(-1, D_POOL)]
        x2 = _ffn_call(x2d, ys, mod3, wos0, l0_g_ffn, w1_0, w2_0, None, tm=512)
        return x2, stf, stb

    zero_st = jnp.zeros((bp, 1, D_RNN), F32)
    xp, stf, stb = layer0(xp, mod_p[0], bp, sp, zero_st, zero_st, ts=sp)
    xs, _, _ = layer0(xs, mod_s[0], bs, ss, state_l0_lru[:, 0:1], state_l0_lru[:, 1:2], ts=512)
    new_lru = jnp.concatenate([stf, stb], axis=1)

    pad_pe = HEAD_PAD - QK_NOPE - QK_ROPE
    w_in1b = l1_w_in.astype(BF16)
    w_in1 = jnp.concatenate([w_in1b[:, :Q_RANK + KV_RANK], jnp.zeros((d, QK_NOPE), BF16),
                             w_in1b[:, Q_RANK + KV_RANK:], jnp.zeros((d, pad_pe), BF16)], axis=1)
    wqb = l1_w_qb.astype(BF16).reshape(Q_RANK, N_HEADS, QK_NOPE + QK_ROPE)
    wqb = jnp.pad(wqb, ((0, 0), (0, 0), (0, pad_pe))).reshape(Q_RANK, N_HEADS * HEAD_PAD)
    wkvb = l1_w_kvb.astype(BF16).reshape(KV_RANK, N_HEADS, QK_NOPE + V_HEAD)
    w_k = jnp.pad(wkvb[:, :, :QK_NOPE], ((0, 0), (0, 0), (0, HEAD_PAD - QK_NOPE)))
    w_k = w_k.reshape(KV_RANK, N_HEADS * HEAD_PAD)
    wv = wkvb[:, :, QK_NOPE:].reshape(KV_RANK, N_HEADS // 2, 2, V_HEAD)
    zv = jnp.zeros((KV_RANK, N_HEADS // 2, V_HEAD), BF16)
    w_v = jnp.stack([wv[:, :, 0], zv, zv, wv[:, :, 1]], axis=2).reshape(KV_RANK, N_HEADS * HEAD_PAD)
    one = jnp.ones((N_HEADS // 2, V_HEAD), F32)
    v_ones = jnp.stack([0 * one, one, one, 0 * one], axis=1).reshape(1, N_HEADS * HEAD_PAD)
    mla = dict(g_mix=l1_g_mix.reshape(1, d), w_in=w_in1, g_q=l1_g_q.reshape(1, Q_RANK), w_qb=wqb,
               g_kv=l1_g_kv.reshape(1, KV_RANK), w_k=w_k, w_v=w_v, v_ones=v_ones)
    wo1 = [l1_w_out.astype(BF16)]
    w1_1, w2_1 = l1_ffn_w1.astype(BF16), l1_ffn_w2.astype(BF16)

    qp, kp, vp, ckv_new, kpe_new = _mla_proj_call(xp, mod_p[1], mla, None, seq_len=bp * sp, emit_cache=True,
                                                  tm=512)
    op = _attn_ctx_call(qp.reshape(bp, sp, -1), kp.reshape(bp, sp, -1), vp.reshape(bp, sp, -1))
    y_prompt = _ffn_call(xp, [op.reshape(bp * sp, -1)], mod_p[1], wo1, l1_g_ffn, w1_1, w2_1, g_final, tm=512)

    tables = _rope_tables(ss)
    qs, ks, vs = _mla_proj_call(xs, mod_s[1], mla, tables, seq_len=ss, emit_cache=False, tm=512)
    kpe_ctx = jnp.pad(cache_l1_kpe.reshape(bs * past, QK_ROPE), ((0, 0), (QK_NOPE, pad_pe)))
    kc, vc = _ctx_expand_call(cache_l1_ckv.reshape(bs * past, KV_RANK), kpe_ctx, mla, tm=past)
    os_ = _attn_lat_call(qs.reshape(bs, ss, -1), ks.reshape(bs, ss, -1), kc.reshape(bs, past, -1),
                         vs.reshape(bs, ss, -1), vc.reshape(bs, past, -1), tq=512)
    y_sample = _ffn_call(xs, [os_.reshape(bs * ss, -1)], mod_s[1], wo1, l1_g_ffn, w1_1, w2_1, g_final, tm=512)

    new_ckv = ckv_new.reshape(bp, sp, KV_RANK)
    new_kpe = kpe_new[:, QK_NOPE:QK_NOPE + QK_ROPE].reshape(bp, sp, QK_ROPE)
    return (y_prompt.reshape(bp, sp, d), y_sample.reshape(bs, ss, d), new_lru, new_ckv, new_kpe)
```

```python
import functools
import math

import jax
import jax.numpy as jnp
import numpy as np
from jax import lax
from jax.experimental import pallas as pl
from jax.experimental.pallas import tpu as pltpu

F32 = jnp.float32
BF16 = jnp.bfloat16

D_MODEL = 1024
D_FF = 4 * D_MODEL
EPS = 1e-6
D_RNN = 512
RNN_BLOCKS = 8
RNN_BLOCK = 64
CONV_W = 4
LRU_C = 8.0
D_POOL = 512
POOL_WINDOWS = (2, 4, 8, 16)
POOL_GROUP = 128
N_HEADS = 16
QK_NOPE = 64
QK_ROPE = 32
V_HEAD = 64
Q_RANK = 384
KV_RANK = 256
ROPE_BASE = 10000.0
GRID_W = 64

LANES = 128
SUBLANES = 8
HEAD_PAD = 128
HALO = 8
VMEM_LIMIT = 52 * 1024 * 1024


def _cparams(sem):
    return pltpu.CompilerParams(dimension_semantics=sem, vmem_limit_bytes=VMEM_LIMIT)


def _dot(a, b):
    return jnp.dot(a, b, preferred_element_type=F32)


def _sigmoid(x):
    return 0.5 * jnp.tanh(0.5 * x) + 0.5


def _rms_mod(x, g, shift, scale):
    ms = jnp.mean(x * x, axis=-1, keepdims=True)
    y = x * lax.rsqrt(ms + EPS) * g
    return y * (1.0 + scale) + shift


def _mod_kernel(c_ref, w_ref, b_ref, o_ref):
    c = c_ref[...]
    s = c * jax.nn.sigmoid(c)
    o_ref[...] = _dot(s.astype(BF16), w_ref[...].astype(BF16)) + b_ref[...]


def _mod_call(cond8, w_mod, b_mod):
    n = w_mod.shape[1]
    tn = 1024
    return pl.pallas_call(
        _mod_kernel,
        grid=(n // tn,),
        in_specs=[
            pl.BlockSpec((SUBLANES, D_MODEL), lambda j: (0, 0)),
            pl.BlockSpec((D_MODEL, tn), lambda j: (0, j)),
            pl.BlockSpec((1, tn), lambda j: (0, j)),
        ],
        out_specs=pl.BlockSpec((SUBLANES, tn), lambda j: (0, j)),
        out_shape=jax.ShapeDtypeStruct((SUBLANES, n), F32),
        compiler_params=_cparams(("arbitrary",)),
        name="adaln_mod",
    )(cond8, w_mod, b_mod.reshape(1, n))


def _lru_coeffs(xc, wa_ref, ba_ref, wi_ref, bi_ref, lam_ref):
    xcb = xc.astype(BF16)
    half = D_RNN // 2
    ga = jnp.concatenate([_dot(xcb[:, :half], wa_ref[0]), _dot(xcb[:, half:], wa_ref[1])], axis=-1)
    gi = jnp.concatenate([_dot(xcb[:, :half], wi_ref[0]), _dot(xcb[:, half:], wi_ref[1])], axis=-1)
    r = _sigmoid(ga + ba_ref[...])
    i = _sigmoid(gi + bi_ref[...])
    log_a = (LRU_C * r) * jax.nn.log_sigmoid(lam_ref[...])
    a = jnp.exp(log_a)
    v = -jnp.tanh(log_a) * (a * a + 1.0)
    u = jnp.where(v > 0.0, v * lax.rsqrt(v), 0.0) * (i * xc)
    return a, u


def _scan_tile(a, u, carry_row, reverse, a_s, u_s, ts):
    g = ts // SUBLANES
    a3 = a.reshape(g, SUBLANES, D_RNN)
    u3 = u.reshape(g, SUBLANES, D_RNN)
    row = lax.broadcasted_iota(jnp.int32, (g, SUBLANES, D_RNN), 1)
    for d in (1, 2, 4):
        if reverse:
            sh = SUBLANES - d
            m = row < SUBLANES - d
        else:
            sh = d
            m = row >= d
        a_sh = pltpu.roll(a3, sh, axis=1)
        u_sh = pltpu.roll(u3, sh, axis=1)
        u3 = u3 + a3 * jnp.where(m, u_sh, 0.0)
        a3 = a3 * jnp.where(m, a_sh, 1.0)
    a_s[...] = a3.reshape(ts, D_RNN)
    u_s[...] = u3.reshape(ts, D_RNN)
    edge = 0 if reverse else SUBLANES - 1

    def body(k, c):
        gi = (g - 1 - k) if reverse else k
        r0 = pl.multiple_of(gi * SUBLANES, SUBLANES)
        h = u_s[pl.ds(r0, SUBLANES), :] + a_s[pl.ds(r0, SUBLANES), :] * c
        u_s[pl.ds(r0, SUBLANES), :] = h
        return jnp.broadcast_to(h[edge:edge + 1, :], (SUBLANES, D_RNN))

    c0 = jnp.broadcast_to(carry_row, (SUBLANES, D_RNN))
    c_last = lax.fori_loop(0, g, body, c0, unroll=4)
    return c_last[0:1, :]


def _lru_kernel(*refs, reverse, with_pool, ts, nt, seq_len):
    if with_pool:
        (x_ref, xprev_ref, xnext_ref, mod_ref, g_ref, win_ref, h0_ref, cw_ref, cb_ref, wa_ref, ba_ref, wi_ref,
         bi_ref, lam_ref, pw_ref, ps_ref, h_ref, st_ref, yp_ref, xc_ref, gate_out_ref,
         xe_s, ext_s, a_s, u_s, carry_s, b2_s, b4_s, b8_s) = refs
    else:
        (xc_in_ref, gate_ref, hf_ref, h0_ref, wa_ref, ba_ref, wi_ref,
         bi_ref, lam_ref, y_ref, st_ref, a_s, u_s, carry_s) = refs
    j = pl.program_id(1)
    t = (nt - 1 - j) if reverse else j
    n2 = ts + 2 * HALO

    @pl.when(j == 0)
    def _():
        carry_s[...] = h0_ref[0]

    if with_pool:
        xe_s[0:HALO, :] = xprev_ref[0]
        xe_s[HALO:HALO + ts, :] = x_ref[0]
        xe_s[HALO + ts:n2, :] = xnext_ref[0]
        shift = mod_ref[0, :, 0:D_MODEL]
        scale = mod_ref[0, :, D_MODEL:2 * D_MODEL]
        hx = _rms_mod(xe_s[...], g_ref[...], shift, scale).astype(BF16)
        z = _dot(hx, win_ref[...])
        row = lax.broadcasted_iota(jnp.int32, (n2, D_RNN), 0)
        inside = ((row >= HALO) | (t > 0)) & ((row < HALO + ts) | (t < nt - 1))
        gate_out_ref[0] = z[HALO:HALO + ts, D_RNN:2 * D_RNN]
        ext_s[...] = jnp.where(inside, z[:, 0:D_RNN], 0.0)

        xc = cb_ref[...]
        for k in range(CONV_W):
            off = HALO - CONV_W // 2 + k
            xc = xc + ext_s[off:off + ts, :] * cw_ref[k:k + 1, :]
        xc_ref[0] = xc
        ext_s[...] = jnp.where(inside, z[:, 2 * D_RNN:], 0.0)
    else:
        xc = xc_in_ref[0]

    a, u = _lru_coeffs(xc, wa_ref, ba_ref, wi_ref, bi_ref, lam_ref)
    c_last = _scan_tile(a, u, carry_s[...], reverse, a_s, u_s, ts)
    carry_s[...] = c_last
    st_ref[0] = c_last
    h = u_s[...]

    if not with_pool:
        y_ref[0] = ((hf_ref[0] + h) * jax.nn.gelu(gate_ref[0])).astype(BF16)
        return

    h_ref[0] = h

    b2_s[1:n2, :] = ext_s[1:n2, :] + ext_s[0:n2 - 1, :]
    b4_s[2:n2 - 1, :] = b2_s[3:n2, LANES:] + b2_s[1:n2 - 2, LANES:]
    b8_s[4:n2 - 3, :] = b4_s[6:n2 - 1, LANES:] + b4_s[2:n2 - 5, LANES:]
    s16 = b8_s[HALO + 4:HALO + 4 + ts, LANES:] + b8_s[HALO - 4:HALO - 4 + ts, LANES:]
    sums = (b2_s[HALO:HALO + ts, 0:LANES], b4_s[HALO:HALO + ts, 0:LANES], b8_s[HALO:HALO + ts, 0:LANES], s16)
    pos = t * ts + lax.broadcasted_iota(jnp.int32, (ts, LANES), 0)
    for gidx, w in enumerate(POOL_WINDOWS):
        left = w // 2
        right = w - 1 - left
        lo = jnp.maximum(pos - left, 0)
        hi = jnp.minimum(pos + right, seq_len - 1) + 1
        cnt = (hi - lo).astype(F32)
        xg = ext_s[HALO:HALO + ts, gidx * LANES:(gidx + 1) * LANES]
        dlt = sums[gidx] / cnt - xg
        yg = _dot(dlt.astype(BF16), pw_ref[gidx]) * ps_ref[:, gidx * LANES:(gidx + 1) * LANES]
        yp_ref[0, :, gidx * LANES:(gidx + 1) * LANES] = yg.astype(BF16)


def _lru_call(x3, mod3, h0, fwd, prm, *, reverse, ts):
    nseq, seq_len = (x3 if not reverse else fwd[0]).shape[:2]
    nt = seq_len // ts
    tb = ts // HALO
    nhb = seq_len // HALO
    d = 1 if reverse else 0
    with_pool = not reverse

    def tpos(j):
        return (nt - 1 - j) if reverse else j

    def const(shape):
        nd = len(shape)
        return pl.BlockSpec(shape, lambda s, j: (0,) * nd)

    tile_out = pl.BlockSpec((1, ts, D_RNN), lambda s, j: (s, tpos(j), 0))
    st_spec = pl.BlockSpec((1, 1, D_RNN), lambda s, j: (s, 0, 0))
    conv_specs = [const((CONV_W, D_RNN)), const((1, D_RNN))]
    conv_args = [prm["conv_w"], prm["conv_b"]]
    w_specs = [const((2, 256, 256)), const((1, D_RNN)), const((2, 256, 256)), const((1, D_RNN)), const((1, D_RNN))]
    w_args = [prm["wa"][d], prm["ba"][d], prm["wi"][d], prm["bi"][d], prm["lam"][d]]
    scratch = [pltpu.VMEM((ts, D_RNN), F32), pltpu.VMEM((ts, D_RNN), F32), pltpu.VMEM((1, D_RNN), F32)]
    kern = functools.partial(_lru_kernel, reverse=reverse, with_pool=with_pool, ts=ts, nt=nt, seq_len=seq_len)
    if with_pool:
        nb = mod3.shape[0]
        x_specs = [
            pl.BlockSpec((1, ts, D_MODEL), lambda s, j: (s, j, 0)),
            pl.BlockSpec((1, HALO, D_MODEL), lambda s, j: (s, jnp.maximum(j * tb - 1, 0), 0)),
            pl.BlockSpec((1, HALO, D_MODEL), lambda s, j: (s, jnp.minimum((j + 1) * tb, nhb - 1), 0)),
            pl.BlockSpec((1, 1, 6 * D_MODEL), lambda s, j: ((s * nb) // nseq, 0, 0)),
            const((1, D_MODEL)), const(prm["w_in"].shape)]
        in_specs = x_specs + [st_spec] + conv_specs + w_specs + [
            const((4, POOL_GROUP, POOL_GROUP)), const((1, D_POOL))]
        args = [x3, x3, x3, mod3, prm["g_mix"], prm["w_in"], h0] + conv_args + w_args + [
            prm["pool_w"], prm["pool_scale"]]
        out_specs = [tile_out, st_spec, tile_out, tile_out, tile_out]
        out_shape = [jax.ShapeDtypeStruct((nseq, seq_len, D_RNN), F32),
                     jax.ShapeDtypeStruct((nseq, 1, D_RNN), F32),
                     jax.ShapeDtypeStruct((nseq, seq_len, D_POOL), BF16),
                     jax.ShapeDtypeStruct((nseq, seq_len, D_RNN), F32),
                     jax.ShapeDtypeStruct((nseq, seq_len, D_RNN), F32)]
        n2 = ts + 2 * HALO
        scratch = [pltpu.VMEM((n2, D_MODEL), F32), pltpu.VMEM((n2, D_RNN), F32)] + scratch + [
            pltpu.VMEM((n2, D_POOL), F32), pltpu.VMEM((n2, D_POOL - LANES), F32),
            pltpu.VMEM((n2, D_POOL - 2 * LANES), F32)]
        name = "l0_in_lru_fwd_pool"
    else:
        hf, xc, gate = fwd
        in_specs = [tile_out, tile_out, tile_out, st_spec] + w_specs
        args = [xc, gate, hf, h0] + w_args
        out_specs = [tile_out, st_spec]
        out_shape = [jax.ShapeDtypeStruct((nseq, seq_len, D_RNN), BF16),
                     jax.ShapeDtypeStruct((nseq, 1, D_RNN), F32)]
        name = "l0_lru_bwd"
    return pl.pallas_call(
        kern,
        grid=(nseq, nt),
        in_specs=in_specs,
        out_specs=out_specs,
        out_shape=out_shape,
        scratch_shapes=scratch,
        compiler_params=_cparams(("arbitrary", "arbitrary")),
        name=name,
    )(*args)


FF_CHUNK = 1024
FFN_ROWS = 1024


def _residual_ffn(x, mix, mod_ref, g_ref, w1_ref, w2_ref):
    def mod(k):
        return mod_ref[0, :, k * D_MODEL:(k + 1) * D_MODEL]

    x1 = x + mod(2) * mix
    hn = _rms_mod(x1, g_ref[...], mod(3), mod(4)).astype(BF16)
    acc = None
    for c in range(D_FF // FF_CHUNK):
        hc = _dot(hn, w1_ref[:, c * FF_CHUNK:(c + 1) * FF_CHUNK])
        hc = jnp.square(jnp.maximum(hc, 0.0)).astype(BF16)
        part = _dot(hc, w2_ref[c * FF_CHUNK:(c + 1) * FF_CHUNK, :])
        acc = part if acc is None else acc + part
    return x1 + mod(5) * acc


def _ffn_kernel(*refs, n_y, final_norm):
    x_ref = refs[0]
    y_refs = refs[1:1 + n_y]
    mod_ref = refs[1 + n_y]
    wo_refs = refs[2 + n_y:2 + 2 * n_y]
    g_ref, w1_ref, w2_ref = refs[2 + 2 * n_y:5 + 2 * n_y]
    rest = refs[5 + 2 * n_y:]
    if final_norm:
        gf_ref, o_ref = rest
    else:
        (o_ref,) = rest

    mix = _dot(y_refs[0][...], wo_refs[0][...])
    for k in range(1, n_y):
        mix = mix + _dot(y_refs[k][...], wo_refs[k][...])
    x2 = _residual_ffn(x_ref[...], mix, mod_ref, g_ref, w1_ref, w2_ref)
    if final_norm:
        ms = jnp.mean(x2 * x2, axis=-1, keepdims=True)
        x2 = x2 * lax.rsqrt(ms + EPS) * gf_ref[...]
    o_ref[...] = x2


def _ffn_call(x2d, ys, mod3, wos, g, w1, w2, g_final, tm):
    t = x2d.shape[0]
    nb = mod3.shape[0]
    rows_per_b = t // nb
    n_y = len(ys)
    final_norm = g_final is not None

    def const(shape):
        return pl.BlockSpec(shape, lambda i: (0, 0), pipeline_mode=pl.Buffered(1))

    in_specs = [pl.BlockSpec((tm, D_MODEL), lambda i: (i, 0))]
    in_specs += [pl.BlockSpec((tm, y.shape[1]), lambda i: (i, 0)) for y in ys]
    in_specs += [pl.BlockSpec((1, 1, 6 * D_MODEL), lambda i: ((i * tm) // rows_per_b, 0, 0))]
    in_specs += [const(w.shape) for w in wos]
    in_specs += [const((1, D_MODEL)), const((D_MODEL, D_FF)), const((D_FF, D_MODEL))]
    args = [x2d, *ys, mod3, *wos, g.reshape(1, D_MODEL), w1, w2]
    if final_norm:
        in_specs.append(const((1, D_MODEL)))
        args.append(g_final.reshape(1, D_MODEL))
    return pl.pallas_call(
        functools.partial(_ffn_kernel, n_y=n_y, final_norm=final_norm),
        grid=(t // tm,),
        in_specs=in_specs,
        out_specs=pl.BlockSpec((tm, D_MODEL), lambda i: (i, 0)),
        out_shape=jax.ShapeDtypeStruct((t, D_MODEL), F32),
        compiler_params=_cparams(("arbitrary",)),
        name="mix_out_ffn",
    )(*args)


def _rope(x, cos, sa, sb):
    return x * cos + pltpu.roll(x, 8, axis=1) * sa + pltpu.roll(x, LANES - 8, axis=1) * sb


def _mla_proj_kernel(*refs, rope, emit_cache):
    x_ref, mod_ref, g_ref, win_ref, gq_ref, wqb_ref, gkv_ref, wk_ref, wv_ref, vones_ref = refs[:10]
    rest = refs[10:]
    if rope:
        cos_ref, sa_ref, sb_ref = rest[:3]
        rest = rest[3:]
    q_ref, k_ref, v_ref = rest[:3]
    if emit_cache:
        ckv_ref, kpe_ref = rest[3:]

    shift = mod_ref[0, :, 0:D_MODEL]
    scale = mod_ref[0, :, D_MODEL:2 * D_MODEL]
    h = _rms_mod(x_ref[...], g_ref[...], shift, scale).astype(BF16)
    z = _dot(h, win_ref[...])
    cq = z[:, :Q_RANK]
    ckv = z[:, Q_RANK:Q_RANK + KV_RANK]
    kpe = z[:, Q_RANK + KV_RANK:]
    cqn = cq * lax.rsqrt(jnp.mean(cq * cq, axis=-1, keepdims=True) + EPS) * gq_ref[...]
    ckvn = ckv * lax.rsqrt(jnp.mean(ckv * ckv, axis=-1, keepdims=True) + EPS) * gkv_ref[...]
    if emit_cache:
        ckv_ref[...] = ckvn
        kpe_ref[...] = kpe
    q = _dot(cqn.astype(BF16), wqb_ref[...])
    ckvb = ckvn.astype(BF16)
    kn = _dot(ckvb, wk_ref[...])
    v_ref[...] = (_dot(ckvb, wv_ref[...]) + vones_ref[...]).astype(BF16)
    if rope:
        cos, sa, sb = cos_ref[...], sa_ref[...], sb_ref[...]
        kpe = _rope(kpe, cos, sa, sb)
    for hd in range(N_HEADS):
        sl = slice(hd * HEAD_PAD, (hd + 1) * HEAD_PAD)
        qh = q[:, sl]
        if rope:
            qh = _rope(qh, cos, sa, sb)
        q_ref[:, sl] = qh.astype(BF16)
        k_ref[:, sl] = (kn[:, sl] + kpe).astype(BF16)


def _mla_proj_call(x2d, mod3, prm, tables, *, seq_len, emit_cache, tm):
    t = x2d.shape[0]
    nb = mod3.shape[0]
    rows_per_b = t // nb
    rope = tables is not None
    tiles_per_seq = seq_len // tm
    hw = N_HEADS * HEAD_PAD

    def const(shape):
        return pl.BlockSpec(shape, lambda i: (0, 0))

    def rows(n):
        return pl.BlockSpec((tm, n), lambda i: (i, 0))

    in_specs = [rows(D_MODEL),
                pl.BlockSpec((1, 1, 6 * D_MODEL), lambda i: ((i * tm) // rows_per_b, 0, 0)),
                const((1, D_MODEL)), const(prm["w_in"].shape), const((1, Q_RANK)), const(prm["w_qb"].shape),
                const((1, KV_RANK)), const(prm["w_k"].shape), const(prm["w_v"].shape), const((1, hw))]
    args = [x2d, mod3, prm["g_mix"], prm["w_in"], prm["g_q"], prm["w_qb"], prm["g_kv"], prm["w_k"], prm["w_v"],
            prm["v_ones"]]
    if rope:
        in_specs += [pl.BlockSpec((tm, LANES), lambda i: (i % tiles_per_seq, 0))] * 3
        args += list(tables)
    out_specs = [rows(hw), rows(hw), rows(hw)]
    out_shape = [jax.ShapeDtypeStruct((t, hw), BF16)] * 3
    if emit_cache:
        out_specs += [rows(KV_RANK), rows(LANES)]
        out_shape += [jax.ShapeDtypeStruct((t, KV_RANK), F32), jax.ShapeDtypeStruct((t, LANES), F32)]
    return pl.pallas_call(
        functools.partial(_mla_proj_kernel, rope=rope, emit_cache=emit_cache),
        grid=(t // tm,),
        in_specs=in_specs,
        out_specs=out_specs,
        out_shape=out_shape,
        compiler_params=_cparams(("arbitrary",)),
        name="l1_mla_proj",
    )(*args)


def _ctx_expand_kernel(ckv_ref, kpe_ref, wk_ref, wv_ref, vones_ref, k_ref, v_ref):
    ckvb = ckv_ref[...].astype(BF16)
    kn = _dot(ckvb, wk_ref[...])
    v_ref[...] = (_dot(ckvb, wv_ref[...]) + vones_ref[...]).astype(BF16)
    kpe = kpe_ref[...]
    for hd in range(N_HEADS):
        sl = slice(hd * HEAD_PAD, (hd + 1) * HEAD_PAD)
        k_ref[:, sl] = (kn[:, sl] + kpe).astype(BF16)


def _ctx_expand_call(ckv2d, kpe2d, prm, tm):
    hw = N_HEADS * HEAD_PAD
    t = ckv2d.shape[0]
    const = lambda a: pl.BlockSpec(a.shape, lambda i: (0, 0))
    rows = lambda n: pl.BlockSpec((tm, n), lambda i: (i, 0))
    return pl.pallas_call(
        _ctx_expand_kernel,
        grid=(t // tm,),
        in_specs=[rows(KV_RANK), rows(LANES), const(prm["w_k"]), const(prm["w_v"]), const(prm["v_ones"])],
        out_specs=[rows(hw), rows(hw)],
        out_shape=[jax.ShapeDtypeStruct((t, hw), BF16), jax.ShapeDtypeStruct((t, hw), BF16)],
        compiler_params=_cparams(("arbitrary",)),
        name="l1_ctx_expand",
    )(ckv2d, kpe2d, prm["w_k"], prm["w_v"], prm["v_ones"])


Q_PRESCALE = (QK_NOPE + QK_ROPE) ** -0.5 * math.log2(math.e)
_NT = (((1,), (1,)), ((), ()))
ROW_SUB = 256


def _pair_out(acc_even, acc_odd):
    lane = lax.broadcasted_iota(jnp.int32, acc_even.shape, 1)
    first = lane < V_HEAD
    num = jnp.where(first, acc_even, acc_odd)
    den = jnp.where(first, pltpu.roll(acc_even, V_HEAD, axis=1), pltpu.roll(acc_odd, V_HEAD, axis=1))
    return (num / den).astype(BF16)


def _attn_ctx_kernel(q_ref, k_ref, v_ref, o_ref, s_scr, p_scr, *, pairs):
    heads = [slice(h * HEAD_PAD, (h + 1) * HEAD_PAD) for h in range(2 * pairs)]
    for h, sl in enumerate(heads):
        s_scr[h] = lax.dot_general(q_ref[0, :, sl], k_ref[0, :, sl], _NT, preferred_element_type=F32)
    for h in range(2 * pairs):
        s = s_scr[h]
        p_scr[h] = jnp.exp2(s - s.max(axis=-1, keepdims=True)).astype(BF16)
    for p in range(pairs):
        accs = [_dot(p_scr[2 * p + e], v_ref[0, :, heads[2 * p + e]]) for e in range(2)]
        o_ref[0, :, p * LANES:(p + 1) * LANES] = _pair_out(accs[0], accs[1])


def _attn_ctx_call(q3, k3, v3):
    b, s, _ = q3.shape
    spec = lambda w: pl.BlockSpec((1, s, w), lambda bi: (bi, 0, 0))
    hw = N_HEADS * HEAD_PAD
    return pl.pallas_call(
        functools.partial(_attn_ctx_kernel, pairs=N_HEADS // 2),
        grid=(b,),
        in_specs=[spec(hw), spec(hw), spec(hw)],
        out_specs=spec(N_HEADS * V_HEAD),
        out_shape=jax.ShapeDtypeStruct((b, s, N_HEADS * V_HEAD), BF16),
        scratch_shapes=[pltpu.VMEM((N_HEADS, s, s), F32), pltpu.VMEM((N_HEADS, s, s), BF16)],
        compiler_params=_cparams(("arbitrary",)),
        name="l1_attention_ctx",
    )(q3, k3, v3)


def _attn_lat_kernel(q_ref, k1_ref, k2_ref, v1c_ref, v2c_ref, v1l_ref, v2l_ref, o_ref,
                     s0, s1, m0, m1, p0, p1, acc0_save):
    n = pl.program_id(0)
    n1, n2 = k1_ref.shape[1], k2_ref.shape[1]
    h0 = slice(0, HEAD_PAD)
    h1 = slice(HEAD_PAD, 2 * HEAD_PAD)

    @pl.when(n == 0)
    def _():
        s1[...] = jnp.zeros(s1.shape, F32)
        m1[...] = jnp.zeros(m1.shape, F32)
        acc0_save[...] = jnp.ones(acc0_save.shape, F32)

    row_blocks = [slice(r, r + ROW_SUB) for r in range(0, q_ref.shape[1], ROW_SUB)]

    def scores(sl, s_w, m_w):
        for rows in row_blocks:
            q = q_ref[0, rows, sl]
            sa = lax.dot_general(q, k1_ref[0, :, sl], _NT, preferred_element_type=F32)
            sb = lax.dot_general(q, k2_ref[0, :, sl], _NT, preferred_element_type=F32)
            s_w[rows, 0:n1] = sa
            s_w[rows, n1:] = sb
            m = jnp.maximum(sa.max(axis=-1, keepdims=True), sb.max(axis=-1, keepdims=True))
            m_w[rows, :] = jnp.broadcast_to(m, (ROW_SUB, LANES))

    def values(sl, s_r, m_r, p_scr, va_ref, vb_ref):
        for rows in row_blocks:
            m = m_r[rows, :]
            for t in range((n1 + n2) // LANES):
                tl = slice(t * LANES, (t + 1) * LANES)
                p_scr[rows, tl] = jnp.exp2(s_r[rows, tl] - m).astype(BF16)
        return _dot(p_scr[:, 0:n1], va_ref[0, :, sl]) + _dot(p_scr[:, n1:], vb_ref[0, :, sl])

    @pl.when(n >= 0)
    def _():
        scores(h0, s0, m0)
        acc1 = values(h1, s1, m1, p1, v1l_ref, v2l_ref)
        o_ref[0] = _pair_out(acc0_save[...], acc1)

    @pl.when(n < pl.num_programs(0))
    def _():
        scores(h1, s1, m1)
        acc0_save[...] = values(h0, s0, m0, p0, v1c_ref, v2c_ref)


def _attn_lat_call(q3, k1, k2, v1, v2, *, tq):
    b, s, _ = q3.shape
    s1, s2 = k1.shape[1], k2.shape[1]
    sk = s1 + s2
    npair = N_HEADS // 2
    nq = s // tq
    n_items = b * npair * nq

    def item(n):
        return n // (npair * nq), n % nq, (n // nq) % npair

    def cur(n):
        return item(jnp.minimum(n, n_items - 1))

    def lag(n):
        return item(jnp.maximum(n - 1, 0))

    qw = 2 * HEAD_PAD
    kv_cur = lambda rows: pl.BlockSpec((1, rows, qw), lambda n: (cur(n)[0], 0, cur(n)[2]))
    kv_lag = lambda rows: pl.BlockSpec((1, rows, qw), lambda n: (lag(n)[0], 0, lag(n)[2]))
    in_specs = [pl.BlockSpec((1, tq, qw), lambda n: cur(n)),
                kv_cur(s1), kv_cur(s2), kv_cur(s1), kv_cur(s2), kv_lag(s1), kv_lag(s2)]
    return pl.pallas_call(
        _attn_lat_kernel,
        grid=(n_items + 1,),
        in_specs=in_specs,
        out_specs=pl.BlockSpec((1, tq, LANES), lambda n: lag(n)),
        out_shape=jax.ShapeDtypeStruct((b, s, N_HEADS * V_HEAD), BF16),
        scratch_shapes=[pltpu.VMEM((tq, sk), F32)] * 2 + [pltpu.VMEM((tq, LANES), F32)] * 2
        + [pltpu.VMEM((tq, sk), BF16)] * 2 + [pltpu.VMEM((tq, LANES), F32)],
        compiler_params=_cparams(("arbitrary",)),
        name="l1_attention_lat",
    )(q3, k1, k2, v1, v2, v1, v2)


def _block_diag_halves(w):
    w4 = w.astype(BF16).reshape(2, 4, RNN_BLOCK, RNN_BLOCK)
    eye = jnp.eye(4, dtype=BF16)
    return (w4[:, :, :, None, :] * eye[None, :, None, :, None]).reshape(2, 4 * RNN_BLOCK, 4 * RNN_BLOCK)


def _rope_tables(seq_len):
    rows = seq_len // GRID_W
    row = np.repeat(np.arange(rows, dtype=np.float64), GRID_W)
    col = np.tile(np.arange(GRID_W, dtype=np.float64), rows)
    half = QK_ROPE // 2
    inv = ROPE_BASE ** (-np.arange(0, half, 2, dtype=np.float64) / half)
    ang_r, ang_c = row[:, None] * inv, col[:, None] * inv
    zeros = np.zeros((seq_len, 8))
    cr, sr, cc, sc = np.cos(ang_r), np.sin(ang_r), np.cos(ang_c), np.sin(ang_c)
    lead = np.ones((seq_len, QK_NOPE))
    tail = np.ones((seq_len, HEAD_PAD - QK_NOPE - QK_ROPE))
    cos = np.concatenate([lead, cr, cr, cc, cc, tail], axis=1)
    sa = np.concatenate([0 * lead, zeros, sr, zeros, sc, 0 * tail], axis=1)
    sb = np.concatenate([0 * lead, -sr, zeros, -sc, zeros, 0 * tail], axis=1)
    return tuple(jnp.asarray(t, dtype=F32) for t in (cos, sa, sb))


def kernel(x_prompt, x_sample, state_l0_lru, cache_l1_ckv, cache_l1_kpe, c, c_ctx, l0_w_mod, l0_b_mod, l0_g_mix, l0_g_ffn, l0_w_in, l0_conv_w, l0_conv_b, l0_lru_w_a, l0_lru_b_a, l0_lru_w_i, l0_lru_b_i, l0_lru_lam, l0_pool_w, l0_pool_scale, l0_w_out, l0_ffn_w1, l0_ffn_w2, l1_w_mod, l1_b_mod, l1_g_mix, l1_g_ffn, l1_w_in, l1_g_q, l1_w_qb, l1_g_kv, l1_w_kvb, l1_w_out, l1_ffn_w1, l1_ffn_w2, g_final):
    bp, sp, d = x_prompt.shape
    bs, ss, _ = x_sample.shape
    past = cache_l1_ckv.shape[1]
    xp = x_prompt.reshape(bp * sp, d)
    xs = x_sample.reshape(bs * ss, d)

    cond8 = jnp.concatenate([c, c_ctx[None, :], jnp.zeros((SUBLANES - bs - 1, d), F32)], axis=0)
    m0 = _mod_call(cond8, l0_w_mod, l0_b_mod)
    m1 = _mod_call(cond8, l1_w_mod, l1_b_mod)
    mod_p = [m[bs:bs + 1].reshape(1, 1, 6 * d) for m in (m0, m1)]
    mod_s = [m[0:bs].reshape(bs, 1, 6 * d) for m in (m0, m1)]

    lru = dict(
        conv_w=l0_conv_w, conv_b=l0_conv_b.reshape(1, D_RNN),
        wa=[_block_diag_halves(l0_lru_w_a[i]) for i in range(2)],
        wi=[_block_diag_halves(l0_lru_w_i[i]) for i in range(2)],
        ba=[l0_lru_b_a[i].reshape(1, D_RNN) for i in range(2)],
        bi=[l0_lru_b_i[i].reshape(1, D_RNN) for i in range(2)],
        lam=[l0_lru_lam[i].reshape(1, D_RNN) for i in range(2)],
        pool_w=l0_pool_w.astype(BF16), pool_scale=l0_pool_scale.reshape(1, D_POOL),
        g_mix=l0_g_mix.reshape(1, d), w_in=l0_w_in.astype(BF16))
    wo0 = l0_w_out.astype(BF16)
    wos0 = [wo0[:D_RNN], wo0[D_RNN:]]
    w1_0, w2_0 = l0_ffn_w1.astype(BF16), l0_ffn_w2.astype(BF16)

    def layer0(x2d, mod3, nseq, seq_len, h0f, h0b, ts):
        x3 = x2d.reshape(nseq, seq_len, d)
        hf, stf, ypool, xc, gate = _lru_call(x3, mod3, h0f, None, lru, reverse=False, ts=ts)
        yrnn, stb = _lru_call(None, None, h0b, (hf, xc, gate), lru, reverse=True, ts=ts)
        ys = [yrnn.reshape(-1, D_RNN), ypool.reshape(-1, D_POOL)]
        x2 = _ffn_call(x2d, ys, mod3, wos0, l0_g_ffn, w1_0, w2_0, None, tm=FFN_ROWS)
        return x2, stf, stb

    zero_st = jnp.zeros((bp, 1, D_RNN), F32)
    xp, stf, stb = layer0(xp, mod_p[0], bp, sp, zero_st, zero_st, ts=sp)
    xs, _, _ = layer0(xs, mod_s[0], bs, ss, state_l0_lru[:, 0:1], state_l0_lru[:, 1:2], ts=512)
    new_lru = jnp.concatenate([stf, stb], axis=1)

    pad_pe = HEAD_PAD - QK_NOPE - QK_ROPE
    w_in1b = l1_w_in.astype(BF16)
    w_in1 = jnp.concatenate([w_in1b[:, :Q_RANK + KV_RANK], jnp.zeros((d, QK_NOPE), BF16),
                             w_in1b[:, Q_RANK + KV_RANK:], jnp.zeros((d, pad_pe), BF16)], axis=1)
    wqb = l1_w_qb.astype(BF16).reshape(Q_RANK, N_HEADS, QK_NOPE + QK_ROPE)
    wqb = jnp.pad(wqb, ((0, 0), (0, 0), (0, pad_pe))).reshape(Q_RANK, N_HEADS * HEAD_PAD)
    wkvb = l1_w_kvb.astype(BF16).reshape(KV_RANK, N_HEADS, QK_NOPE + V_HEAD)
    w_k = jnp.pad(wkvb[:, :, :QK_NOPE], ((0, 0), (0, 0), (0, HEAD_PAD - QK_NOPE)))
    w_k = w_k.reshape(KV_RANK, N_HEADS * HEAD_PAD)
    wv = wkvb[:, :, QK_NOPE:].reshape(KV_RANK, N_HEADS // 2, 2, V_HEAD)
    zv = jnp.zeros((KV_RANK, N_HEADS // 2, V_HEAD), BF16)
    w_v = jnp.stack([wv[:, :, 0], zv, zv, wv[:, :, 1]], axis=2).reshape(KV_RANK, N_HEADS * HEAD_PAD)
    one = jnp.ones((N_HEADS // 2, V_HEAD), F32)
    v_ones = jnp.stack([0 * one, one, one, 0 * one], axis=1).reshape(1, N_HEADS * HEAD_PAD)
    mla = dict(g_mix=l1_g_mix.reshape(1, d), w_in=w_in1, g_q=l1_g_q.reshape(1, Q_RANK) * Q_PRESCALE, w_qb=wqb,
               g_kv=l1_g_kv.reshape(1, KV_RANK), w_k=w_k, w_v=w_v, v_ones=v_ones)
    wo1 = [l1_w_out.astype(BF16)]
    w1_1, w2_1 = l1_ffn_w1.astype(BF16), l1_ffn_w2.astype(BF16)

    qp, kp, vp, ckv_new, kpe_new = _mla_proj_call(xp, mod_p[1], mla, None, seq_len=bp * sp, emit_cache=True,
                                                  tm=512)
    op = _attn_ctx_call(qp.reshape(bp, sp, -1), kp.reshape(bp, sp, -1), vp.reshape(bp, sp, -1))
    y_prompt = _ffn_call(xp, [op.reshape(bp * sp, -1)], mod_p[1], wo1, l1_g_ffn, w1_1, w2_1, g_final, tm=FFN_ROWS)

    tables = _rope_tables(ss)
    qs, ks, vs = _mla_proj_call(xs, mod_s[1], mla, tables, seq_len=ss, emit_cache=False, tm=512)
    kpe_ctx = jnp.pad(cache_l1_kpe.reshape(bs * past, QK_ROPE), ((0, 0), (QK_NOPE, pad_pe)))
    kc, vc = _ctx_expand_call(cache_l1_ckv.reshape(bs * past, KV_RANK), kpe_ctx, mla, tm=past)
    os_ = _attn_lat_call(qs.reshape(bs, ss, -1), ks.reshape(bs, ss, -1), kc.reshape(bs, past, -1),
                         vs.reshape(bs, ss, -1), vc.reshape(bs, past, -1), tq=512)
    y_sample = _ffn_call(xs, [os_.reshape(bs * ss, -1)], mod_s[1], wo1, l1_g_ffn, w1_1, w2_1, g_final, tm=FFN_ROWS)

    new_ckv = ckv_new.reshape(bp, sp, KV_RANK)
    new_kpe = kpe_new[:, QK_NOPE:QK_NOPE + QK_ROPE].reshape(bp, sp, QK_ROPE)
    return (y_prompt.reshape(bp, sp, d), y_sample.reshape(bs, ss, d), new_lru, new_ckv, new_kpe)
```

```python
import functools
import math

import jax
import jax.numpy as jnp
import numpy as np
from jax import lax
from jax.experimental import pallas as pl
from jax.experimental.pallas import tpu as pltpu

F32 = jnp.float32
BF16 = jnp.bfloat16

D_MODEL = 1024
D_FF = 4 * D_MODEL
EPS = 1e-6
D_RNN = 512
RNN_BLOCKS = 8
RNN_BLOCK = 64
CONV_W = 4
LRU_C = 8.0
D_POOL = 512
POOL_WINDOWS = (2, 4, 8, 16)
POOL_GROUP = 128
N_HEADS = 16
QK_NOPE = 64
QK_ROPE = 32
V_HEAD = 64
Q_RANK = 384
KV_RANK = 256
ROPE_BASE = 10000.0
GRID_W = 64

LANES = 128
SUBLANES = 8
HEAD_PAD = 128
PE_WIDTH = 48
HALO = 8
VMEM_LIMIT = 52 * 1024 * 1024


def _cparams(sem):
    return pltpu.CompilerParams(dimension_semantics=sem, vmem_limit_bytes=VMEM_LIMIT)


def _dot(a, b):
    return jnp.dot(a, b, preferred_element_type=F32)


def _sigmoid(x):
    return 0.5 * jnp.tanh(0.5 * x) + 0.5


def _rms_mod(x, g, shift, scale):
    ms = jnp.mean(x * x, axis=-1, keepdims=True)
    y = x * lax.rsqrt(ms + EPS) * g
    return y * (1.0 + scale) + shift


def _mod_kernel(c_ref, w_ref, b_ref, o_ref):
    c = c_ref[...]
    s = c * jax.nn.sigmoid(c)
    o_ref[...] = _dot(s.astype(BF16), w_ref[...].astype(BF16)) + b_ref[...]


def _mod_call(cond8, w_mod, b_mod):
    n = w_mod.shape[1]
    tn = 1024
    return pl.pallas_call(
        _mod_kernel,
        grid=(n // tn,),
        in_specs=[
            pl.BlockSpec((SUBLANES, D_MODEL), lambda j: (0, 0)),
            pl.BlockSpec((D_MODEL, tn), lambda j: (0, j)),
            pl.BlockSpec((1, tn), lambda j: (0, j)),
        ],
        out_specs=pl.BlockSpec((SUBLANES, tn), lambda j: (0, j)),
        out_shape=jax.ShapeDtypeStruct((SUBLANES, n), F32),
        compiler_params=_cparams(("arbitrary",)),
        name="adaln_mod",
    )(cond8, w_mod, b_mod.reshape(1, n))


def _lru_coeffs(xc, wa_ref, ba_ref, wi_ref, bi_ref, lam_ref):
    xcb = xc.astype(BF16)
    half = D_RNN // 2
    ga = jnp.concatenate([_dot(xcb[:, :half], wa_ref[0]), _dot(xcb[:, half:], wa_ref[1])], axis=-1)
    gi = jnp.concatenate([_dot(xcb[:, :half], wi_ref[0]), _dot(xcb[:, half:], wi_ref[1])], axis=-1)
    r = _sigmoid(ga + ba_ref[...])
    i = _sigmoid(gi + bi_ref[...])
    log_a = (LRU_C * r) * jax.nn.log_sigmoid(lam_ref[...])
    a = jnp.exp(log_a)
    v = -jnp.tanh(log_a) * (a * a + 1.0)
    u = jnp.where(v > 0.0, v * lax.rsqrt(v), 0.0) * (i * xc)
    return a, u


def _scan_tile(a, u, carry_row, reverse, a_s, u_s, ts):
    g = ts // SUBLANES
    a3 = a.reshape(g, SUBLANES, D_RNN)
    u3 = u.reshape(g, SUBLANES, D_RNN)
    row = lax.broadcasted_iota(jnp.int32, (g, SUBLANES, D_RNN), 1)
    for d in (1, 2, 4):
        if reverse:
            sh = SUBLANES - d
            m = row < SUBLANES - d
        else:
            sh = d
            m = row >= d
        a_sh = pltpu.roll(a3, sh, axis=1)
        u_sh = pltpu.roll(u3, sh, axis=1)
        u3 = u3 + a3 * jnp.where(m, u_sh, 0.0)
        a3 = a3 * jnp.where(m, a_sh, 1.0)
    a_s[...] = a3.reshape(ts, D_RNN)
    u_s[...] = u3.reshape(ts, D_RNN)
    edge = 0 if reverse else SUBLANES - 1

    def body(k, c):
        gi = (g - 1 - k) if reverse else k
        r0 = pl.multiple_of(gi * SUBLANES, SUBLANES)
        h = u_s[pl.ds(r0, SUBLANES), :] + a_s[pl.ds(r0, SUBLANES), :] * c
        u_s[pl.ds(r0, SUBLANES), :] = h
        return jnp.broadcast_to(h[edge:edge + 1, :], (SUBLANES, D_RNN))

    c0 = jnp.broadcast_to(carry_row, (SUBLANES, D_RNN))
    c_last = lax.fori_loop(0, g, body, c0, unroll=4)
    return c_last[0:1, :]


def _lru_kernel(*refs, reverse, with_pool, ts, nt, seq_len):
    if with_pool:
        (x_ref, xprev_ref, xnext_ref, mod_ref, g_ref, win_ref, h0_ref, cw_ref, cb_ref, wa_ref, ba_ref, wi_ref,
         bi_ref, lam_ref, pw_ref, ps_ref, h_ref, st_ref, yp_ref, xc_ref, gate_out_ref,
         xe_s, ext_s, a_s, u_s, carry_s, b2_s, b4_s, b8_s) = refs
    else:
        (xc_in_ref, gate_ref, hf_ref, h0_ref, wa_ref, ba_ref, wi_ref,
         bi_ref, lam_ref, y_ref, st_ref, a_s, u_s, carry_s) = refs
    j = pl.program_id(1)
    t = (nt - 1 - j) if reverse else j
    n2 = ts + 2 * HALO

    @pl.when(j == 0)
    def _():
        carry_s[...] = h0_ref[0]

    if with_pool:
        xe_s[0:HALO, :] = xprev_ref[0]
        xe_s[HALO:HALO + ts, :] = x_ref[0]
        xe_s[HALO + ts:n2, :] = xnext_ref[0]
        shift = mod_ref[0, :, 0:D_MODEL]
        scale = mod_ref[0, :, D_MODEL:2 * D_MODEL]
        hx = _rms_mod(xe_s[...], g_ref[...], shift, scale).astype(BF16)
        z = _dot(hx, win_ref[...])
        gate_out_ref[0] = z[HALO:HALO + ts, D_RNN:2 * D_RNN]

        def fill_ext(cols):
            ext_s[0:HALO, :] = jnp.where(t > 0, z[0:HALO, cols], 0.0)
            ext_s[HALO:HALO + ts, :] = z[HALO:HALO + ts, cols]
            ext_s[HALO + ts:n2, :] = jnp.where(t < nt - 1, z[HALO + ts:n2, cols], 0.0)

        fill_ext(slice(0, D_RNN))
        xc = cb_ref[...]
        for k in range(CONV_W):
            off = HALO - CONV_W // 2 + k
            xc = xc + ext_s[off:off + ts, :] * cw_ref[k:k + 1, :]
        xc_ref[0] = xc
        fill_ext(slice(2 * D_RNN, 3 * D_RNN))
    else:
        xc = xc_in_ref[0]

    a, u = _lru_coeffs(xc, wa_ref, ba_ref, wi_ref, bi_ref, lam_ref)
    c_last = _scan_tile(a, u, carry_s[...], reverse, a_s, u_s, ts)
    carry_s[...] = c_last
    st_ref[0] = c_last
    h = u_s[...]

    if not with_pool:
        y_ref[0] = ((hf_ref[0] + h) * jax.nn.gelu(gate_ref[0])).astype(BF16)
        return

    h_ref[0] = h

    b2_s[1:n2, :] = ext_s[1:n2, :] + ext_s[0:n2 - 1, :]
    b4_s[2:n2 - 1, :] = b2_s[3:n2, LANES:] + b2_s[1:n2 - 2, LANES:]
    b8_s[4:n2 - 3, :] = b4_s[6:n2 - 1, LANES:] + b4_s[2:n2 - 5, LANES:]
    s16 = b8_s[HALO + 4:HALO + 4 + ts, LANES:] + b8_s[HALO - 4:HALO - 4 + ts, LANES:]
    sums = (b2_s[HALO:HALO + ts, 0:LANES], b4_s[HALO:HALO + ts, 0:LANES], b8_s[HALO:HALO + ts, 0:LANES], s16)
    before = (t * ts + lax.broadcasted_iota(jnp.int32, (ts, LANES), 0)).astype(F32)
    after = float(seq_len - 1) - before
    for gidx, w in enumerate(POOL_WINDOWS):
        left = w // 2
        right = w - 1 - left
        cnt = jnp.minimum(before, float(left)) + (jnp.minimum(after, float(right)) + 1.0)
        xg = ext_s[HALO:HALO + ts, gidx * LANES:(gidx + 1) * LANES]
        dlt = sums[gidx] / cnt - xg
        yg = _dot(dlt.astype(BF16), pw_ref[gidx]) * ps_ref[:, gidx * LANES:(gidx + 1) * LANES]
        yp_ref[0, :, gidx * LANES:(gidx + 1) * LANES] = yg.astype(BF16)


def _lru_call(x3, mod3, h0, fwd, prm, *, reverse, ts):
    nseq, seq_len = (x3 if not reverse else fwd[0]).shape[:2]
    nt = seq_len // ts
    tb = ts // HALO
    nhb = seq_len // HALO
    d = 1 if reverse else 0
    with_pool = not reverse

    def tpos(j):
        return (nt - 1 - j) if reverse else j

    def const(shape):
        nd = len(shape)
        return pl.BlockSpec(shape, lambda s, j: (0,) * nd)

    tile_out = pl.BlockSpec((1, ts, D_RNN), lambda s, j: (s, tpos(j), 0))
    st_spec = pl.BlockSpec((1, 1, D_RNN), lambda s, j: (s, 0, 0))
    conv_specs = [const((CONV_W, D_RNN)), const((1, D_RNN))]
    conv_args = [prm["conv_w"], prm["conv_b"]]
    w_specs = [const((2, 256, 256)), const((1, D_RNN)), const((2, 256, 256)), const((1, D_RNN)), const((1, D_RNN))]
    w_args = [prm["wa"][d], prm["ba"][d], prm["wi"][d], prm["bi"][d], prm["lam"][d]]
    scratch = [pltpu.VMEM((ts, D_RNN), F32), pltpu.VMEM((ts, D_RNN), F32), pltpu.VMEM((1, D_RNN), F32)]
    kern = functools.partial(_lru_kernel, reverse=reverse, with_pool=with_pool, ts=ts, nt=nt, seq_len=seq_len)
    if with_pool:
        nb = mod3.shape[0]
        x_specs = [
            pl.BlockSpec((1, ts, D_MODEL), lambda s, j: (s, j, 0)),
            pl.BlockSpec((1, HALO, D_MODEL), lambda s, j: (s, jnp.maximum(j * tb - 1, 0), 0)),
            pl.BlockSpec((1, HALO, D_MODEL), lambda s, j: (s, jnp.minimum((j + 1) * tb, nhb - 1), 0)),
            pl.BlockSpec((1, 1, 6 * D_MODEL), lambda s, j: ((s * nb) // nseq, 0, 0)),
            const((1, D_MODEL)), const(prm["w_in"].shape)]
        in_specs = x_specs + [st_spec] + conv_specs + w_specs + [
            const((4, POOL_GROUP, POOL_GROUP)), const((1, D_POOL))]
        args = [x3, x3, x3, mod3, prm["g_mix"], prm["w_in"], h0] + conv_args + w_args + [
            prm["pool_w"], prm["pool_scale"]]
        out_specs = [tile_out, st_spec, tile_out, tile_out, tile_out]
        out_shape = [jax.ShapeDtypeStruct((nseq, seq_len, D_RNN), F32),
                     jax.ShapeDtypeStruct((nseq, 1, D_RNN), F32),
                     jax.ShapeDtypeStruct((nseq, seq_len, D_POOL), BF16),
                     jax.ShapeDtypeStruct((nseq, seq_len, D_RNN), F32),
                     jax.ShapeDtypeStruct((nseq, seq_len, D_RNN), F32)]
        n2 = ts + 2 * HALO
        scratch = [pltpu.VMEM((n2, D_MODEL), F32), pltpu.VMEM((n2, D_RNN), F32)] + scratch + [
            pltpu.VMEM((n2, D_POOL), F32), pltpu.VMEM((n2, D_POOL - LANES), F32),
            pltpu.VMEM((n2, D_POOL - 2 * LANES), F32)]
        name = "l0_in_lru_fwd_pool"
    else:
        hf, xc, gate = fwd
        in_specs = [tile_out, tile_out, tile_out, st_spec] + w_specs
        args = [xc, gate, hf, h0] + w_args
        out_specs = [tile_out, st_spec]
        out_shape = [jax.ShapeDtypeStruct((nseq, seq_len, D_RNN), BF16),
                     jax.ShapeDtypeStruct((nseq, 1, D_RNN), F32)]
        name = "l0_lru_bwd"
    return pl.pallas_call(
        kern,
        grid=(nseq, nt),
        in_specs=in_specs,
        out_specs=out_specs,
        out_shape=out_shape,
        scratch_shapes=scratch,
        compiler_params=_cparams(("arbitrary", "arbitrary")),
        name=name,
    )(*args)


FF_CHUNK = 1024
FFN_ROWS = 1024


def _residual_ffn(x, mix, mod_ref, g_ref, w1_ref, w2_ref):
    def mod(k):
        return mod_ref[0, :, k * D_MODEL:(k + 1) * D_MODEL]

    x1 = x + mod(2) * mix
    hn = _rms_mod(x1, g_ref[...], mod(3), mod(4)).astype(BF16)
    acc = None
    for c in range(D_FF // FF_CHUNK):
        hc = _dot(hn, w1_ref[:, c * FF_CHUNK:(c + 1) * FF_CHUNK])
        hc = jnp.square(jnp.maximum(hc, 0.0)).astype(BF16)
        part = _dot(hc, w2_ref[c * FF_CHUNK:(c + 1) * FF_CHUNK, :])
        acc = part if acc is None else acc + part
    return x1 + mod(5) * acc


def _ffn_kernel(*refs, n_y, final_norm):
    x_ref = refs[0]
    y_refs = refs[1:1 + n_y]
    mod_ref = refs[1 + n_y]
    wo_refs = refs[2 + n_y:2 + 2 * n_y]
    g_ref, w1_ref, w2_ref = refs[2 + 2 * n_y:5 + 2 * n_y]
    rest = refs[5 + 2 * n_y:]
    if final_norm:
        gf_ref, o_ref = rest
    else:
        (o_ref,) = rest

    mix = _dot(y_refs[0][...], wo_refs[0][...])
    for k in range(1, n_y):
        mix = mix + _dot(y_refs[k][...], wo_refs[k][...])
    x2 = _residual_ffn(x_ref[...], mix, mod_ref, g_ref, w1_ref, w2_ref)
    if final_norm:
        ms = jnp.mean(x2 * x2, axis=-1, keepdims=True)
        x2 = x2 * lax.rsqrt(ms + EPS) * gf_ref[...]
    o_ref[...] = x2


def _ffn_call(x2d, ys, mod3, wos, g, w1, w2, g_final, tm):
    t = x2d.shape[0]
    nb = mod3.shape[0]
    rows_per_b = t // nb
    n_y = len(ys)
    final_norm = g_final is not None

    def const(shape):
        return pl.BlockSpec(shape, lambda i: (0, 0), pipeline_mode=pl.Buffered(1))

    in_specs = [pl.BlockSpec((tm, D_MODEL), lambda i: (i, 0))]
    in_specs += [pl.BlockSpec((tm, y.shape[1]), lambda i: (i, 0)) for y in ys]
    in_specs += [pl.BlockSpec((1, 1, 6 * D_MODEL), lambda i: ((i * tm) // rows_per_b, 0, 0))]
    in_specs += [const(w.shape) for w in wos]
    in_specs += [const((1, D_MODEL)), const((D_MODEL, D_FF)), const((D_FF, D_MODEL))]
    args = [x2d, *ys, mod3, *wos, g.reshape(1, D_MODEL), w1, w2]
    if final_norm:
        in_specs.append(const((1, D_MODEL)))
        args.append(g_final.reshape(1, D_MODEL))
    return pl.pallas_call(
        functools.partial(_ffn_kernel, n_y=n_y, final_norm=final_norm),
        grid=(t // tm,),
        in_specs=in_specs,
        out_specs=pl.BlockSpec((tm, D_MODEL), lambda i: (i, 0)),
        out_shape=jax.ShapeDtypeStruct((t, D_MODEL), F32),
        compiler_params=_cparams(("arbitrary",)),
        name="mix_out_ffn",
    )(*args)


def _rope(x, cos, sin):
    return x * cos + pltpu.roll(x, 8, axis=1) * sin


def _mla_proj_kernel(*refs, rope, emit_cache):
    x_ref, mod_ref, g_ref, win_ref, gq_ref, wqb_ref, gkv_ref, wk_ref, wv_ref, vones_ref, pemask_ref = refs[:11]
    rest = refs[11:]
    if rope:
        cos_ref, sin_ref = rest[:2]
        rest = rest[2:]
    q_ref, k_ref, v_ref = rest[:3]
    if emit_cache:
        ckv_ref, kpe_ref = rest[3:]

    shift = mod_ref[0, :, 0:D_MODEL]
    scale = mod_ref[0, :, D_MODEL:2 * D_MODEL]
    h = _rms_mod(x_ref[...], g_ref[...], shift, scale).astype(BF16)
    z = _dot(h, win_ref[...])
    cq = z[:, :Q_RANK]
    ckv = z[:, Q_RANK:Q_RANK + KV_RANK]
    kpe = z[:, Q_RANK + KV_RANK:]
    cqn = cq * lax.rsqrt(jnp.mean(cq * cq, axis=-1, keepdims=True) + EPS) * gq_ref[...]
    ckvn = ckv * lax.rsqrt(jnp.mean(ckv * ckv, axis=-1, keepdims=True) + EPS) * gkv_ref[...]
    if emit_cache:
        ckv_ref[...] = ckvn
        kpe_ref[...] = kpe
    q = _dot(cqn.astype(BF16), wqb_ref[...])
    ckvb = ckvn.astype(BF16)
    kn = _dot(ckvb, wk_ref[...])
    v_ref[...] = (_dot(ckvb, wv_ref[...]) + vones_ref[...]).astype(BF16)
    if rope:
        cos, sin = cos_ref[...], sin_ref[...]
        kpe = _rope(kpe, cos, sin)
    else:
        kpe = kpe * pemask_ref[...]
    for hd in range(N_HEADS):
        sl = slice(hd * HEAD_PAD, (hd + 1) * HEAD_PAD)
        qh = q[:, sl]
        if rope:
            qh = _rope(qh, cos, sin)
        q_ref[:, sl] = qh.astype(BF16)
        k_ref[:, sl] = (kn[:, sl] + kpe).astype(BF16)


def _mla_proj_call(x2d, mod3, prm, tables, *, seq_len, emit_cache, tm):
    t = x2d.shape[0]
    nb = mod3.shape[0]
    rows_per_b = t // nb
    rope = tables is not None
    tiles_per_seq = seq_len // tm
    hw = N_HEADS * HEAD_PAD

    def const(shape):
        return pl.BlockSpec(shape, lambda i: (0, 0))

    def rows(n):
        return pl.BlockSpec((tm, n), lambda i: (i, 0))

    in_specs = [rows(D_MODEL),
                pl.BlockSpec((1, 1, 6 * D_MODEL), lambda i: ((i * tm) // rows_per_b, 0, 0)),
                const((1, D_MODEL)), const(prm["w_in"].shape), const((1, Q_RANK)), const(prm["w_qb"].shape),
                const((1, KV_RANK)), const(prm["w_k"].shape), const(prm["w_v"].shape), const((1, hw)),
                const((1, LANES))]
    args = [x2d, mod3, prm["g_mix"], prm["w_in"], prm["g_q"], prm["w_qb"], prm["g_kv"], prm["w_k"], prm["w_v"],
            prm["v_ones"], prm["pe_mask"]]
    if rope:
        in_specs += [pl.BlockSpec((tm, LANES), lambda i: (i % tiles_per_seq, 0))] * 2
        args += list(tables)
    out_specs = [rows(hw), rows(hw), rows(hw)]
    out_shape = [jax.ShapeDtypeStruct((t, hw), BF16)] * 3
    if emit_cache:
        out_specs += [rows(KV_RANK), rows(LANES)]
        out_shape += [jax.ShapeDtypeStruct((t, KV_RANK), F32), jax.ShapeDtypeStruct((t, LANES), F32)]
    return pl.pallas_call(
        functools.partial(_mla_proj_kernel, rope=rope, emit_cache=emit_cache),
        grid=(t // tm,),
        in_specs=in_specs,
        out_specs=out_specs,
        out_shape=out_shape,
        compiler_params=_cparams(("arbitrary",)),
        name="l1_mla_proj",
    )(*args)


def _ctx_expand_kernel(ckv_ref, kpe_ref, wk_ref, wv_ref, vones_ref, k_ref, v_ref):
    ckvb = ckv_ref[...].astype(BF16)
    kn = _dot(ckvb, wk_ref[...])
    v_ref[...] = (_dot(ckvb, wv_ref[...]) + vones_ref[...]).astype(BF16)
    kpe = kpe_ref[...]
    for hd in range(N_HEADS):
        sl = slice(hd * HEAD_PAD, (hd + 1) * HEAD_PAD)
        k_ref[:, sl] = (kn[:, sl] + kpe).astype(BF16)


def _ctx_expand_call(ckv2d, kpe2d, prm, tm):
    hw = N_HEADS * HEAD_PAD
    t = ckv2d.shape[0]
    const = lambda a: pl.BlockSpec(a.shape, lambda i: (0, 0))
    rows = lambda n: pl.BlockSpec((tm, n), lambda i: (i, 0))
    return pl.pallas_call(
        _ctx_expand_kernel,
        grid=(t // tm,),
        in_specs=[rows(KV_RANK), rows(LANES), const(prm["w_k"]), const(prm["w_v"]), const(prm["v_ones"])],
        out_specs=[rows(hw), rows(hw)],
        out_shape=[jax.ShapeDtypeStruct((t, hw), BF16), jax.ShapeDtypeStruct((t, hw), BF16)],
        compiler_params=_cparams(("arbitrary",)),
        name="l1_ctx_expand",
    )(ckv2d, kpe2d, prm["w_k"], prm["w_v"], prm["v_ones"])


Q_PRESCALE = (QK_NOPE + QK_ROPE) ** -0.5 * math.log2(math.e)
_NT = (((1,), (1,)), ((), ()))
ROW_SUB = 256


def _pair_out(acc_even, acc_odd):
    lane = lax.broadcasted_iota(jnp.int32, acc_even.shape, 1)
    first = lane < V_HEAD
    num = jnp.where(first, acc_even, acc_odd)
    den = jnp.where(first, pltpu.roll(acc_even, V_HEAD, axis=1), pltpu.roll(acc_odd, V_HEAD, axis=1))
    return (num / den).astype(BF16)


def _attn_ctx_kernel(q_ref, k_ref, v_ref, o_ref, s_scr, p_scr, *, pairs):
    heads = [slice(h * HEAD_PAD, (h + 1) * HEAD_PAD) for h in range(2 * pairs)]
    for h, sl in enumerate(heads):
        s_scr[h] = lax.dot_general(q_ref[0, :, sl], k_ref[0, :, sl], _NT, preferred_element_type=F32)
    for h in range(2 * pairs):
        s = s_scr[h]
        p_scr[h] = jnp.exp2(s - s.max(axis=-1, keepdims=True)).astype(BF16)
    for p in range(pairs):
        accs = [_dot(p_scr[2 * p + e], v_ref[0, :, heads[2 * p + e]]) for e in range(2)]
        o_ref[0, :, p * LANES:(p + 1) * LANES] = _pair_out(accs[0], accs[1])


def _attn_ctx_call(q3, k3, v3):
    b, s, _ = q3.shape
    spec = lambda w: pl.BlockSpec((1, s, w), lambda bi: (bi, 0, 0))
    hw = N_HEADS * HEAD_PAD
    return pl.pallas_call(
        functools.partial(_attn_ctx_kernel, pairs=N_HEADS // 2),
        grid=(b,),
        in_specs=[spec(hw), spec(hw), spec(hw)],
        out_specs=spec(N_HEADS * V_HEAD),
        out_shape=jax.ShapeDtypeStruct((b, s, N_HEADS * V_HEAD), BF16),
        scratch_shapes=[pltpu.VMEM((N_HEADS, s, s), F32), pltpu.VMEM((N_HEADS, s, s), BF16)],
        compiler_params=_cparams(("arbitrary",)),
        name="l1_attention_ctx",
    )(q3, k3, v3)


def _attn_lat_kernel(q_ref, k1_ref, k2_ref, v1c_ref, v2c_ref, v1l_ref, v2l_ref, o_ref,
                     s0, s1, m0, m1, p0, p1, acc0_save):
    n = pl.program_id(0)
    n1, n2 = k1_ref.shape[1], k2_ref.shape[1]
    h0 = slice(0, HEAD_PAD)
    h1 = slice(HEAD_PAD, 2 * HEAD_PAD)

    @pl.when(n == 0)
    def _():
        s1[...] = jnp.zeros(s1.shape, F32)
        m1[...] = jnp.zeros(m1.shape, F32)
        acc0_save[...] = jnp.ones(acc0_save.shape, F32)

    row_blocks = [slice(r, r + ROW_SUB) for r in range(0, q_ref.shape[1], ROW_SUB)]

    def scores(sl, s_w, m_w):
        for rows in row_blocks:
            q = q_ref[0, rows, sl]
            sa = lax.dot_general(q, k1_ref[0, :, sl], _NT, preferred_element_type=F32)
            sb = lax.dot_general(q, k2_ref[0, :, sl], _NT, preferred_element_type=F32)
            s_w[rows, 0:n1] = sa
            s_w[rows, n1:] = sb
            m = jnp.maximum(sa.max(axis=-1, keepdims=True), sb.max(axis=-1, keepdims=True))
            m_w[rows, :] = jnp.broadcast_to(m, (ROW_SUB, LANES))

    def values(sl, s_r, m_r, p_scr, va_ref, vb_ref):
        for rows in row_blocks:
            m = m_r[rows, :]
            for t in range((n1 + n2) // LANES):
                tl = slice(t * LANES, (t + 1) * LANES)
                p_scr[rows, tl] = jnp.exp2(s_r[rows, tl] - m).astype(BF16)
        return _dot(p_scr[:, 0:n1], va_ref[0, :, sl]) + _dot(p_scr[:, n1:], vb_ref[0, :, sl])

    @pl.when(n >= 0)
    def _():
        scores(h0, s0, m0)
        acc1 = values(h1, s1, m1, p1, v1l_ref, v2l_ref)
        o_ref[0] = _pair_out(acc0_save[...], acc1)

    @pl.when(n < pl.num_programs(0))
    def _():
        scores(h1, s1, m1)
        acc0_save[...] = values(h0, s0, m0, p0, v1c_ref, v2c_ref)


def _attn_lat_call(q3, k1, k2, v1, v2, *, tq):
    b, s, _ = q3.shape
    s1, s2 = k1.shape[1], k2.shape[1]
    sk = s1 + s2
    npair = N_HEADS // 2
    nq = s // tq
    n_items = b * npair * nq

    def item(n):
        return n // (npair * nq), n % nq, (n // nq) % npair

    def cur(n):
        return item(jnp.minimum(n, n_items - 1))

    def lag(n):
        return item(jnp.maximum(n - 1, 0))

    qw = 2 * HEAD_PAD
    kv_cur = lambda rows: pl.BlockSpec((1, rows, qw), lambda n: (cur(n)[0], 0, cur(n)[2]))
    kv_lag = lambda rows: pl.BlockSpec((1, rows, qw), lambda n: (lag(n)[0], 0, lag(n)[2]))
    in_specs = [pl.BlockSpec((1, tq, qw), lambda n: cur(n)),
                kv_cur(s1), kv_cur(s2), kv_cur(s1), kv_cur(s2), kv_lag(s1), kv_lag(s2)]
    return pl.pallas_call(
        _attn_lat_kernel,
        grid=(n_items + 1,),
        in_specs=in_specs,
        out_specs=pl.BlockSpec((1, tq, LANES), lambda n: lag(n)),
        out_shape=jax.ShapeDtypeStruct((b, s, N_HEADS * V_HEAD), BF16),
        scratch_shapes=[pltpu.VMEM((tq, sk), F32)] * 2 + [pltpu.VMEM((tq, LANES), F32)] * 2
        + [pltpu.VMEM((tq, sk), BF16)] * 2 + [pltpu.VMEM((tq, LANES), F32)],
        compiler_params=_cparams(("arbitrary",)),
        name="l1_attention_lat",
    )(q3, k1, k2, v1, v2, v1, v2)


def _block_diag_halves(w):
    w4 = w.astype(BF16).reshape(2, 4, RNN_BLOCK, RNN_BLOCK)
    eye = jnp.eye(4, dtype=BF16)
    return (w4[:, :, :, None, :] * eye[None, :, None, :, None]).reshape(2, 4 * RNN_BLOCK, 4 * RNN_BLOCK)


def _rope_tables(seq_len):
    rows = seq_len // GRID_W
    row = np.repeat(np.arange(rows, dtype=np.float64), GRID_W)
    col = np.tile(np.arange(GRID_W, dtype=np.float64), rows)
    half = QK_ROPE // 2
    inv = ROPE_BASE ** (-np.arange(0, half, 2, dtype=np.float64) / half)
    ang_r, ang_c = row[:, None] * inv, col[:, None] * inv
    zeros = np.zeros((seq_len, 8))
    cr, sr, cc, sc = np.cos(ang_r), np.sin(ang_r), np.cos(ang_c), np.sin(ang_c)
    lead = np.ones((seq_len, QK_NOPE))
    tail = np.zeros((seq_len, HEAD_PAD - QK_NOPE - PE_WIDTH))
    cos = np.concatenate([lead, zeros, cr, cr, zeros, cc, cc, tail], axis=1)
    sin = np.concatenate([0 * lead, zeros, -sr, sr, zeros, -sc, sc, tail], axis=1)
    return tuple(jnp.asarray(t, dtype=F32) for t in (cos, sin))


def _pe_layout_cols(w):
    x1r, x2r, x1c, x2c = (w[..., 8 * i:8 * (i + 1)] for i in range(4))
    return jnp.concatenate([x2r, x1r, x2r, x2c, x1c, x2c], axis=-1)


def kernel(x_prompt, x_sample, state_l0_lru, cache_l1_ckv, cache_l1_kpe, c, c_ctx, l0_w_mod, l0_b_mod, l0_g_mix, l0_g_ffn, l0_w_in, l0_conv_w, l0_conv_b, l0_lru_w_a, l0_lru_b_a, l0_lru_w_i, l0_lru_b_i, l0_lru_lam, l0_pool_w, l0_pool_scale, l0_w_out, l0_ffn_w1, l0_ffn_w2, l1_w_mod, l1_b_mod, l1_g_mix, l1_g_ffn, l1_w_in, l1_g_q, l1_w_qb, l1_g_kv, l1_w_kvb, l1_w_out, l1_ffn_w1, l1_ffn_w2, g_final):
    bp, sp, d = x_prompt.shape
    bs, ss, _ = x_sample.shape
    past = cache_l1_ckv.shape[1]
    xp = x_prompt.reshape(bp * sp, d)
    xs = x_sample.reshape(bs * ss, d)

    cond8 = jnp.concatenate([c, c_ctx[None, :], jnp.zeros((SUBLANES - bs - 1, d), F32)], axis=0)
    m0 = _mod_call(cond8, l0_w_mod, l0_b_mod)
    m1 = _mod_call(cond8, l1_w_mod, l1_b_mod)
    mod_p = [m[bs:bs + 1].reshape(1, 1, 6 * d) for m in (m0, m1)]
    mod_s = [m[0:bs].reshape(bs, 1, 6 * d) for m in (m0, m1)]

    lru = dict(
        conv_w=l0_conv_w, conv_b=l0_conv_b.reshape(1, D_RNN),
        wa=[_block_diag_halves(l0_lru_w_a[i]) for i in range(2)],
        wi=[_block_diag_halves(l0_lru_w_i[i]) for i in range(2)],
        ba=[l0_lru_b_a[i].reshape(1, D_RNN) for i in range(2)],
        bi=[l0_lru_b_i[i].reshape(1, D_RNN) for i in range(2)],
        lam=[l0_lru_lam[i].reshape(1, D_RNN) for i in range(2)],
        pool_w=l0_pool_w.astype(BF16), pool_scale=l0_pool_scale.reshape(1, D_POOL),
        g_mix=l0_g_mix.reshape(1, d), w_in=l0_w_in.astype(BF16))
    wo0 = l0_w_out.astype(BF16)
    wos0 = [wo0[:D_RNN], wo0[D_RNN:]]
    w1_0, w2_0 = l0_ffn_w1.astype(BF16), l0_ffn_w2.astype(BF16)

    def layer0(x2d, mod3, nseq, seq_len, h0f, h0b, ts):
        x3 = x2d.reshape(nseq, seq_len, d)
        hf, stf, ypool, xc, gate = _lru_call(x3, mod3, h0f, None, lru, reverse=False, ts=ts)
        yrnn, stb = _lru_call(None, None, h0b, (hf, xc, gate), lru, reverse=True, ts=ts)
        ys = [yrnn.reshape(-1, D_RNN), ypool.reshape(-1, D_POOL)]
        x2 = _ffn_call(x2d, ys, mod3, wos0, l0_g_ffn, w1_0, w2_0, None, tm=FFN_ROWS)
        return x2, stf, stb

    zero_st = jnp.zeros((bp, 1, D_RNN), F32)
    xp, stf, stb = layer0(xp, mod_p[0], bp, sp, zero_st, zero_st, ts=sp)
    xs, _, _ = layer0(xs, mod_s[0], bs, ss, state_l0_lru[:, 0:1], state_l0_lru[:, 1:2], ts=512)
    new_lru = jnp.concatenate([stf, stb], axis=1)

    pad_pe = HEAD_PAD - QK_NOPE - PE_WIDTH
    w_in1b = l1_w_in.astype(BF16)
    w_in1 = jnp.concatenate([w_in1b[:, :Q_RANK + KV_RANK], jnp.zeros((d, QK_NOPE), BF16),
                             _pe_layout_cols(w_in1b[:, Q_RANK + KV_RANK:]), jnp.zeros((d, pad_pe), BF16)], axis=1)
    wqb = l1_w_qb.astype(BF16).reshape(Q_RANK, N_HEADS, QK_NOPE + QK_ROPE)
    wqb = jnp.concatenate([wqb[:, :, :QK_NOPE], _pe_layout_cols(wqb[:, :, QK_NOPE:]),
                           jnp.zeros((Q_RANK, N_HEADS, pad_pe), BF16)], axis=2).reshape(Q_RANK, N_HEADS * HEAD_PAD)
    pe_mask = jnp.concatenate([jnp.ones((QK_NOPE,), F32)] + [m * jnp.ones((8,), F32) for m in (0, 1, 1, 0, 1, 1)]
                              + [jnp.zeros((pad_pe,), F32)]).reshape(1, HEAD_PAD)
    wkvb = l1_w_kvb.astype(BF16).reshape(KV_RANK, N_HEADS, QK_NOPE + V_HEAD)
    w_k = jnp.pad(wkvb[:, :, :QK_NOPE], ((0, 0), (0, 0), (0, HEAD_PAD - QK_NOPE)))
    w_k = w_k.reshape(KV_RANK, N_HEADS * HEAD_PAD)
    wv = wkvb[:, :, QK_NOPE:].reshape(KV_RANK, N_HEADS // 2, 2, V_HEAD)
    zv = jnp.zeros((KV_RANK, N_HEADS // 2, V_HEAD), BF16)
    w_v = jnp.stack([wv[:, :, 0], zv, zv, wv[:, :, 1]], axis=2).reshape(KV_RANK, N_HEADS * HEAD_PAD)
    one = jnp.ones((N_HEADS // 2, V_HEAD), F32)
    v_ones = jnp.stack([0 * one, one, one, 0 * one], axis=1).reshape(1, N_HEADS * HEAD_PAD)
    mla = dict(g_mix=l1_g_mix.reshape(1, d), w_in=w_in1, g_q=l1_g_q.reshape(1, Q_RANK) * Q_PRESCALE, w_qb=wqb,
               g_kv=l1_g_kv.reshape(1, KV_RANK), w_k=w_k, w_v=w_v, v_ones=v_ones, pe_mask=pe_mask)
    wo1 = [l1_w_out.astype(BF16)]
    w1_1, w2_1 = l1_ffn_w1.astype(BF16), l1_ffn_w2.astype(BF16)

    qp, kp, vp, ckv_new, kpe_new = _mla_proj_call(xp, mod_p[1], mla, None, seq_len=bp * sp, emit_cache=True,
                                                  tm=512)
    op = _attn_ctx_call(qp.reshape(bp, sp, -1), kp.reshape(bp, sp, -1), vp.reshape(bp, sp, -1))
    y_prompt = _ffn_call(xp, [op.reshape(bp * sp, -1)], mod_p[1], wo1, l1_g_ffn, w1_1, w2_1, g_final, tm=FFN_ROWS)

    tables = _rope_tables(ss)
    qs, ks, vs = _mla_proj_call(xs, mod_s[1], mla, tables, seq_len=ss, emit_cache=False, tm=512)
    kpe_c = cache_l1_kpe.reshape(bs * past, QK_ROPE)
    gap = jnp.zeros((bs * past, 8), F32)
    kpe_ctx = jnp.concatenate([jnp.zeros((bs * past, QK_NOPE), F32), gap, kpe_c[:, :16], gap, kpe_c[:, 16:],
                               jnp.zeros((bs * past, pad_pe), F32)], axis=1)
    kc, vc = _ctx_expand_call(cache_l1_ckv.reshape(bs * past, KV_RANK), kpe_ctx, mla, tm=past)
    os_ = _attn_lat_call(qs.reshape(bs, ss, -1), ks.reshape(bs, ss, -1), kc.reshape(bs, past, -1),
                         vs.reshape(bs, ss, -1), vc.reshape(bs, past, -1), tq=512)
    y_sample = _ffn_call(xs, [os_.reshape(bs * ss, -1)], mod_s[1], wo1, l1_g_ffn, w1_1, w2_1, g_final, tm=FFN_ROWS)

    new_ckv = ckv_new.reshape(bp, sp, KV_RANK)
    new_kpe = jnp.concatenate([kpe_new[:, QK_NOPE + 8:QK_NOPE + 24], kpe_new[:, QK_NOPE + 32:QK_NOPE + 48]],
                              axis=1).reshape(bp, sp, QK_ROPE)
    return (y_prompt.reshape(bp, sp, d), y_sample.reshape(bs, ss, d), new_lru, new_ckv, new_kpe)
```

```python
import functools
import math

import jax
import jax.numpy as jnp
import numpy as np
from jax import lax
from jax.experimental import pallas as pl
from jax.experimental.pallas import tpu as pltpu

F32 = jnp.float32
BF16 = jnp.bfloat16

D_MODEL = 1024
D_FF = 4 * D_MODEL
EPS = 1e-6
D_RNN = 512
RNN_BLOCKS = 8
RNN_BLOCK = 64
CONV_W = 4
LRU_C = 8.0
D_POOL = 512
POOL_WINDOWS = (2, 4, 8, 16)
POOL_GROUP = 128
N_HEADS = 16
QK_NOPE = 64
QK_ROPE = 32
V_HEAD = 64
Q_RANK = 384
KV_RANK = 256
ROPE_BASE = 10000.0
GRID_W = 64

LANES = 128
SUBLANES = 8
HEAD_PAD = 128
PE_WIDTH = 48
HALO = 8
VMEM_LIMIT = 52 * 1024 * 1024
SEQ_TILE = 512


def _cparams(sem):
    return pltpu.CompilerParams(dimension_semantics=sem, vmem_limit_bytes=VMEM_LIMIT)


def _dot(a, b):
    return jnp.dot(a, b, preferred_element_type=F32)


def _sigmoid(x):
    return 0.5 * jnp.tanh(0.5 * x) + 0.5


def _rms_mod(x, g, shift, scale):
    ms = jnp.mean(x * x, axis=-1, keepdims=True)
    y = x * lax.rsqrt(ms + EPS) * g
    return y * (1.0 + scale) + shift


def _mod_kernel(c_ref, w_ref, b_ref, o_ref):
    c = c_ref[...]
    s = c * jax.nn.sigmoid(c)
    o_ref[...] = _dot(s.astype(BF16), w_ref[...].astype(BF16)) + b_ref[...]


def _mod_call(cond8, w_mod, b_mod):
    n = w_mod.shape[1]
    tn = 1024
    return pl.pallas_call(
        _mod_kernel,
        grid=(n // tn,),
        in_specs=[
            pl.BlockSpec((SUBLANES, D_MODEL), lambda j: (0, 0)),
            pl.BlockSpec((D_MODEL, tn), lambda j: (0, j)),
            pl.BlockSpec((1, tn), lambda j: (0, j)),
        ],
        out_specs=pl.BlockSpec((SUBLANES, tn), lambda j: (0, j)),
        out_shape=jax.ShapeDtypeStruct((SUBLANES, n), F32),
        compiler_params=_cparams(("arbitrary",)),
        name="adaln_mod",
    )(cond8, w_mod, b_mod.reshape(1, n))


def _lru_coeffs(xc, wa_ref, ba_ref, wi_ref, bi_ref, lam_ref):
    xcb = xc.astype(BF16)
    half = D_RNN // 2
    ga = jnp.concatenate([_dot(xcb[:, :half], wa_ref[0]), _dot(xcb[:, half:], wa_ref[1])], axis=-1)
    gi = jnp.concatenate([_dot(xcb[:, :half], wi_ref[0]), _dot(xcb[:, half:], wi_ref[1])], axis=-1)
    r = _sigmoid(ga + ba_ref[...])
    i = _sigmoid(gi + bi_ref[...])
    log_a = (LRU_C * r) * jax.nn.log_sigmoid(lam_ref[...])
    a = jnp.exp(log_a)
    v = -jnp.tanh(log_a) * (a * a + 1.0)
    u = jnp.where(v > 0.0, v * lax.rsqrt(v), 0.0) * (i * xc)
    return a, u


def _scan_tile(a, u, carry_row, reverse, a_s, u_s, ts):
    g = ts // SUBLANES
    a3 = a.reshape(g, SUBLANES, D_RNN)
    u3 = u.reshape(g, SUBLANES, D_RNN)
    row = lax.broadcasted_iota(jnp.int32, (g, SUBLANES, D_RNN), 1)
    for d in (1, 2, 4):
        if reverse:
            sh = SUBLANES - d
            m = row < SUBLANES - d
        else:
            sh = d
            m = row >= d
        a_sh = pltpu.roll(a3, sh, axis=1)
        u_sh = pltpu.roll(u3, sh, axis=1)
        u3 = u3 + a3 * jnp.where(m, u_sh, 0.0)
        a3 = a3 * jnp.where(m, a_sh, 1.0)
    a_s[...] = a3.reshape(ts, D_RNN)
    u_s[...] = u3.reshape(ts, D_RNN)
    edge = 0 if reverse else SUBLANES - 1

    def body(k, c):
        gi = (g - 1 - k) if reverse else k
        r0 = pl.multiple_of(gi * SUBLANES, SUBLANES)
        h = u_s[pl.ds(r0, SUBLANES), :] + a_s[pl.ds(r0, SUBLANES), :] * c
        u_s[pl.ds(r0, SUBLANES), :] = h
        return jnp.broadcast_to(h[edge:edge + 1, :], (SUBLANES, D_RNN))

    c0 = jnp.broadcast_to(carry_row, (SUBLANES, D_RNN))
    c_last = lax.fori_loop(0, g, body, c0, unroll=4)
    return c_last[0:1, :]


def _lru_kernel(*refs, reverse, with_pool, both, ts, nt, seq_len):
    if both:
        (x_ref, xprev_ref, xnext_ref, mod_ref, g_ref, win_ref, h0_ref, cw_ref, cb_ref, wa_ref, ba_ref, wi_ref,
         bi_ref, lam_ref, pw_ref, ps_ref, h0b_ref, wab_ref, bab_ref, wib_ref, bib_ref, lamb_ref,
         y_ref, st_ref, yp_ref, stb_ref,
         xe_s, ext_s, a_s, u_s, carry_s, b2_s, b4_s, b8_s, ab_s, ub_s) = refs
    elif with_pool:
        (x_ref, xprev_ref, xnext_ref, mod_ref, g_ref, win_ref, h0_ref, cw_ref, cb_ref, wa_ref, ba_ref, wi_ref,
         bi_ref, lam_ref, pw_ref, ps_ref, h_ref, st_ref, yp_ref, xc_ref, gate_out_ref,
         xe_s, ext_s, a_s, u_s, carry_s, b2_s, b4_s, b8_s) = refs
    else:
        (xc_in_ref, gate_ref, hf_ref, h0_ref, wa_ref, ba_ref, wi_ref,
         bi_ref, lam_ref, y_ref, st_ref, a_s, u_s, carry_s) = refs
    j = pl.program_id(1)
    t = (nt - 1 - j) if reverse else j
    n2 = ts + 2 * HALO

    @pl.when(j == 0)
    def _():
        carry_s[...] = h0_ref[0]

    if with_pool:
        xe_s[0:HALO, :] = xprev_ref[0]
        xe_s[HALO:HALO + ts, :] = x_ref[0]
        xe_s[HALO + ts:n2, :] = xnext_ref[0]
        shift = mod_ref[0, :, 0:D_MODEL]
        scale = mod_ref[0, :, D_MODEL:2 * D_MODEL]
        hx = _rms_mod(xe_s[...], g_ref[...], shift, scale).astype(BF16)
        z = _dot(hx, win_ref[...])
        if not both:
            gate_out_ref[0] = z[HALO:HALO + ts, D_RNN:2 * D_RNN]

        def fill_ext(cols):
            ext_s[0:HALO, :] = jnp.where(t > 0, z[0:HALO, cols], 0.0)
            ext_s[HALO:HALO + ts, :] = z[HALO:HALO + ts, cols]
            ext_s[HALO + ts:n2, :] = jnp.where(t < nt - 1, z[HALO + ts:n2, cols], 0.0)

        fill_ext(slice(0, D_RNN))
        xc = cb_ref[...]
        for k in range(CONV_W):
            off = HALO - CONV_W // 2 + k
            xc = xc + ext_s[off:off + ts, :] * cw_ref[k:k + 1, :]
        if not both:
            xc_ref[0] = xc
        fill_ext(slice(2 * D_RNN, 3 * D_RNN))
    else:
        xc = xc_in_ref[0]

    a, u = _lru_coeffs(xc, wa_ref, ba_ref, wi_ref, bi_ref, lam_ref)
    c_last = _scan_tile(a, u, carry_s[...], reverse, a_s, u_s, ts)
    carry_s[...] = c_last
    st_ref[0] = c_last
    h = u_s[...]

    if not with_pool:
        y_ref[0] = ((hf_ref[0] + h) * jax.nn.gelu(gate_ref[0])).astype(BF16)
        return

    if both:
        ab, ub = _lru_coeffs(xc, wab_ref, bab_ref, wib_ref, bib_ref, lamb_ref)
        stb_ref[0] = _scan_tile(ab, ub, h0b_ref[0], True, ab_s, ub_s, ts)
        gate = z[HALO:HALO + ts, D_RNN:2 * D_RNN]
        y_ref[0] = ((h + ub_s[...]) * jax.nn.gelu(gate)).astype(BF16)
    else:
        h_ref[0] = h

    b2_s[1:n2, :] = ext_s[1:n2, :] + ext_s[0:n2 - 1, :]
    b4_s[2:n2 - 1, :] = b2_s[3:n2, LANES:] + b2_s[1:n2 - 2, LANES:]
    b8_s[4:n2 - 3, :] = b4_s[6:n2 - 1, LANES:] + b4_s[2:n2 - 5, LANES:]
    s16 = b8_s[HALO + 4:HALO + 4 + ts, LANES:] + b8_s[HALO - 4:HALO - 4 + ts, LANES:]
    sums = (b2_s[HALO:HALO + ts, 0:LANES], b4_s[HALO:HALO + ts, 0:LANES], b8_s[HALO:HALO + ts, 0:LANES], s16)
    before = (t * ts + lax.broadcasted_iota(jnp.int32, (ts, LANES), 0)).astype(F32)
    after = float(seq_len - 1) - before
    for gidx, w in enumerate(POOL_WINDOWS):
        left = w // 2
        right = w - 1 - left
        cnt = jnp.minimum(before, float(left)) + (jnp.minimum(after, float(right)) + 1.0)
        xg = ext_s[HALO:HALO + ts, gidx * LANES:(gidx + 1) * LANES]
        dlt = sums[gidx] / cnt - xg
        yg = _dot(dlt.astype(BF16), pw_ref[gidx]) * ps_ref[:, gidx * LANES:(gidx + 1) * LANES]
        yp_ref[0, :, gidx * LANES:(gidx + 1) * LANES] = yg.astype(BF16)


def _lru_call(x3, mod3, h0, fwd, prm, *, reverse, ts, h0b=None):
    nseq, seq_len = (x3 if not reverse else fwd[0]).shape[:2]
    nt = seq_len // ts
    tb = ts // HALO
    nhb = seq_len // HALO
    d = 1 if reverse else 0
    with_pool = not reverse
    both = h0b is not None
    assert not both or (with_pool and nt == 1)

    def tpos(j):
        return (nt - 1 - j) if reverse else j

    def const(shape):
        nd = len(shape)
        return pl.BlockSpec(shape, lambda s, j: (0,) * nd)

    tile_out = pl.BlockSpec((1, ts, D_RNN), lambda s, j: (s, tpos(j), 0))
    st_spec = pl.BlockSpec((1, 1, D_RNN), lambda s, j: (s, 0, 0))
    conv_specs = [const((CONV_W, D_RNN)), const((1, D_RNN))]
    conv_args = [prm["conv_w"], prm["conv_b"]]
    w_specs = [const((2, 256, 256)), const((1, D_RNN)), const((2, 256, 256)), const((1, D_RNN)), const((1, D_RNN))]
    w_args = [prm["wa"][d], prm["ba"][d], prm["wi"][d], prm["bi"][d], prm["lam"][d]]
    scratch = [pltpu.VMEM((ts, D_RNN), F32), pltpu.VMEM((ts, D_RNN), F32), pltpu.VMEM((1, D_RNN), F32)]
    kern = functools.partial(_lru_kernel, reverse=reverse, with_pool=with_pool, both=both, ts=ts, nt=nt,
                             seq_len=seq_len)
    if with_pool:
        nb = mod3.shape[0]
        x_specs = [
            pl.BlockSpec((1, ts, D_MODEL), lambda s, j: (s, j, 0)),
            pl.BlockSpec((1, HALO, D_MODEL), lambda s, j: (s, jnp.maximum(j * tb - 1, 0), 0)),
            pl.BlockSpec((1, HALO, D_MODEL), lambda s, j: (s, jnp.minimum((j + 1) * tb, nhb - 1), 0)),
            pl.BlockSpec((1, 1, 6 * D_MODEL), lambda s, j: ((s * nb) // nseq, 0, 0)),
            const((1, D_MODEL)), const(prm["w_in"].shape)]
        in_specs = x_specs + [st_spec] + conv_specs + w_specs + [
            const((4, POOL_GROUP, POOL_GROUP)), const((1, D_POOL))]
        args = [x3, x3, x3, mod3, prm["g_mix"], prm["w_in"], h0] + conv_args + w_args + [
            prm["pool_w"], prm["pool_scale"]]
        out_specs = [tile_out, st_spec, tile_out, tile_out, tile_out]
        out_shape = [jax.ShapeDtypeStruct((nseq, seq_len, D_RNN), F32),
                     jax.ShapeDtypeStruct((nseq, 1, D_RNN), F32),
                     jax.ShapeDtypeStruct((nseq, seq_len, D_POOL), BF16),
                     jax.ShapeDtypeStruct((nseq, seq_len, D_RNN), F32),
                     jax.ShapeDtypeStruct((nseq, seq_len, D_RNN), F32)]
        n2 = ts + 2 * HALO
        scratch = [pltpu.VMEM((n2, D_MODEL), F32), pltpu.VMEM((n2, D_RNN), F32)] + scratch + [
            pltpu.VMEM((n2, D_POOL), F32), pltpu.VMEM((n2, D_POOL - LANES), F32),
            pltpu.VMEM((n2, D_POOL - 2 * LANES), F32)]
        name = "l0_in_lru_fwd_pool"
        if both:
            in_specs = in_specs + [st_spec] + w_specs
            args = args + [h0b, prm["wa"][1], prm["ba"][1], prm["wi"][1], prm["bi"][1], prm["lam"][1]]
            out_specs = [tile_out, st_spec, tile_out, st_spec]
            out_shape = [jax.ShapeDtypeStruct((nseq, seq_len, D_RNN), BF16), out_shape[1], out_shape[2],
                         out_shape[1]]
            scratch = scratch + [pltpu.VMEM((ts, D_RNN), F32), pltpu.VMEM((ts, D_RNN), F32)]
            name = "l0_in_lru_both_pool"
    else:
        hf, xc, gate = fwd
        in_specs = [tile_out, tile_out, tile_out, st_spec] + w_specs
        args = [xc, gate, hf, h0] + w_args
        out_specs = [tile_out, st_spec]
        out_shape = [jax.ShapeDtypeStruct((nseq, seq_len, D_RNN), BF16),
                     jax.ShapeDtypeStruct((nseq, 1, D_RNN), F32)]
        name = "l0_lru_bwd"
    return pl.pallas_call(
        kern,
        grid=(nseq, nt),
        in_specs=in_specs,
        out_specs=out_specs,
        out_shape=out_shape,
        scratch_shapes=scratch,
        compiler_params=_cparams(("arbitrary", "arbitrary")),
        name=name,
    )(*args)


FF_CHUNK = 1024
FFN_ROWS = 1024


def _residual_ffn(x, mix, mod_ref, g_ref, w1_ref, w2_ref):
    def mod(k):
        return mod_ref[0, :, k * D_MODEL:(k + 1) * D_MODEL]

    x1 = x + mod(2) * mix
    hn = _rms_mod(x1, g_ref[...], mod(3), mod(4)).astype(BF16)
    acc = None
    for c in range(D_FF // FF_CHUNK):
        hc = _dot(hn, w1_ref[:, c * FF_CHUNK:(c + 1) * FF_CHUNK])
        hc = jnp.square(jnp.maximum(hc, 0.0)).astype(BF16)
        part = _dot(hc, w2_ref[c * FF_CHUNK:(c + 1) * FF_CHUNK, :])
        acc = part if acc is None else acc + part
    return x1 + mod(5) * acc


def _ffn_kernel(*refs, n_y, final_norm):
    x_ref = refs[0]
    y_refs = refs[1:1 + n_y]
    mod_ref = refs[1 + n_y]
    wo_refs = refs[2 + n_y:2 + 2 * n_y]
    g_ref, w1_ref, w2_ref = refs[2 + 2 * n_y:5 + 2 * n_y]
    rest = refs[5 + 2 * n_y:]
    if final_norm:
        gf_ref, o_ref = rest
    else:
        (o_ref,) = rest

    mix = _dot(y_refs[0][...], wo_refs[0][...])
    for k in range(1, n_y):
        mix = mix + _dot(y_refs[k][...], wo_refs[k][...])
    x2 = _residual_ffn(x_ref[...], mix, mod_ref, g_ref, w1_ref, w2_ref)
    if final_norm:
        ms = jnp.mean(x2 * x2, axis=-1, keepdims=True)
        x2 = x2 * lax.rsqrt(ms + EPS) * gf_ref[...]
    o_ref[...] = x2


def _ffn_call(x2d, ys, mod3, wos, g, w1, w2, g_final, tm):
    t = x2d.shape[0]
    nb = mod3.shape[0]
    rows_per_b = t // nb
    n_y = len(ys)
    final_norm = g_final is not None

    def const(shape):
        return pl.BlockSpec(shape, lambda i: (0, 0), pipeline_mode=pl.Buffered(1))

    in_specs = [pl.BlockSpec((tm, D_MODEL), lambda i: (i, 0))]
    in_specs += [pl.BlockSpec((tm, y.shape[1]), lambda i: (i, 0)) for y in ys]
    in_specs += [pl.BlockSpec((1, 1, 6 * D_MODEL), lambda i: ((i * tm) // rows_per_b, 0, 0))]
    in_specs += [const(w.shape) for w in wos]
    in_specs += [const((1, D_MODEL)), const((D_MODEL, D_FF)), const((D_FF, D_MODEL))]
    args = [x2d, *ys, mod3, *wos, g.reshape(1, D_MODEL), w1, w2]
    if final_norm:
        in_specs.append(const((1, D_MODEL)))
        args.append(g_final.reshape(1, D_MODEL))
    return pl.pallas_call(
        functools.partial(_ffn_kernel, n_y=n_y, final_norm=final_norm),
        grid=(t // tm,),
        in_specs=in_specs,
        out_specs=pl.BlockSpec((tm, D_MODEL), lambda i: (i, 0)),
        out_shape=jax.ShapeDtypeStruct((t, D_MODEL), F32),
        compiler_params=_cparams(("arbitrary",)),
        name="mix_out_ffn",
    )(*args)


def _rope(x, cos, sin):
    return x * cos + pltpu.roll(x, 8, axis=1) * sin


def _mla_proj_kernel(*refs, rope, emit_cache):
    x_ref, mod_ref, g_ref, win_ref, gq_ref, wqb_ref, gkv_ref, wk_ref, wv_ref, vones_ref, pemask_ref = refs[:11]
    rest = refs[11:]
    if rope:
        cos_ref, sin_ref = rest[:2]
        rest = rest[2:]
    q_ref, k_ref, v_ref = rest[:3]
    if emit_cache:
        ckv_ref, kpe_ref = rest[3:]

    shift = mod_ref[0, :, 0:D_MODEL]
    scale = mod_ref[0, :, D_MODEL:2 * D_MODEL]
    h = _rms_mod(x_ref[...], g_ref[...], shift, scale).astype(BF16)
    z = _dot(h, win_ref[...])
    cq = z[:, :Q_RANK]
    ckv = z[:, Q_RANK:Q_RANK + KV_RANK]
    kpe = z[:, Q_RANK + KV_RANK:]
    cqn = cq * lax.rsqrt(jnp.mean(cq * cq, axis=-1, keepdims=True) + EPS) * gq_ref[...]
    ckvn = ckv * lax.rsqrt(jnp.mean(ckv * ckv, axis=-1, keepdims=True) + EPS) * gkv_ref[...]
    if emit_cache:
        ckv_ref[...] = ckvn
        kpe_ref[...] = kpe
    q = _dot(cqn.astype(BF16), wqb_ref[...])
    ckvb = ckvn.astype(BF16)
    kn = _dot(ckvb, wk_ref[...])
    v_ref[...] = (_dot(ckvb, wv_ref[...]) + vones_ref[...]).astype(BF16)
    if rope:
        cos, sin = cos_ref[...], sin_ref[...]
        kpe = _rope(kpe, cos, sin)
    else:
        kpe = kpe * pemask_ref[...]
    for hd in range(N_HEADS):
        sl = slice(hd * HEAD_PAD, (hd + 1) * HEAD_PAD)
        qh = q[:, sl]
        if rope:
            qh = _rope(qh, cos, sin)
        q_ref[:, sl] = qh.astype(BF16)
        k_ref[:, sl] = (kn[:, sl] + kpe).astype(BF16)


def _mla_proj_call(x2d, mod3, prm, tables, *, seq_len, emit_cache, tm):
    t = x2d.shape[0]
    nb = mod3.shape[0]
    rows_per_b = t // nb
    rope = tables is not None
    tiles_per_seq = seq_len // tm
    hw = N_HEADS * HEAD_PAD

    def const(shape):
        return pl.BlockSpec(shape, lambda i: (0, 0))

    def rows(n):
        return pl.BlockSpec((tm, n), lambda i: (i, 0))

    in_specs = [rows(D_MODEL),
                pl.BlockSpec((1, 1, 6 * D_MODEL), lambda i: ((i * tm) // rows_per_b, 0, 0)),
                const((1, D_MODEL)), const(prm["w_in"].shape), const((1, Q_RANK)), const(prm["w_qb"].shape),
                const((1, KV_RANK)), const(prm["w_k"].shape), const(prm["w_v"].shape), const((1, hw)),
                const((1, LANES))]
    args = [x2d, mod3, prm["g_mix"], prm["w_in"], prm["g_q"], prm["w_qb"], prm["g_kv"], prm["w_k"], prm["w_v"],
            prm["v_ones"], prm["pe_mask"]]
    if rope:
        in_specs += [pl.BlockSpec((tm, LANES), lambda i: (i % tiles_per_seq, 0))] * 2
        args += list(tables)
    out_specs = [rows(hw), rows(hw), rows(hw)]
    out_shape = [jax.ShapeDtypeStruct((t, hw), BF16)] * 3
    if emit_cache:
        out_specs += [rows(KV_RANK), rows(LANES)]
        out_shape += [jax.ShapeDtypeStruct((t, KV_RANK), F32), jax.ShapeDtypeStruct((t, LANES), F32)]
    return pl.pallas_call(
        functools.partial(_mla_proj_kernel, rope=rope, emit_cache=emit_cache),
        grid=(t // tm,),
        in_specs=in_specs,
        out_specs=out_specs,
        out_shape=out_shape,
        compiler_params=_cparams(("arbitrary",)),
        name="l1_mla_proj",
    )(*args)


def _ctx_expand_kernel(ckv_ref, kpe_ref, wk_ref, wv_ref, vones_ref, k_ref, v_ref):
    ckvb = ckv_ref[...].astype(BF16)
    kn = _dot(ckvb, wk_ref[...])
    v_ref[...] = (_dot(ckvb, wv_ref[...]) + vones_ref[...]).astype(BF16)
    kpe = kpe_ref[...]
    for hd in range(N_HEADS):
        sl = slice(hd * HEAD_PAD, (hd + 1) * HEAD_PAD)
        k_ref[:, sl] = (kn[:, sl] + kpe).astype(BF16)


def _ctx_expand_call(ckv2d, kpe2d, prm, tm):
    hw = N_HEADS * HEAD_PAD
    t = ckv2d.shape[0]
    const = lambda a: pl.BlockSpec(a.shape, lambda i: (0, 0))
    rows = lambda n: pl.BlockSpec((tm, n), lambda i: (i, 0))
    return pl.pallas_call(
        _ctx_expand_kernel,
        grid=(t // tm,),
        in_specs=[rows(KV_RANK), rows(LANES), const(prm["w_k"]), const(prm["w_v"]), const(prm["v_ones"])],
        out_specs=[rows(hw), rows(hw)],
        out_shape=[jax.ShapeDtypeStruct((t, hw), BF16), jax.ShapeDtypeStruct((t, hw), BF16)],
        compiler_params=_cparams(("arbitrary",)),
        name="l1_ctx_expand",
    )(ckv2d, kpe2d, prm["w_k"], prm["w_v"], prm["v_ones"])


Q_PRESCALE = (QK_NOPE + QK_ROPE) ** -0.5 * math.log2(math.e)
_NT = (((1,), (1,)), ((), ()))
ROW_SUB = 256


def _pair_out(acc_even, acc_odd):
    lane = lax.broadcasted_iota(jnp.int32, acc_even.shape, 1)
    first = lane < V_HEAD
    num = jnp.where(first, acc_even, acc_odd)
    den = jnp.where(first, pltpu.roll(acc_even, V_HEAD, axis=1), pltpu.roll(acc_odd, V_HEAD, axis=1))
    return (num / den).astype(BF16)


def _attn_ctx_kernel(q_ref, k_ref, v_ref, o_ref, s_scr, p_scr, *, pairs):
    heads = [slice(h * HEAD_PAD, (h + 1) * HEAD_PAD) for h in range(2 * pairs)]
    for h, sl in enumerate(heads):
        s_scr[h] = lax.dot_general(q_ref[0, :, sl], k_ref[0, :, sl], _NT, preferred_element_type=F32)
    for h in range(2 * pairs):
        s = s_scr[h]
        p_scr[h] = jnp.exp2(s - s.max(axis=-1, keepdims=True)).astype(BF16)
    for p in range(pairs):
        accs = [_dot(p_scr[2 * p + e], v_ref[0, :, heads[2 * p + e]]) for e in range(2)]
        o_ref[0, :, p * LANES:(p + 1) * LANES] = _pair_out(accs[0], accs[1])


def _attn_ctx_call(q3, k3, v3):
    b, s, _ = q3.shape
    spec = lambda w: pl.BlockSpec((1, s, w), lambda bi: (bi, 0, 0))
    hw = N_HEADS * HEAD_PAD
    return pl.pallas_call(
        functools.partial(_attn_ctx_kernel, pairs=N_HEADS // 2),
        grid=(b,),
        in_specs=[spec(hw), spec(hw), spec(hw)],
        out_specs=spec(N_HEADS * V_HEAD),
        out_shape=jax.ShapeDtypeStruct((b, s, N_HEADS * V_HEAD), BF16),
        scratch_shapes=[pltpu.VMEM((N_HEADS, s, s), F32), pltpu.VMEM((N_HEADS, s, s), BF16)],
        compiler_params=_cparams(("arbitrary",)),
        name="l1_attention_ctx",
    )(q3, k3, v3)


def _attn_lat_kernel(q_ref, k1_ref, k2_ref, v1c_ref, v2c_ref, v1l_ref, v2l_ref, o_ref,
                     s0, s1, m0, m1, p0, p1, acc0_save):
    n = pl.program_id(0)
    n1, n2 = k1_ref.shape[1], k2_ref.shape[1]
    h0 = slice(0, HEAD_PAD)
    h1 = slice(HEAD_PAD, 2 * HEAD_PAD)

    @pl.when(n == 0)
    def _():
        s1[...] = jnp.zeros(s1.shape, F32)
        m1[...] = jnp.zeros(m1.shape, F32)
        acc0_save[...] = jnp.ones(acc0_save.shape, F32)

    row_blocks = [slice(r, r + ROW_SUB) for r in range(0, q_ref.shape[1], ROW_SUB)]

    def scores(sl, s_w, m_w):
        for rows in row_blocks:
            q = q_ref[0, rows, sl]
            sa = lax.dot_general(q, k1_ref[0, :, sl], _NT, preferred_element_type=F32)
            sb = lax.dot_general(q, k2_ref[0, :, sl], _NT, preferred_element_type=F32)
            s_w[rows, 0:n1] = sa
            s_w[rows, n1:] = sb
            m = jnp.maximum(sa.max(axis=-1, keepdims=True), sb.max(axis=-1, keepdims=True))
            m_w[rows, :] = jnp.broadcast_to(m, (ROW_SUB, LANES))

    def values(sl, s_r, m_r, p_scr, va_ref, vb_ref):
        for rows in row_blocks:
            m = m_r[rows, :]
            for t in range((n1 + n2) // LANES):
                tl = slice(t * LANES, (t + 1) * LANES)
                p_scr[rows, tl] = jnp.exp2(s_r[rows, tl] - m).astype(BF16)
        return _dot(p_scr[:, 0:n1], va_ref[0, :, sl]) + _dot(p_scr[:, n1:], vb_ref[0, :, sl])

    @pl.when(n >= 0)
    def _():
        scores(h0, s0, m0)
        acc1 = values(h1, s1, m1, p1, v1l_ref, v2l_ref)
        o_ref[0] = _pair_out(acc0_save[...], acc1)

    @pl.when(n < pl.num_programs(0))
    def _():
        scores(h1, s1, m1)
        acc0_save[...] = values(h0, s0, m0, p0, v1c_ref, v2c_ref)


def _attn_lat_call(q3, k1, k2, v1, v2, *, tq):
    b, s, _ = q3.shape
    s1, s2 = k1.shape[1], k2.shape[1]
    sk = s1 + s2
    npair = N_HEADS // 2
    nq = s // tq
    n_items = b * npair * nq

    def item(n):
        return n // (npair * nq), n % nq, (n // nq) % npair

    def cur(n):
        return item(jnp.minimum(n, n_items - 1))

    def lag(n):
        return item(jnp.maximum(n - 1, 0))

    qw = 2 * HEAD_PAD
    kv_cur = lambda rows: pl.BlockSpec((1, rows, qw), lambda n: (cur(n)[0], 0, cur(n)[2]))
    kv_lag = lambda rows: pl.BlockSpec((1, rows, qw), lambda n: (lag(n)[0], 0, lag(n)[2]))
    in_specs = [pl.BlockSpec((1, tq, qw), lambda n: cur(n)),
                kv_cur(s1), kv_cur(s2), kv_cur(s1), kv_cur(s2), kv_lag(s1), kv_lag(s2)]
    return pl.pallas_call(
        _attn_lat_kernel,
        grid=(n_items + 1,),
        in_specs=in_specs,
        out_specs=pl.BlockSpec((1, tq, LANES), lambda n: lag(n)),
        out_shape=jax.ShapeDtypeStruct((b, s, N_HEADS * V_HEAD), BF16),
        scratch_shapes=[pltpu.VMEM((tq, sk), F32)] * 2 + [pltpu.VMEM((tq, LANES), F32)] * 2
        + [pltpu.VMEM((tq, sk), BF16)] * 2 + [pltpu.VMEM((tq, LANES), F32)],
        compiler_params=_cparams(("arbitrary",)),
        name="l1_attention_lat",
    )(q3, k1, k2, v1, v2, v1, v2)


def _block_diag_halves(w):
    nb = RNN_BLOCKS // 2
    w4 = w.astype(BF16).reshape(2, nb, RNN_BLOCK, RNN_BLOCK)
    eye = jnp.eye(nb, dtype=BF16)
    return (w4[:, :, :, None, :] * eye[None, :, None, :, None]).reshape(2, nb * RNN_BLOCK, nb * RNN_BLOCK)


def _rope_tables(seq_len):
    rows = seq_len // GRID_W
    row = np.repeat(np.arange(rows, dtype=np.float64), GRID_W)
    col = np.tile(np.arange(GRID_W, dtype=np.float64), rows)
    half = QK_ROPE // 2
    inv = ROPE_BASE ** (-np.arange(0, half, 2, dtype=np.float64) / half)
    ang_r, ang_c = row[:, None] * inv, col[:, None] * inv
    zeros = np.zeros((seq_len, 8))
    cr, sr, cc, sc = np.cos(ang_r), np.sin(ang_r), np.cos(ang_c), np.sin(ang_c)
    lead = np.ones((seq_len, QK_NOPE))
    tail = np.zeros((seq_len, HEAD_PAD - QK_NOPE - PE_WIDTH))
    cos = np.concatenate([lead, zeros, cr, cr, zeros, cc, cc, tail], axis=1)
    sin = np.concatenate([0 * lead, zeros, -sr, sr, zeros, -sc, sc, tail], axis=1)
    return tuple(jnp.asarray(t, dtype=F32) for t in (cos, sin))


def _pe_layout_cols(w):
    x1r, x2r, x1c, x2c = (w[..., 8 * i:8 * (i + 1)] for i in range(4))
    return jnp.concatenate([x2r, x1r, x2r, x2c, x1c, x2c], axis=-1)


def kernel(x_prompt, x_sample, state_l0_lru, cache_l1_ckv, cache_l1_kpe, c, c_ctx, l0_w_mod, l0_b_mod, l0_g_mix, l0_g_ffn, l0_w_in, l0_conv_w, l0_conv_b, l0_lru_w_a, l0_lru_b_a, l0_lru_w_i, l0_lru_b_i, l0_lru_lam, l0_pool_w, l0_pool_scale, l0_w_out, l0_ffn_w1, l0_ffn_w2, l1_w_mod, l1_b_mod, l1_g_mix, l1_g_ffn, l1_w_in, l1_g_q, l1_w_qb, l1_g_kv, l1_w_kvb, l1_w_out, l1_ffn_w1, l1_ffn_w2, g_final):
    bp, sp, d = x_prompt.shape
    bs, ss, _ = x_sample.shape
    past = cache_l1_ckv.shape[1]
    xp = x_prompt.reshape(bp * sp, d)
    xs = x_sample.reshape(bs * ss, d)

    cond8 = jnp.concatenate([c, c_ctx[None, :], jnp.zeros((SUBLANES - bs - 1, d), F32)], axis=0)
    m0 = _mod_call(cond8, l0_w_mod, l0_b_mod)
    m1 = _mod_call(cond8, l1_w_mod, l1_b_mod)
    mod_p = [m[bs:bs + 1].reshape(1, 1, 6 * d) for m in (m0, m1)]
    mod_s = [m[0:bs].reshape(bs, 1, 6 * d) for m in (m0, m1)]

    lru = dict(
        conv_w=l0_conv_w, conv_b=l0_conv_b.reshape(1, D_RNN),
        wa=[_block_diag_halves(l0_lru_w_a[i]) for i in range(2)],
        wi=[_block_diag_halves(l0_lru_w_i[i]) for i in range(2)],
        ba=[l0_lru_b_a[i].reshape(1, D_RNN) for i in range(2)],
        bi=[l0_lru_b_i[i].reshape(1, D_RNN) for i in range(2)],
        lam=[l0_lru_lam[i].reshape(1, D_RNN) for i in range(2)],
        pool_w=l0_pool_w.astype(BF16), pool_scale=l0_pool_scale.reshape(1, D_POOL),
        g_mix=l0_g_mix.reshape(1, d), w_in=l0_w_in.astype(BF16))
    wo0 = l0_w_out.astype(BF16)
    wos0 = [wo0[:D_RNN], wo0[D_RNN:]]
    w1_0, w2_0 = l0_ffn_w1.astype(BF16), l0_ffn_w2.astype(BF16)

    def layer0(x2d, mod3, nseq, seq_len, h0f, h0b, ts):
        x3 = x2d.reshape(nseq, seq_len, d)
        if ts == seq_len:
            yrnn, stf, ypool, stb = _lru_call(x3, mod3, h0f, None, lru, reverse=False, ts=ts, h0b=h0b)
        else:
            hf, stf, ypool, xc, gate = _lru_call(x3, mod3, h0f, None, lru, reverse=False, ts=ts)
            yrnn, stb = _lru_call(None, None, h0b, (hf, xc, gate), lru, reverse=True, ts=ts)
        ys = [yrnn.reshape(-1, D_RNN), ypool.reshape(-1, D_POOL)]
        x2 = _ffn_call(x2d, ys, mod3, wos0, l0_g_ffn, w1_0, w2_0, None, tm=FFN_ROWS)
        return x2, stf, stb

    zero_st = jnp.zeros((bp, 1, D_RNN), F32)
    xp, stf, stb = layer0(xp, mod_p[0], bp, sp, zero_st, zero_st, ts=min(sp, SEQ_TILE))
    xs, _, _ = layer0(xs, mod_s[0], bs, ss, state_l0_lru[:, 0:1], state_l0_lru[:, 1:2], ts=min(ss, SEQ_TILE))
    new_lru = jnp.concatenate([stf, stb], axis=1)

    pad_pe = HEAD_PAD - QK_NOPE - PE_WIDTH
    w_in1b = l1_w_in.astype(BF16)
    w_in1 = jnp.concatenate([w_in1b[:, :Q_RANK + KV_RANK], jnp.zeros((d, QK_NOPE), BF16),
                             _pe_layout_cols(w_in1b[:, Q_RANK + KV_RANK:]), jnp.zeros((d, pad_pe), BF16)], axis=1)
    wqb = l1_w_qb.astype(BF16).reshape(Q_RANK, N_HEADS, QK_NOPE + QK_ROPE)
    wqb = jnp.concatenate([wqb[:, :, :QK_NOPE], _pe_layout_cols(wqb[:, :, QK_NOPE:]),
                           jnp.zeros((Q_RANK, N_HEADS, pad_pe), BF16)], axis=2).reshape(Q_RANK, N_HEADS * HEAD_PAD)
    pe_mask = jnp.concatenate([jnp.ones((QK_NOPE,), F32)] + [m * jnp.ones((8,), F32) for m in (0, 1, 1, 0, 1, 1)]
                              + [jnp.zeros((pad_pe,), F32)]).reshape(1, HEAD_PAD)
    wkvb = l1_w_kvb.astype(BF16).reshape(KV_RANK, N_HEADS, QK_NOPE + V_HEAD)
    w_k = jnp.pad(wkvb[:, :, :QK_NOPE], ((0, 0), (0, 0), (0, HEAD_PAD - QK_NOPE)))
    w_k = w_k.reshape(KV_RANK, N_HEADS * HEAD_PAD)
    wv = wkvb[:, :, QK_NOPE:].reshape(KV_RANK, N_HEADS // 2, 2, V_HEAD)
    zv = jnp.zeros((KV_RANK, N_HEADS // 2, V_HEAD), BF16)
    w_v = jnp.stack([wv[:, :, 0], zv, zv, wv[:, :, 1]], axis=2).reshape(KV_RANK, N_HEADS * HEAD_PAD)
    one = jnp.ones((N_HEADS // 2, V_HEAD), F32)
    v_ones = jnp.stack([0 * one, one, one, 0 * one], axis=1).reshape(1, N_HEADS * HEAD_PAD)
    mla = dict(g_mix=l1_g_mix.reshape(1, d), w_in=w_in1, g_q=l1_g_q.reshape(1, Q_RANK) * Q_PRESCALE, w_qb=wqb,
               g_kv=l1_g_kv.reshape(1, KV_RANK), w_k=w_k, w_v=w_v, v_ones=v_ones, pe_mask=pe_mask)
    wo1 = [l1_w_out.astype(BF16)]
    w1_1, w2_1 = l1_ffn_w1.astype(BF16), l1_ffn_w2.astype(BF16)

    qp, kp, vp, ckv_new, kpe_new = _mla_proj_call(xp, mod_p[1], mla, None, seq_len=bp * sp, emit_cache=True,
                                                  tm=SEQ_TILE)
    op = _attn_ctx_call(qp.reshape(bp, sp, -1), kp.reshape(bp, sp, -1), vp.reshape(bp, sp, -1))
    y_prompt = _ffn_call(xp, [op.reshape(bp * sp, -1)], mod_p[1], wo1, l1_g_ffn, w1_1, w2_1, g_final, tm=FFN_ROWS)

    tables = _rope_tables(ss)
    qs, ks, vs = _mla_proj_call(xs, mod_s[1], mla, tables, seq_len=ss, emit_cache=False, tm=SEQ_TILE)
    kpe_c = cache_l1_kpe.reshape(bs * past, QK_ROPE)
    gap = jnp.zeros((bs * past, 8), F32)
    kpe_ctx = jnp.concatenate([jnp.zeros((bs * past, QK_NOPE), F32), gap, kpe_c[:, :16], gap, kpe_c[:, 16:],
                               jnp.zeros((bs * past, pad_pe), F32)], axis=1)
    kc, vc = _ctx_expand_call(cache_l1_ckv.reshape(bs * past, KV_RANK), kpe_ctx, mla, tm=past)
    os_ = _attn_lat_call(qs.reshape(bs, ss, -1), ks.reshape(bs, ss, -1), kc.reshape(bs, past, -1),
                         vs.reshape(bs, ss, -1), vc.reshape(bs, past, -1), tq=SEQ_TILE)
    y_sample = _ffn_call(xs, [os_.reshape(bs * ss, -1)], mod_s[1], wo1, l1_g_ffn, w1_1, w2_1, g_final, tm=FFN_ROWS)

    new_ckv = ckv_new.reshape(bp, sp, KV_RANK)
    new_kpe = jnp.concatenate([kpe_new[:, QK_NOPE + 8:QK_NOPE + 24], kpe_new[:, QK_NOPE + 32:QK_NOPE + 48]],
                              axis=1).reshape(bp, sp, QK_ROPE)
    return (y_prompt.reshape(bp, sp, d), y_sample.reshape(bs, ss, d), new_lru, new_ckv, new_kpe)
```

```python
import functools
import math

import jax
import jax.numpy as jnp
import numpy as np
from jax import lax
from jax.experimental import pallas as pl
from jax.experimental.pallas import tpu as pltpu

F32 = jnp.float32
BF16 = jnp.bfloat16

D_MODEL = 1024
D_FF = 4 * D_MODEL
EPS = 1e-6
D_RNN = 512
RNN_BLOCKS = 8
RNN_BLOCK = 64
CONV_W = 4
LRU_C = 8.0
D_POOL = 512
POOL_WINDOWS = (2, 4, 8, 16)
POOL_GROUP = 128
N_HEADS = 16
QK_NOPE = 64
QK_ROPE = 32
V_HEAD = 64
Q_RANK = 384
KV_RANK = 256
ROPE_BASE = 10000.0
GRID_W = 64

LANES = 128
SUBLANES = 8
HEAD_PAD = 128
PE_WIDTH = 48
HALO = 8
VMEM_LIMIT = 52 * 1024 * 1024
SEQ_TILE = 512


def _cparams(sem):
    return pltpu.CompilerParams(dimension_semantics=sem, vmem_limit_bytes=VMEM_LIMIT)


def _dot(a, b):
    return jnp.dot(a, b, preferred_element_type=F32)


def _sigmoid(x):
    return 0.5 * jnp.tanh(0.5 * x) + 0.5


def _rms_mod(x, g, shift, scale):
    ms = jnp.mean(x * x, axis=-1, keepdims=True)
    y = x * lax.rsqrt(ms + EPS) * g
    return y * (1.0 + scale) + shift


def _mod_kernel(c_ref, w_ref, b_ref, o_ref):
    c = c_ref[...]
    s = c * jax.nn.sigmoid(c)
    o_ref[...] = _dot(s.astype(BF16), w_ref[...].astype(BF16)) + b_ref[...]


def _mod_call(cond8, w_mod, b_mod):
    n = w_mod.shape[1]
    tn = 1024
    return pl.pallas_call(
        _mod_kernel,
        grid=(n // tn,),
        in_specs=[
            pl.BlockSpec((SUBLANES, D_MODEL), lambda j: (0, 0)),
            pl.BlockSpec((D_MODEL, tn), lambda j: (0, j)),
            pl.BlockSpec((1, tn), lambda j: (0, j)),
        ],
        out_specs=pl.BlockSpec((SUBLANES, tn), lambda j: (0, j)),
        out_shape=jax.ShapeDtypeStruct((SUBLANES, n), F32),
        compiler_params=_cparams(("arbitrary",)),
        name="adaln_mod",
    )(cond8, w_mod, b_mod.reshape(1, n))


def _lru_coeffs(xc, wa_ref, ba_ref, wi_ref, bi_ref, lam_ref):
    xcb = xc.astype(BF16)
    half = D_RNN // 2
    ga = jnp.concatenate([_dot(xcb[:, :half], wa_ref[0]), _dot(xcb[:, half:], wa_ref[1])], axis=-1)
    gi = jnp.concatenate([_dot(xcb[:, :half], wi_ref[0]), _dot(xcb[:, half:], wi_ref[1])], axis=-1)
    r = _sigmoid(ga + ba_ref[...])
    i = _sigmoid(gi + bi_ref[...])
    log_a = (LRU_C * r) * jax.nn.log_sigmoid(lam_ref[...])
    a = jnp.exp(log_a)
    v = -jnp.tanh(log_a) * (a * a + 1.0)
    u = jnp.where(v > 0.0, v * lax.rsqrt(v), 0.0) * (i * xc)
    return a, u


def _scan_tile(a, u, carry_row, reverse, a_s, u_s, ts):
    g = ts // SUBLANES
    a3 = a.reshape(g, SUBLANES, D_RNN)
    u3 = u.reshape(g, SUBLANES, D_RNN)
    row = lax.broadcasted_iota(jnp.int32, (g, SUBLANES, D_RNN), 1)
    for d in (1, 2, 4):
        if reverse:
            sh = SUBLANES - d
            m = row < SUBLANES - d
        else:
            sh = d
            m = row >= d
        a_sh = pltpu.roll(a3, sh, axis=1)
        u_sh = pltpu.roll(u3, sh, axis=1)
        u3 = u3 + a3 * jnp.where(m, u_sh, 0.0)
        a3 = a3 * jnp.where(m, a_sh, 1.0)
    a_s[...] = a3.reshape(ts, D_RNN)
    u_s[...] = u3.reshape(ts, D_RNN)
    edge = 0 if reverse else SUBLANES - 1

    def body(k, c):
        gi = (g - 1 - k) if reverse else k
        r0 = pl.multiple_of(gi * SUBLANES, SUBLANES)
        h = u_s[pl.ds(r0, SUBLANES), :] + a_s[pl.ds(r0, SUBLANES), :] * c
        u_s[pl.ds(r0, SUBLANES), :] = h
        return jnp.broadcast_to(h[edge:edge + 1, :], (SUBLANES, D_RNN))

    c0 = jnp.broadcast_to(carry_row, (SUBLANES, D_RNN))
    c_last = lax.fori_loop(0, g, body, c0, unroll=4)
    return c_last[0:1, :]


def _lru_kernel(*refs, reverse, with_pool, both, ts, nt, seq_len):
    if both:
        (x_ref, xprev_ref, xnext_ref, mod_ref, g_ref, win_ref, h0_ref, cw_ref, cb_ref, wa_ref, ba_ref, wi_ref,
         bi_ref, lam_ref, pw_ref, ps_ref, h0b_ref, wab_ref, bab_ref, wib_ref, bib_ref, lamb_ref,
         y_ref, st_ref, yp_ref, stb_ref,
         xe_s, ext_s, a_s, u_s, carry_s, b2_s, b4_s, b8_s, ab_s, ub_s) = refs
    elif with_pool:
        (x_ref, xprev_ref, xnext_ref, mod_ref, g_ref, win_ref, h0_ref, cw_ref, cb_ref, wa_ref, ba_ref, wi_ref,
         bi_ref, lam_ref, pw_ref, ps_ref, h_ref, st_ref, yp_ref, xc_ref, gate_out_ref,
         xe_s, ext_s, a_s, u_s, carry_s, b2_s, b4_s, b8_s) = refs
    else:
        (xc_in_ref, gate_ref, hf_ref, h0_ref, wa_ref, ba_ref, wi_ref,
         bi_ref, lam_ref, y_ref, st_ref, a_s, u_s, carry_s) = refs
    j = pl.program_id(1)
    t = (nt - 1 - j) if reverse else j
    n2 = ts + 2 * HALO

    @pl.when(j == 0)
    def _():
        carry_s[...] = h0_ref[0]

    if with_pool:
        xe_s[0:HALO, :] = xprev_ref[0]
        xe_s[HALO:HALO + ts, :] = x_ref[0]
        xe_s[HALO + ts:n2, :] = xnext_ref[0]
        shift = mod_ref[0, :, 0:D_MODEL]
        scale = mod_ref[0, :, D_MODEL:2 * D_MODEL]
        hx = _rms_mod(xe_s[...], g_ref[...], shift, scale).astype(BF16)
        z = _dot(hx, win_ref[...])
        if not both:
            gate_out_ref[0] = z[HALO:HALO + ts, D_RNN:2 * D_RNN]

        def fill_ext(cols):
            ext_s[0:HALO, :] = jnp.where(t > 0, z[0:HALO, cols], 0.0)
            ext_s[HALO:HALO + ts, :] = z[HALO:HALO + ts, cols]
            ext_s[HALO + ts:n2, :] = jnp.where(t < nt - 1, z[HALO + ts:n2, cols], 0.0)

        fill_ext(slice(0, D_RNN))
        xc = cb_ref[...]
        for k in range(CONV_W):
            off = HALO - CONV_W // 2 + k
            xc = xc + ext_s[off:off + ts, :] * cw_ref[k:k + 1, :]
        if not both:
            xc_ref[0] = xc
        fill_ext(slice(2 * D_RNN, 3 * D_RNN))
    else:
        xc = xc_in_ref[0]

    a, u = _lru_coeffs(xc, wa_ref, ba_ref, wi_ref, bi_ref, lam_ref)
    c_last = _scan_tile(a, u, carry_s[...], reverse, a_s, u_s, ts)
    carry_s[...] = c_last
    st_ref[0] = c_last
    h = u_s[...]

    if not with_pool:
        y_ref[0] = ((hf_ref[0] + h) * jax.nn.gelu(gate_ref[0])).astype(BF16)
        return

    if both:
        ab, ub = _lru_coeffs(xc, wab_ref, bab_ref, wib_ref, bib_ref, lamb_ref)
        stb_ref[0] = _scan_tile(ab, ub, h0b_ref[0], True, ab_s, ub_s, ts)
        gate = z[HALO:HALO + ts, D_RNN:2 * D_RNN]
        y_ref[0] = ((h + ub_s[...]) * jax.nn.gelu(gate)).astype(BF16)
    else:
        h_ref[0] = h

    b2_s[1:n2, :] = ext_s[1:n2, :] + ext_s[0:n2 - 1, :]
    b4_s[2:n2 - 1, :] = b2_s[3:n2, LANES:] + b2_s[1:n2 - 2, LANES:]
    b8_s[4:n2 - 3, :] = b4_s[6:n2 - 1, LANES:] + b4_s[2:n2 - 5, LANES:]
    s16 = b8_s[HALO + 4:HALO + 4 + ts, LANES:] + b8_s[HALO - 4:HALO - 4 + ts, LANES:]
    sums = (b2_s[HALO:HALO + ts, 0:LANES], b4_s[HALO:HALO + ts, 0:LANES], b8_s[HALO:HALO + ts, 0:LANES], s16)
    before = (t * ts + lax.broadcasted_iota(jnp.int32, (ts, LANES), 0)).astype(F32)
    after = float(seq_len - 1) - before
    for gidx, w in enumerate(POOL_WINDOWS):
        left = w // 2
        right = w - 1 - left
        cnt = jnp.minimum(before, float(left)) + (jnp.minimum(after, float(right)) + 1.0)
        xg = ext_s[HALO:HALO + ts, gidx * LANES:(gidx + 1) * LANES]
        dlt = sums[gidx] / cnt - xg
        yg = _dot(dlt.astype(BF16), pw_ref[gidx]) * ps_ref[:, gidx * LANES:(gidx + 1) * LANES]
        yp_ref[0, :, gidx * LANES:(gidx + 1) * LANES] = yg.astype(BF16)


def _lru_call(x3, mod3, h0, fwd, prm, *, reverse, ts, h0b=None):
    nseq, seq_len = (x3 if not reverse else fwd[0]).shape[:2]
    nt = seq_len // ts
    tb = ts // HALO
    nhb = seq_len // HALO
    d = 1 if reverse else 0
    with_pool = not reverse
    both = h0b is not None
    assert not both or (with_pool and nt == 1)

    def tpos(j):
        return (nt - 1 - j) if reverse else j

    def const(shape):
        nd = len(shape)
        return pl.BlockSpec(shape, lambda s, j: (0,) * nd)

    tile_out = pl.BlockSpec((1, ts, D_RNN), lambda s, j: (s, tpos(j), 0))
    st_spec = pl.BlockSpec((1, 1, D_RNN), lambda s, j: (s, 0, 0))
    conv_specs = [const((CONV_W, D_RNN)), const((1, D_RNN))]
    conv_args = [prm["conv_w"], prm["conv_b"]]
    w_specs = [const((2, 256, 256)), const((1, D_RNN)), const((2, 256, 256)), const((1, D_RNN)), const((1, D_RNN))]
    w_args = [prm["wa"][d], prm["ba"][d], prm["wi"][d], prm["bi"][d], prm["lam"][d]]
    scratch = [pltpu.VMEM((ts, D_RNN), F32), pltpu.VMEM((ts, D_RNN), F32), pltpu.VMEM((1, D_RNN), F32)]
    kern = functools.partial(_lru_kernel, reverse=reverse, with_pool=with_pool, both=both, ts=ts, nt=nt,
                             seq_len=seq_len)
    if with_pool:
        nb = mod3.shape[0]
        x_specs = [
            pl.BlockSpec((1, ts, D_MODEL), lambda s, j: (s, j, 0)),
            pl.BlockSpec((1, HALO, D_MODEL), lambda s, j: (s, jnp.maximum(j * tb - 1, 0), 0)),
            pl.BlockSpec((1, HALO, D_MODEL), lambda s, j: (s, jnp.minimum((j + 1) * tb, nhb - 1), 0)),
            pl.BlockSpec((1, 1, 6 * D_MODEL), lambda s, j: ((s * nb) // nseq, 0, 0)),
            const((1, D_MODEL)), const(prm["w_in"].shape)]
        in_specs = x_specs + [st_spec] + conv_specs + w_specs + [
            const((4, POOL_GROUP, POOL_GROUP)), const((1, D_POOL))]
        args = [x3, x3, x3, mod3, prm["g_mix"], prm["w_in"], h0] + conv_args + w_args + [
            prm["pool_w"], prm["pool_scale"]]
        out_specs = [tile_out, st_spec, tile_out, tile_out, tile_out]
        out_shape = [jax.ShapeDtypeStruct((nseq, seq_len, D_RNN), F32),
                     jax.ShapeDtypeStruct((nseq, 1, D_RNN), F32),
                     jax.ShapeDtypeStruct((nseq, seq_len, D_POOL), BF16),
                     jax.ShapeDtypeStruct((nseq, seq_len, D_RNN), F32),
                     jax.ShapeDtypeStruct((nseq, seq_len, D_RNN), F32)]
        n2 = ts + 2 * HALO
        scratch = [pltpu.VMEM((n2, D_MODEL), F32), pltpu.VMEM((n2, D_RNN), F32)] + scratch + [
            pltpu.VMEM((n2, D_POOL), F32), pltpu.VMEM((n2, D_POOL - LANES), F32),
            pltpu.VMEM((n2, D_POOL - 2 * LANES), F32)]
        name = "l0_in_lru_fwd_pool"
        if both:
            in_specs = in_specs + [st_spec] + w_specs
            args = args + [h0b, prm["wa"][1], prm["ba"][1], prm["wi"][1], prm["bi"][1], prm["lam"][1]]
            out_specs = [tile_out, st_spec, tile_out, st_spec]
            out_shape = [jax.ShapeDtypeStruct((nseq, seq_len, D_RNN), BF16), out_shape[1], out_shape[2],
                         out_shape[1]]
            scratch = scratch + [pltpu.VMEM((ts, D_RNN), F32), pltpu.VMEM((ts, D_RNN), F32)]
            name = "l0_in_lru_both_pool"
    else:
        hf, xc, gate = fwd
        in_specs = [tile_out, tile_out, tile_out, st_spec] + w_specs
        args = [xc, gate, hf, h0] + w_args
        out_specs = [tile_out, st_spec]
        out_shape = [jax.ShapeDtypeStruct((nseq, seq_len, D_RNN), BF16),
                     jax.ShapeDtypeStruct((nseq, 1, D_RNN), F32)]
        name = "l0_lru_bwd"
    return pl.pallas_call(
        kern,
        grid=(nseq, nt),
        in_specs=in_specs,
        out_specs=out_specs,
        out_shape=out_shape,
        scratch_shapes=scratch,
        compiler_params=_cparams(("arbitrary", "arbitrary")),
        name=name,
    )(*args)


FF_CHUNK = 1024
FFN_ROWS = 1024


def _residual_ffn(x, mix, mod_ref, g_ref, w1_ref, w2_ref):
    def mod(k):
        return mod_ref[0, :, k * D_MODEL:(k + 1) * D_MODEL]

    x1 = x + mod(2) * mix
    hn = _rms_mod(x1, g_ref[...], mod(3), mod(4)).astype(BF16)
    acc = None
    for c in range(D_FF // FF_CHUNK):
        hc = _dot(hn, w1_ref[:, c * FF_CHUNK:(c + 1) * FF_CHUNK])
        hc = jnp.square(jnp.maximum(hc, 0.0)).astype(BF16)
        part = _dot(hc, w2_ref[c * FF_CHUNK:(c + 1) * FF_CHUNK, :])
        acc = part if acc is None else acc + part
    return x1 + mod(5) * acc


def _ffn_kernel(*refs, n_y, final_norm):
    x_ref = refs[0]
    y_refs = refs[1:1 + n_y]
    mod_ref = refs[1 + n_y]
    wo_refs = refs[2 + n_y:2 + 2 * n_y]
    g_ref, w1_ref, w2_ref = refs[2 + 2 * n_y:5 + 2 * n_y]
    rest = refs[5 + 2 * n_y:]
    if final_norm:
        gf_ref, o_ref = rest
    else:
        (o_ref,) = rest

    mix = _dot(y_refs[0][...], wo_refs[0][...])
    for k in range(1, n_y):
        mix = mix + _dot(y_refs[k][...], wo_refs[k][...])
    x2 = _residual_ffn(x_ref[...], mix, mod_ref, g_ref, w1_ref, w2_ref)
    if final_norm:
        ms = jnp.mean(x2 * x2, axis=-1, keepdims=True)
        x2 = x2 * lax.rsqrt(ms + EPS) * gf_ref[...]
    o_ref[...] = x2


def _ffn_call(x2d, ys, mod3, wos, g, w1, w2, g_final, tm):
    t = x2d.shape[0]
    nb = mod3.shape[0]
    rows_per_b = t // nb
    n_y = len(ys)
    final_norm = g_final is not None

    def const(shape):
        return pl.BlockSpec(shape, lambda i: (0, 0), pipeline_mode=pl.Buffered(1))

    in_specs = [pl.BlockSpec((tm, D_MODEL), lambda i: (i, 0))]
    in_specs += [pl.BlockSpec((tm, y.shape[1]), lambda i: (i, 0)) for y in ys]
    in_specs += [pl.BlockSpec((1, 1, 6 * D_MODEL), lambda i: ((i * tm) // rows_per_b, 0, 0))]
    in_specs += [const(w.shape) for w in wos]
    in_specs += [const((1, D_MODEL)), const((D_MODEL, D_FF)), const((D_FF, D_MODEL))]
    args = [x2d, *ys, mod3, *wos, g.reshape(1, D_MODEL), w1, w2]
    if final_norm:
        in_specs.append(const((1, D_MODEL)))
        args.append(g_final.reshape(1, D_MODEL))
    return pl.pallas_call(
        functools.partial(_ffn_kernel, n_y=n_y, final_norm=final_norm),
        grid=(t // tm,),
        in_specs=in_specs,
        out_specs=pl.BlockSpec((tm, D_MODEL), lambda i: (i, 0)),
        out_shape=jax.ShapeDtypeStruct((t, D_MODEL), F32),
        compiler_params=_cparams(("arbitrary",)),
        name="mix_out_ffn",
    )(*args)


def _rope(x, cos, sin):
    return x * cos + pltpu.roll(x, 8, axis=1) * sin


def _mla_proj_kernel(*refs, rope, emit_cache, ctx_every):
    x_ref, mod_ref, g_ref, win_ref, gq_ref, wqb_ref, gkv_ref, wk_ref, wv_ref, vones_ref, pemask_ref = refs[:11]
    rest = refs[11:]
    if rope:
        cos_ref, sin_ref = rest[:2]
        rest = rest[2:]
    if ctx_every:
        cckv_ref, ckpe_ref = rest[:2]
        rest = rest[2:]
    q_ref, k_ref, v_ref = rest[:3]
    if emit_cache:
        ckv_ref, kpe_ref = rest[3:]

    def emit_kv(ckvn, kpe):
        ckvb = ckvn.astype(BF16)
        kn = _dot(ckvb, wk_ref[...])
        v_ref[...] = (_dot(ckvb, wv_ref[...]) + vones_ref[...]).astype(BF16)
        for hd in range(N_HEADS):
            sl = slice(hd * HEAD_PAD, (hd + 1) * HEAD_PAD)
            k_ref[:, sl] = (kn[:, sl] + kpe).astype(BF16)

    def project_tokens():
        shift = mod_ref[0, :, 0:D_MODEL]
        scale = mod_ref[0, :, D_MODEL:2 * D_MODEL]
        h = _rms_mod(x_ref[...], g_ref[...], shift, scale).astype(BF16)
        z = _dot(h, win_ref[...])
        cq = z[:, :Q_RANK]
        ckv = z[:, Q_RANK:Q_RANK + KV_RANK]
        kpe = z[:, Q_RANK + KV_RANK:]
        cqn = cq * lax.rsqrt(jnp.mean(cq * cq, axis=-1, keepdims=True) + EPS) * gq_ref[...]
        ckvn = ckv * lax.rsqrt(jnp.mean(ckv * ckv, axis=-1, keepdims=True) + EPS) * gkv_ref[...]
        if emit_cache:
            ckv_ref[...] = ckvn
            kpe_ref[...] = kpe
        q = _dot(cqn.astype(BF16), wqb_ref[...])
        if rope:
            cos, sin = cos_ref[...], sin_ref[...]
            kpe = _rope(kpe, cos, sin)
        else:
            kpe = kpe * pemask_ref[...]
        emit_kv(ckvn, kpe)
        for hd in range(N_HEADS):
            sl = slice(hd * HEAD_PAD, (hd + 1) * HEAD_PAD)
            qh = q[:, sl]
            if rope:
                qh = _rope(qh, cos, sin)
            q_ref[:, sl] = qh.astype(BF16)

    if ctx_every:
        is_ctx = pl.program_id(0) % ctx_every == ctx_every - 1

        @pl.when(is_ctx)
        def _():
            emit_kv(cckv_ref[...], ckpe_ref[...])

        @pl.when(jnp.logical_not(is_ctx))
        def _():
            project_tokens()
    else:
        project_tokens()


def _mla_proj_call(x2d, mod3, prm, tables, *, seq_len, emit_cache, tm, ctx=None):
    t = x2d.shape[0]
    nb = mod3.shape[0]
    rows_per_b = t // nb
    rope = tables is not None
    tiles_per_seq = seq_len // tm
    hw = N_HEADS * HEAD_PAD
    steps_per_seq = tiles_per_seq + (1 if ctx is not None else 0)
    n_steps = (t // seq_len) * steps_per_seq

    def tok(i):
        return (i // steps_per_seq) * tiles_per_seq + jnp.minimum(i % steps_per_seq, tiles_per_seq - 1)

    def const(shape):
        return pl.BlockSpec(shape, lambda i: (0, 0))

    def rows(n):
        return pl.BlockSpec((tm, n), lambda i: (tok(i), 0))

    in_specs = [rows(D_MODEL),
                pl.BlockSpec((1, 1, 6 * D_MODEL), lambda i: ((tok(i) * tm) // rows_per_b, 0, 0)),
                const((1, D_MODEL)), const(prm["w_in"].shape), const((1, Q_RANK)), const(prm["w_qb"].shape),
                const((1, KV_RANK)), const(prm["w_k"].shape), const(prm["w_v"].shape), const((1, hw)),
                const((1, LANES))]
    args = [x2d, mod3, prm["g_mix"], prm["w_in"], prm["g_q"], prm["w_qb"], prm["g_kv"], prm["w_k"], prm["w_v"],
            prm["v_ones"], prm["pe_mask"]]
    if rope:
        in_specs += [pl.BlockSpec((tm, LANES), lambda i: (tok(i) % tiles_per_seq, 0))] * 2
        args += list(tables)
    if ctx is not None:
        in_specs += [pl.BlockSpec((tm, KV_RANK), lambda i: (i // steps_per_seq, 0)),
                     pl.BlockSpec((tm, LANES), lambda i: (i // steps_per_seq, 0))]
        args += list(ctx)
    kv_rows = pl.BlockSpec((tm, hw), lambda i: (i, 0))
    out_specs = [rows(hw), kv_rows, kv_rows]
    out_shape = [jax.ShapeDtypeStruct((t, hw), BF16)] + [jax.ShapeDtypeStruct((n_steps * tm, hw), BF16)] * 2
    if emit_cache:
        out_specs += [rows(KV_RANK), rows(LANES)]
        out_shape += [jax.ShapeDtypeStruct((t, KV_RANK), F32), jax.ShapeDtypeStruct((t, LANES), F32)]
    return pl.pallas_call(
        functools.partial(_mla_proj_kernel, rope=rope, emit_cache=emit_cache,
                          ctx_every=steps_per_seq if ctx is not None else 0),
        grid=(n_steps,),
        in_specs=in_specs,
        out_specs=out_specs,
        out_shape=out_shape,
        compiler_params=_cparams(("arbitrary",)),
        name="l1_mla_proj",
    )(*args)


Q_PRESCALE = (QK_NOPE + QK_ROPE) ** -0.5 * math.log2(math.e)
_NT = (((1,), (1,)), ((), ()))
ROW_SUB = 256


def _pair_out(acc_even, acc_odd):
    lane = lax.broadcasted_iota(jnp.int32, acc_even.shape, 1)
    first = lane < V_HEAD
    num = jnp.where(first, acc_even, acc_odd)
    den = jnp.where(first, pltpu.roll(acc_even, V_HEAD, axis=1), pltpu.roll(acc_odd, V_HEAD, axis=1))
    return (num / den).astype(BF16)


def _attn_ctx_kernel(q_ref, k_ref, v_ref, o_ref, s_scr, p_scr, *, pairs):
    heads = [slice(h * HEAD_PAD, (h + 1) * HEAD_PAD) for h in range(2 * pairs)]
    for h, sl in enumerate(heads):
        s_scr[h] = lax.dot_general(q_ref[0, :, sl], k_ref[0, :, sl], _NT, preferred_element_type=F32)
    for h in range(2 * pairs):
        s = s_scr[h]
        p_scr[h] = jnp.exp2(s - s.max(axis=-1, keepdims=True)).astype(BF16)
    for p in range(pairs):
        accs = [_dot(p_scr[2 * p + e], v_ref[0, :, heads[2 * p + e]]) for e in range(2)]
        o_ref[0, :, p * LANES:(p + 1) * LANES] = _pair_out(accs[0], accs[1])


def _attn_ctx_call(q3, k3, v3):
    b, s, _ = q3.shape
    spec = lambda w: pl.BlockSpec((1, s, w), lambda bi: (bi, 0, 0))
    hw = N_HEADS * HEAD_PAD
    return pl.pallas_call(
        functools.partial(_attn_ctx_kernel, pairs=N_HEADS // 2),
        grid=(b,),
        in_specs=[spec(hw), spec(hw), spec(hw)],
        out_specs=spec(N_HEADS * V_HEAD),
        out_shape=jax.ShapeDtypeStruct((b, s, N_HEADS * V_HEAD), BF16),
        scratch_shapes=[pltpu.VMEM((N_HEADS, s, s), F32), pltpu.VMEM((N_HEADS, s, s), BF16)],
        compiler_params=_cparams(("arbitrary",)),
        name="l1_attention_ctx",
    )(q3, k3, v3)


def _attn_lat_kernel(q_ref, k_ref, vc_ref, vl_ref, o_ref, s0, s1, m0, m1, p0, p1, acc0_save):
    n = pl.program_id(0)
    n_keys = k_ref.shape[1]
    h0 = slice(0, HEAD_PAD)
    h1 = slice(HEAD_PAD, 2 * HEAD_PAD)

    @pl.when(n == 0)
    def _():
        s1[...] = jnp.zeros(s1.shape, F32)
        m1[...] = jnp.zeros(m1.shape, F32)
        acc0_save[...] = jnp.ones(acc0_save.shape, F32)

    row_blocks = [slice(r, r + ROW_SUB) for r in range(0, q_ref.shape[1], ROW_SUB)]

    def scores(sl, s_w, m_w):
        for rows in row_blocks:
            s = lax.dot_general(q_ref[0, rows, sl], k_ref[0, :, sl], _NT, preferred_element_type=F32)
            s_w[rows, :] = s
            m_w[rows, :] = jnp.broadcast_to(s.max(axis=-1, keepdims=True), (ROW_SUB, LANES))

    def values(sl, s_r, m_r, p_scr, v_ref):
        for rows in row_blocks:
            m = m_r[rows, :]
            for t in range(n_keys // LANES):
                tl = slice(t * LANES, (t + 1) * LANES)
                p_scr[rows, tl] = jnp.exp2(s_r[rows, tl] - m).astype(BF16)
        return _dot(p_scr[...], v_ref[0, :, sl])

    @pl.when(n >= 0)
    def _():
        scores(h0, s0, m0)
        acc1 = values(h1, s1, m1, p1, vl_ref)
        o_ref[0] = _pair_out(acc0_save[...], acc1)

    @pl.when(n < pl.num_programs(0))
    def _():
        scores(h1, s1, m1)
        acc0_save[...] = values(h0, s0, m0, p0, vc_ref)


def _attn_lat_call(q3, k3, v3, *, tq):
    b, s, _ = q3.shape
    sk = k3.shape[1]
    npair = N_HEADS // 2
    nq = s // tq
    n_items = b * npair * nq

    def item(n):
        return n // (npair * nq), n % nq, (n // nq) % npair

    def cur(n):
        return item(jnp.minimum(n, n_items - 1))

    def lag(n):
        return item(jnp.maximum(n - 1, 0))

    qw = 2 * HEAD_PAD
    kv_cur = pl.BlockSpec((1, sk, qw), lambda n: (cur(n)[0], 0, cur(n)[2]))
    kv_lag = pl.BlockSpec((1, sk, qw), lambda n: (lag(n)[0], 0, lag(n)[2]))
    in_specs = [pl.BlockSpec((1, tq, qw), lambda n: cur(n)), kv_cur, kv_cur, kv_lag]
    return pl.pallas_call(
        _attn_lat_kernel,
        grid=(n_items + 1,),
        in_specs=in_specs,
        out_specs=pl.BlockSpec((1, tq, LANES), lambda n: lag(n)),
        out_shape=jax.ShapeDtypeStruct((b, s, N_HEADS * V_HEAD), BF16),
        scratch_shapes=[pltpu.VMEM((tq, sk), F32)] * 2 + [pltpu.VMEM((tq, LANES), F32)] * 2
        + [pltpu.VMEM((tq, sk), BF16)] * 2 + [pltpu.VMEM((tq, LANES), F32)],
        compiler_params=_cparams(("arbitrary",)),
        name="l1_attention_lat",
    )(q3, k3, v3, v3)


def _block_diag_halves(w):
    nb = RNN_BLOCKS // 2
    w4 = w.astype(BF16).reshape(2, nb, RNN_BLOCK, RNN_BLOCK)
    eye = jnp.eye(nb, dtype=BF16)
    return (w4[:, :, :, None, :] * eye[None, :, None, :, None]).reshape(2, nb * RNN_BLOCK, nb * RNN_BLOCK)


def _rope_tables(seq_len):
    rows = seq_len // GRID_W
    row = np.repeat(np.arange(rows, dtype=np.float64), GRID_W)
    col = np.tile(np.arange(GRID_W, dtype=np.float64), rows)
    half = QK_ROPE // 2
    inv = ROPE_BASE ** (-np.arange(0, half, 2, dtype=np.float64) / half)
    ang_r, ang_c = row[:, None] * inv, col[:, None] * inv
    zeros = np.zeros((seq_len, 8))
    cr, sr, cc, sc = np.cos(ang_r), np.sin(ang_r), np.cos(ang_c), np.sin(ang_c)
    lead = np.ones((seq_len, QK_NOPE))
    tail = np.zeros((seq_len, HEAD_PAD - QK_NOPE - PE_WIDTH))
    cos = np.concatenate([lead, zeros, cr, cr, zeros, cc, cc, tail], axis=1)
    sin = np.concatenate([0 * lead, zeros, -sr, sr, zeros, -sc, sc, tail], axis=1)
    return tuple(jnp.asarray(t, dtype=F32) for t in (cos, sin))


def _pe_layout_cols(w):
    x1r, x2r, x1c, x2c = (w[..., 8 * i:8 * (i + 1)] for i in range(4))
    return jnp.concatenate([x2r, x1r, x2r, x2c, x1c, x2c], axis=-1)


def kernel(x_prompt, x_sample, state_l0_lru, cache_l1_ckv, cache_l1_kpe, c, c_ctx, l0_w_mod, l0_b_mod, l0_g_mix, l0_g_ffn, l0_w_in, l0_conv_w, l0_conv_b, l0_lru_w_a, l0_lru_b_a, l0_lru_w_i, l0_lru_b_i, l0_lru_lam, l0_pool_w, l0_pool_scale, l0_w_out, l0_ffn_w1, l0_ffn_w2, l1_w_mod, l1_b_mod, l1_g_mix, l1_g_ffn, l1_w_in, l1_g_q, l1_w_qb, l1_g_kv, l1_w_kvb, l1_w_out, l1_ffn_w1, l1_ffn_w2, g_final):
    bp, sp, d = x_prompt.shape
    bs, ss, _ = x_sample.shape
    past = cache_l1_ckv.shape[1]
    xp = x_prompt.reshape(bp * sp, d)
    xs = x_sample.reshape(bs * ss, d)

    cond8 = jnp.concatenate([c, c_ctx[None, :], jnp.zeros((SUBLANES - bs - 1, d), F32)], axis=0)
    m0 = _mod_call(cond8, l0_w_mod, l0_b_mod)
    m1 = _mod_call(cond8, l1_w_mod, l1_b_mod)
    mod_p = [m[bs:bs + 1].reshape(1, 1, 6 * d) for m in (m0, m1)]
    mod_s = [m[0:bs].reshape(bs, 1, 6 * d) for m in (m0, m1)]

    lru = dict(
        conv_w=l0_conv_w, conv_b=l0_conv_b.reshape(1, D_RNN),
        wa=[_block_diag_halves(l0_lru_w_a[i]) for i in range(2)],
        wi=[_block_diag_halves(l0_lru_w_i[i]) for i in range(2)],
        ba=[l0_lru_b_a[i].reshape(1, D_RNN) for i in range(2)],
        bi=[l0_lru_b_i[i].reshape(1, D_RNN) for i in range(2)],
        lam=[l0_lru_lam[i].reshape(1, D_RNN) for i in range(2)],
        pool_w=l0_pool_w.astype(BF16), pool_scale=l0_pool_scale.reshape(1, D_POOL),
        g_mix=l0_g_mix.reshape(1, d), w_in=l0_w_in.astype(BF16))
    wo0 = l0_w_out.astype(BF16)
    wos0 = [wo0[:D_RNN], wo0[D_RNN:]]
    w1_0, w2_0 = l0_ffn_w1.astype(BF16), l0_ffn_w2.astype(BF16)

    def layer0(x2d, mod3, nseq, seq_len, h0f, h0b, ts):
        x3 = x2d.reshape(nseq, seq_len, d)
        if ts == seq_len:
            yrnn, stf, ypool, stb = _lru_call(x3, mod3, h0f, None, lru, reverse=False, ts=ts, h0b=h0b)
        else:
            hf, stf, ypool, xc, gate = _lru_call(x3, mod3, h0f, None, lru, reverse=False, ts=ts)
            yrnn, stb = _lru_call(None, None, h0b, (hf, xc, gate), lru, reverse=True, ts=ts)
        ys = [yrnn.reshape(-1, D_RNN), ypool.reshape(-1, D_POOL)]
        x2 = _ffn_call(x2d, ys, mod3, wos0, l0_g_ffn, w1_0, w2_0, None, tm=FFN_ROWS)
        return x2, stf, stb

    zero_st = jnp.zeros((bp, 1, D_RNN), F32)
    xp, stf, stb = layer0(xp, mod_p[0], bp, sp, zero_st, zero_st, ts=min(sp, SEQ_TILE))
    xs, _, _ = layer0(xs, mod_s[0], bs, ss, state_l0_lru[:, 0:1], state_l0_lru[:, 1:2], ts=min(ss, SEQ_TILE))
    new_lru = jnp.concatenate([stf, stb], axis=1)

    pad_pe = HEAD_PAD - QK_NOPE - PE_WIDTH
    w_in1b = l1_w_in.astype(BF16)
    w_in1 = jnp.concatenate([w_in1b[:, :Q_RANK + KV_RANK], jnp.zeros((d, QK_NOPE), BF16),
                             _pe_layout_cols(w_in1b[:, Q_RANK + KV_RANK:]), jnp.zeros((d, pad_pe), BF16)], axis=1)
    wqb = l1_w_qb.astype(BF16).reshape(Q_RANK, N_HEADS, QK_NOPE + QK_ROPE)
    wqb = jnp.concatenate([wqb[:, :, :QK_NOPE], _pe_layout_cols(wqb[:, :, QK_NOPE:]),
                           jnp.zeros((Q_RANK, N_HEADS, pad_pe), BF16)], axis=2).reshape(Q_RANK, N_HEADS * HEAD_PAD)
    pe_mask = jnp.concatenate([jnp.ones((QK_NOPE,), F32)] + [m * jnp.ones((8,), F32) for m in (0, 1, 1, 0, 1, 1)]
                              + [jnp.zeros((pad_pe,), F32)]).reshape(1, HEAD_PAD)
    wkvb = l1_w_kvb.astype(BF16).reshape(KV_RANK, N_HEADS, QK_NOPE + V_HEAD)
    w_k = jnp.pad(wkvb[:, :, :QK_NOPE], ((0, 0), (0, 0), (0, HEAD_PAD - QK_NOPE)))
    w_k = w_k.reshape(KV_RANK, N_HEADS * HEAD_PAD)
    wv = wkvb[:, :, QK_NOPE:].reshape(KV_RANK, N_HEADS // 2, 2, V_HEAD)
    zv = jnp.zeros((KV_RANK, N_HEADS // 2, V_HEAD), BF16)
    w_v = jnp.stack([wv[:, :, 0], zv, zv, wv[:, :, 1]], axis=2).reshape(KV_RANK, N_HEADS * HEAD_PAD)
    one = jnp.ones((N_HEADS // 2, V_HEAD), F32)
    v_ones = jnp.stack([0 * one, one, one, 0 * one], axis=1).reshape(1, N_HEADS * HEAD_PAD)
    mla = dict(g_mix=l1_g_mix.reshape(1, d), w_in=w_in1, g_q=l1_g_q.reshape(1, Q_RANK) * Q_PRESCALE, w_qb=wqb,
               g_kv=l1_g_kv.reshape(1, KV_RANK), w_k=w_k, w_v=w_v, v_ones=v_ones, pe_mask=pe_mask)
    wo1 = [l1_w_out.astype(BF16)]
    w1_1, w2_1 = l1_ffn_w1.astype(BF16), l1_ffn_w2.astype(BF16)

    qp, kp, vp, ckv_new, kpe_new = _mla_proj_call(xp, mod_p[1], mla, None, seq_len=bp * sp, emit_cache=True,
                                                  tm=SEQ_TILE)
    op = _attn_ctx_call(qp.reshape(bp, sp, -1), kp.reshape(bp, sp, -1), vp.reshape(bp, sp, -1))
    y_prompt = _ffn_call(xp, [op.reshape(bp * sp, -1)], mod_p[1], wo1, l1_g_ffn, w1_1, w2_1, g_final, tm=FFN_ROWS)

    assert past == SEQ_TILE
    tables = _rope_tables(ss)
    kpe_c = cache_l1_kpe.reshape(bs * past, QK_ROPE)
    gap = jnp.zeros((bs * past, 8), F32)
    kpe_ctx = jnp.concatenate([jnp.zeros((bs * past, QK_NOPE), F32), gap, kpe_c[:, :16], gap, kpe_c[:, 16:],
                               jnp.zeros((bs * past, pad_pe), F32)], axis=1)
    qs, ks, vs = _mla_proj_call(xs, mod_s[1], mla, tables, seq_len=ss, emit_cache=False, tm=SEQ_TILE,
                                ctx=(cache_l1_ckv.reshape(bs * past, KV_RANK), kpe_ctx))
    os_ = _attn_lat_call(qs.reshape(bs, ss, -1), ks.reshape(bs, ss + past, -1), vs.reshape(bs, ss + past, -1),
                         tq=SEQ_TILE)
    y_sample = _ffn_call(xs, [os_.reshape(bs * ss, -1)], mod_s[1], wo1, l1_g_ffn, w1_1, w2_1, g_final, tm=FFN_ROWS)

    new_ckv = ckv_new.reshape(bp, sp, KV_RANK)
    new_kpe = jnp.concatenate([kpe_new[:, QK_NOPE + 8:QK_NOPE + 24], kpe_new[:, QK_NOPE + 32:QK_NOPE + 48]],
                              axis=1).reshape(bp, sp, QK_ROPE)
    return (y_prompt.reshape(bp, sp, d), y_sample.reshape(bs, ss, d), new_lru, new_ckv, new_kpe)
```

```python
import functools
import math

import jax
import jax.numpy as jnp
import numpy as np
from jax import lax
from jax.experimental import pallas as pl
from jax.experimental.pallas import tpu as pltpu

F32 = jnp.float32
BF16 = jnp.bfloat16

D_MODEL = 1024
D_FF = 4 * D_MODEL
EPS = 1e-6
D_RNN = 512
RNN_BLOCKS = 8
RNN_BLOCK = 64
CONV_W = 4
LRU_C = 8.0
D_POOL = 512
POOL_WINDOWS = (2, 4, 8, 16)
POOL_GROUP = 128
N_HEADS = 16
QK_NOPE = 64
QK_ROPE = 32
V_HEAD = 64
Q_RANK = 384
KV_RANK = 256
ROPE_BASE = 10000.0
GRID_W = 64

LANES = 128
SUBLANES = 8
HEAD_PAD = 128
PE_WIDTH = 48
HALO = 8
VMEM_LIMIT = 52 * 1024 * 1024
SEQ_TILE = 512
SCAN_TILE = 1024


def _cparams(sem):
    return pltpu.CompilerParams(dimension_semantics=sem, vmem_limit_bytes=VMEM_LIMIT)


def _dot(a, b):
    return jnp.dot(a, b, preferred_element_type=F32)


def _sigmoid(x):
    return 0.5 * jnp.tanh(0.5 * x) + 0.5


def _rms_mod(x, g, shift, scale):
    ms = jnp.mean(x * x, axis=-1, keepdims=True)
    y = x * lax.rsqrt(ms + EPS) * g
    return y * (1.0 + scale) + shift


def _mod_kernel(c_ref, w_ref, b_ref, o_ref):
    c = c_ref[...]
    s = c * jax.nn.sigmoid(c)
    o_ref[...] = _dot(s.astype(BF16), w_ref[...].astype(BF16)) + b_ref[...]


def _mod_call(cond8, w_mod, b_mod):
    n = w_mod.shape[1]
    tn = 1024
    return pl.pallas_call(
        _mod_kernel,
        grid=(n // tn,),
        in_specs=[
            pl.BlockSpec((SUBLANES, D_MODEL), lambda j: (0, 0)),
            pl.BlockSpec((D_MODEL, tn), lambda j: (0, j)),
            pl.BlockSpec((1, tn), lambda j: (0, j)),
        ],
        out_specs=pl.BlockSpec((SUBLANES, tn), lambda j: (0, j)),
        out_shape=jax.ShapeDtypeStruct((SUBLANES, n), F32),
        compiler_params=_cparams(("arbitrary",)),
        name="adaln_mod",
    )(cond8, w_mod, b_mod.reshape(1, n))


def _lru_coeffs(xc, wa_ref, ba_ref, wi_ref, bi_ref, lam_ref):
    xcb = xc.astype(BF16)
    half = D_RNN // 2
    ga = jnp.concatenate([_dot(xcb[:, :half], wa_ref[0]), _dot(xcb[:, half:], wa_ref[1])], axis=-1)
    gi = jnp.concatenate([_dot(xcb[:, :half], wi_ref[0]), _dot(xcb[:, half:], wi_ref[1])], axis=-1)
    r = _sigmoid(ga + ba_ref[...])
    i = _sigmoid(gi + bi_ref[...])
    log_a = (LRU_C * r) * jax.nn.log_sigmoid(lam_ref[...])
    a = jnp.exp(log_a)
    v = -jnp.tanh(log_a) * (a * a + 1.0)
    u = jnp.where(v > 0.0, v * lax.rsqrt(v), 0.0) * (i * xc)
    return a, u


def _scan_tile(a, u, carry_row, reverse, a_s, u_s, ts):
    g = ts // SUBLANES
    a3 = a.reshape(g, SUBLANES, D_RNN)
    u3 = u.reshape(g, SUBLANES, D_RNN)
    row = lax.broadcasted_iota(jnp.int32, (g, SUBLANES, D_RNN), 1)
    for d in (1, 2, 4):
        if reverse:
            sh = SUBLANES - d
            m = row < SUBLANES - d
        else:
            sh = d
            m = row >= d
        a_sh = pltpu.roll(a3, sh, axis=1)
        u_sh = pltpu.roll(u3, sh, axis=1)
        u3 = u3 + a3 * jnp.where(m, u_sh, 0.0)
        a3 = a3 * jnp.where(m, a_sh, 1.0)
    a_s[...] = a3.reshape(ts, D_RNN)
    u_s[...] = u3.reshape(ts, D_RNN)
    edge = 0 if reverse else SUBLANES - 1

    def body(k, c):
        gi = (g - 1 - k) if reverse else k
        r0 = pl.multiple_of(gi * SUBLANES, SUBLANES)
        h = u_s[pl.ds(r0, SUBLANES), :] + a_s[pl.ds(r0, SUBLANES), :] * c
        u_s[pl.ds(r0, SUBLANES), :] = h
        return jnp.broadcast_to(h[edge:edge + 1, :], (SUBLANES, D_RNN))

    c0 = jnp.broadcast_to(carry_row, (SUBLANES, D_RNN))
    c_last = lax.fori_loop(0, g, body, c0, unroll=4)
    return c_last[0:1, :]


def _lru_kernel(*refs, reverse, with_pool, both, ts, nt, seq_len):
    if both:
        (x_ref, xprev_ref, xnext_ref, mod_ref, g_ref, win_ref, h0_ref, cw_ref, cb_ref, wa_ref, ba_ref, wi_ref,
         bi_ref, lam_ref, pw_ref, ps_ref, h0b_ref, wab_ref, bab_ref, wib_ref, bib_ref, lamb_ref,
         y_ref, st_ref, yp_ref, stb_ref,
         xe_s, ext_s, a_s, u_s, carry_s, b2_s, b4_s, b8_s, ab_s, ub_s) = refs
    elif with_pool:
        (x_ref, xprev_ref, xnext_ref, mod_ref, g_ref, win_ref, h0_ref, cw_ref, cb_ref, wa_ref, ba_ref, wi_ref,
         bi_ref, lam_ref, pw_ref, ps_ref, h_ref, st_ref, yp_ref, xc_ref, gate_out_ref,
         xe_s, ext_s, a_s, u_s, carry_s, b2_s, b4_s, b8_s) = refs
    else:
        (xc_in_ref, gate_ref, hf_ref, h0_ref, wa_ref, ba_ref, wi_ref,
         bi_ref, lam_ref, y_ref, st_ref, a_s, u_s, carry_s) = refs
    j = pl.program_id(1)
    t = (nt - 1 - j) if reverse else j
    n2 = ts + 2 * HALO

    @pl.when(j == 0)
    def _():
        carry_s[...] = h0_ref[0]

    if with_pool:
        xe_s[0:HALO, :] = xprev_ref[0]
        xe_s[HALO:HALO + ts, :] = x_ref[0]
        xe_s[HALO + ts:n2, :] = xnext_ref[0]
        shift = mod_ref[0, :, 0:D_MODEL]
        scale = mod_ref[0, :, D_MODEL:2 * D_MODEL]
        hx = _rms_mod(xe_s[...], g_ref[...], shift, scale).astype(BF16)
        z = _dot(hx, win_ref[...])
        if not both:
            gate_out_ref[0] = z[HALO:HALO + ts, D_RNN:2 * D_RNN]

        def fill_ext(cols):
            ext_s[0:HALO, :] = jnp.where(t > 0, z[0:HALO, cols], 0.0)
            ext_s[HALO:HALO + ts, :] = z[HALO:HALO + ts, cols]
            ext_s[HALO + ts:n2, :] = jnp.where(t < nt - 1, z[HALO + ts:n2, cols], 0.0)

        fill_ext(slice(0, D_RNN))
        xc = cb_ref[...]
        for k in range(CONV_W):
            off = HALO - CONV_W // 2 + k
            xc = xc + ext_s[off:off + ts, :] * cw_ref[k:k + 1, :]
        if not both:
            xc_ref[0] = xc
        fill_ext(slice(2 * D_RNN, 3 * D_RNN))
    else:
        xc = xc_in_ref[0]

    a, u = _lru_coeffs(xc, wa_ref, ba_ref, wi_ref, bi_ref, lam_ref)
    c_last = _scan_tile(a, u, carry_s[...], reverse, a_s, u_s, ts)
    carry_s[...] = c_last
    st_ref[0] = c_last
    h = u_s[...]

    if not with_pool:
        y_ref[0] = ((hf_ref[0] + h) * jax.nn.gelu(gate_ref[0])).astype(BF16)
        return

    if both:
        ab, ub = _lru_coeffs(xc, wab_ref, bab_ref, wib_ref, bib_ref, lamb_ref)
        stb_ref[0] = _scan_tile(ab, ub, h0b_ref[0], True, ab_s, ub_s, ts)
        gate = z[HALO:HALO + ts, D_RNN:2 * D_RNN]
        y_ref[0] = ((h + ub_s[...]) * jax.nn.gelu(gate)).astype(BF16)
    else:
        h_ref[0] = h

    b2_s[1:n2, :] = ext_s[1:n2, :] + ext_s[0:n2 - 1, :]
    b4_s[2:n2 - 1, :] = b2_s[3:n2, LANES:] + b2_s[1:n2 - 2, LANES:]
    b8_s[4:n2 - 3, :] = b4_s[6:n2 - 1, LANES:] + b4_s[2:n2 - 5, LANES:]
    s16 = b8_s[HALO + 4:HALO + 4 + ts, LANES:] + b8_s[HALO - 4:HALO - 4 + ts, LANES:]
    sums = (b2_s[HALO:HALO + ts, 0:LANES], b4_s[HALO:HALO + ts, 0:LANES], b8_s[HALO:HALO + ts, 0:LANES], s16)
    before = (t * ts + lax.broadcasted_iota(jnp.int32, (ts, LANES), 0)).astype(F32)
    after = float(seq_len - 1) - before
    for gidx, w in enumerate(POOL_WINDOWS):
        left = w // 2
        right = w - 1 - left
        cnt = jnp.minimum(before, float(left)) + (jnp.minimum(after, float(right)) + 1.0)
        xg = ext_s[HALO:HALO + ts, gidx * LANES:(gidx + 1) * LANES]
        dlt = sums[gidx] / cnt - xg
        yg = _dot(dlt.astype(BF16), pw_ref[gidx]) * ps_ref[:, gidx * LANES:(gidx + 1) * LANES]
        yp_ref[0, :, gidx * LANES:(gidx + 1) * LANES] = yg.astype(BF16)


def _lru_call(x3, mod3, h0, fwd, prm, *, reverse, ts, h0b=None):
    nseq, seq_len = (x3 if not reverse else fwd[0]).shape[:2]
    nt = seq_len // ts
    tb = ts // HALO
    nhb = seq_len // HALO
    d = 1 if reverse else 0
    with_pool = not reverse
    both = h0b is not None
    assert not both or (with_pool and nt == 1)

    def tpos(j):
        return (nt - 1 - j) if reverse else j

    def const(shape):
        nd = len(shape)
        return pl.BlockSpec(shape, lambda s, j: (0,) * nd)

    tile_out = pl.BlockSpec((1, ts, D_RNN), lambda s, j: (s, tpos(j), 0))
    st_spec = pl.BlockSpec((1, 1, D_RNN), lambda s, j: (s, 0, 0))
    conv_specs = [const((CONV_W, D_RNN)), const((1, D_RNN))]
    conv_args = [prm["conv_w"], prm["conv_b"]]
    w_specs = [const((2, 256, 256)), const((1, D_RNN)), const((2, 256, 256)), const((1, D_RNN)), const((1, D_RNN))]
    w_args = [prm["wa"][d], prm["ba"][d], prm["wi"][d], prm["bi"][d], prm["lam"][d]]
    scratch = [pltpu.VMEM((ts, D_RNN), F32), pltpu.VMEM((ts, D_RNN), F32), pltpu.VMEM((1, D_RNN), F32)]
    kern = functools.partial(_lru_kernel, reverse=reverse, with_pool=with_pool, both=both, ts=ts, nt=nt,
                             seq_len=seq_len)
    if with_pool:
        nb = mod3.shape[0]
        x_specs = [
            pl.BlockSpec((1, ts, D_MODEL), lambda s, j: (s, j, 0)),
            pl.BlockSpec((1, HALO, D_MODEL), lambda s, j: (s, jnp.maximum(j * tb - 1, 0), 0)),
            pl.BlockSpec((1, HALO, D_MODEL), lambda s, j: (s, jnp.minimum((j + 1) * tb, nhb - 1), 0)),
            pl.BlockSpec((1, 1, 6 * D_MODEL), lambda s, j: ((s * nb) // nseq, 0, 0)),
            const((1, D_MODEL)), const(prm["w_in"].shape)]
        in_specs = x_specs + [st_spec] + conv_specs + w_specs + [
            const((4, POOL_GROUP, POOL_GROUP)), const((1, D_POOL))]
        args = [x3, x3, x3, mod3, prm["g_mix"], prm["w_in"], h0] + conv_args + w_args + [
            prm["pool_w"], prm["pool_scale"]]
        out_specs = [tile_out, st_spec, tile_out, tile_out, tile_out]
        out_shape = [jax.ShapeDtypeStruct((nseq, seq_len, D_RNN), F32),
                     jax.ShapeDtypeStruct((nseq, 1, D_RNN), F32),
                     jax.ShapeDtypeStruct((nseq, seq_len, D_POOL), BF16),
                     jax.ShapeDtypeStruct((nseq, seq_len, D_RNN), F32),
                     jax.ShapeDtypeStruct((nseq, seq_len, D_RNN), F32)]
        n2 = ts + 2 * HALO
        scratch = [pltpu.VMEM((n2, D_MODEL), F32), pltpu.VMEM((n2, D_RNN), F32)] + scratch + [
            pltpu.VMEM((n2, D_POOL), F32), pltpu.VMEM((n2, D_POOL - LANES), F32),
            pltpu.VMEM((n2, D_POOL - 2 * LANES), F32)]
        name = "l0_in_lru_fwd_pool"
        if both:
            in_specs = in_specs + [st_spec] + w_specs
            args = args + [h0b, prm["wa"][1], prm["ba"][1], prm["wi"][1], prm["bi"][1], prm["lam"][1]]
            out_specs = [tile_out, st_spec, tile_out, st_spec]
            out_shape = [jax.ShapeDtypeStruct((nseq, seq_len, D_RNN), BF16), out_shape[1], out_shape[2],
                         out_shape[1]]
            scratch = scratch + [pltpu.VMEM((ts, D_RNN), F32), pltpu.VMEM((ts, D_RNN), F32)]
            name = "l0_in_lru_both_pool"
    else:
        hf, xc, gate = fwd
        in_specs = [tile_out, tile_out, tile_out, st_spec] + w_specs
        args = [xc, gate, hf, h0] + w_args
        out_specs = [tile_out, st_spec]
        out_shape = [jax.ShapeDtypeStruct((nseq, seq_len, D_RNN), BF16),
                     jax.ShapeDtypeStruct((nseq, 1, D_RNN), F32)]
        name = "l0_lru_bwd"
    return pl.pallas_call(
        kern,
        grid=(nseq, nt),
        in_specs=in_specs,
        out_specs=out_specs,
        out_shape=out_shape,
        scratch_shapes=scratch,
        compiler_params=_cparams(("arbitrary", "arbitrary")),
        name=name,
    )(*args)


FF_CHUNK = 1024
FFN_ROWS = 1024


def _residual_ffn(x, mix, mod_ref, g_ref, w1_ref, w2_ref):
    def mod(k):
        return mod_ref[0, :, k * D_MODEL:(k + 1) * D_MODEL]

    x1 = x + mod(2) * mix
    hn = _rms_mod(x1, g_ref[...], mod(3), mod(4)).astype(BF16)
    acc = None
    for c in range(D_FF // FF_CHUNK):
        hc = _dot(hn, w1_ref[:, c * FF_CHUNK:(c + 1) * FF_CHUNK])
        hc = jnp.square(jnp.maximum(hc, 0.0)).astype(BF16)
        part = _dot(hc, w2_ref[c * FF_CHUNK:(c + 1) * FF_CHUNK, :])
        acc = part if acc is None else acc + part
    return x1 + mod(5) * acc


def _ffn_kernel(*refs, n_y, final_norm):
    x_ref = refs[0]
    y_refs = refs[1:1 + n_y]
    mod_ref = refs[1 + n_y]
    wo_refs = refs[2 + n_y:2 + 2 * n_y]
    g_ref, w1_ref, w2_ref = refs[2 + 2 * n_y:5 + 2 * n_y]
    rest = refs[5 + 2 * n_y:]
    if final_norm:
        gf_ref, o_ref = rest
    else:
        (o_ref,) = rest

    mix = _dot(y_refs[0][...], wo_refs[0][...])
    for k in range(1, n_y):
        mix = mix + _dot(y_refs[k][...], wo_refs[k][...])
    x2 = _residual_ffn(x_ref[...], mix, mod_ref, g_ref, w1_ref, w2_ref)
    if final_norm:
        ms = jnp.mean(x2 * x2, axis=-1, keepdims=True)
        x2 = x2 * lax.rsqrt(ms + EPS) * gf_ref[...]
    o_ref[...] = x2


def _ffn_call(x2d, ys, mod3, wos, g, w1, w2, g_final, tm):
    t = x2d.shape[0]
    nb = mod3.shape[0]
    rows_per_b = t // nb
    n_y = len(ys)
    final_norm = g_final is not None

    def const(shape):
        return pl.BlockSpec(shape, lambda i: (0, 0), pipeline_mode=pl.Buffered(1))

    in_specs = [pl.BlockSpec((tm, D_MODEL), lambda i: (i, 0))]
    in_specs += [pl.BlockSpec((tm, y.shape[1]), lambda i: (i, 0)) for y in ys]
    in_specs += [pl.BlockSpec((1, 1, 6 * D_MODEL), lambda i: ((i * tm) // rows_per_b, 0, 0))]
    in_specs += [const(w.shape) for w in wos]
    in_specs += [const((1, D_MODEL)), const((D_MODEL, D_FF)), const((D_FF, D_MODEL))]
    args = [x2d, *ys, mod3, *wos, g.reshape(1, D_MODEL), w1, w2]
    if final_norm:
        in_specs.append(const((1, D_MODEL)))
        args.append(g_final.reshape(1, D_MODEL))
    return pl.pallas_call(
        functools.partial(_ffn_kernel, n_y=n_y, final_norm=final_norm),
        grid=(t // tm,),
        in_specs=in_specs,
        out_specs=pl.BlockSpec((tm, D_MODEL), lambda i: (i, 0)),
        out_shape=jax.ShapeDtypeStruct((t, D_MODEL), F32),
        compiler_params=_cparams(("arbitrary",)),
        name="mix_out_ffn",
    )(*args)


def _rope(x, cos, sin):
    return x * cos + pltpu.roll(x, 8, axis=1) * sin


def _mla_proj_kernel(*refs, rope, emit_cache):
    x_ref, mod_ref, g_ref, win_ref, gq_ref, wqb_ref, gkv_ref, wk_ref, wv_ref, vones_ref, pemask_ref = refs[:11]
    rest = refs[11:]
    if rope:
        cos_ref, sin_ref = rest[:2]
        rest = rest[2:]
    q_ref, k_ref, v_ref = rest[:3]
    if emit_cache:
        ckv_ref, kpe_ref = rest[3:]

    shift = mod_ref[0, :, 0:D_MODEL]
    scale = mod_ref[0, :, D_MODEL:2 * D_MODEL]
    h = _rms_mod(x_ref[...], g_ref[...], shift, scale).astype(BF16)
    z = _dot(h, win_ref[...])
    cq = z[:, :Q_RANK]
    ckv = z[:, Q_RANK:Q_RANK + KV_RANK]
    kpe = z[:, Q_RANK + KV_RANK:]
    cqn = cq * lax.rsqrt(jnp.mean(cq * cq, axis=-1, keepdims=True) + EPS) * gq_ref[...]
    ckvn = ckv * lax.rsqrt(jnp.mean(ckv * ckv, axis=-1, keepdims=True) + EPS) * gkv_ref[...]
    if emit_cache:
        ckv_ref[...] = ckvn
        kpe_ref[...] = kpe
    q = _dot(cqn.astype(BF16), wqb_ref[...])
    ckvb = ckvn.astype(BF16)
    kn = _dot(ckvb, wk_ref[...])
    v_ref[...] = (_dot(ckvb, wv_ref[...]) + vones_ref[...]).astype(BF16)
    if rope:
        cos, sin = cos_ref[...], sin_ref[...]
        kpe = _rope(kpe, cos, sin)
    else:
        kpe = kpe * pemask_ref[...]
    for hd in range(N_HEADS):
        sl = slice(hd * HEAD_PAD, (hd + 1) * HEAD_PAD)
        qh = q[:, sl]
        if rope:
            qh = _rope(qh, cos, sin)
        q_ref[:, sl] = qh.astype(BF16)
        k_ref[:, sl] = (kn[:, sl] + kpe).astype(BF16)


def _mla_proj_call(x2d, mod3, prm, tables, *, seq_len, emit_cache, tm):
    t = x2d.shape[0]
    nb = mod3.shape[0]
    rows_per_b = t // nb
    rope = tables is not None
    tiles_per_seq = seq_len // tm
    hw = N_HEADS * HEAD_PAD

    def const(shape):
        return pl.BlockSpec(shape, lambda i: (0, 0))

    def rows(n):
        return pl.BlockSpec((tm, n), lambda i: (i, 0))

    in_specs = [rows(D_MODEL),
                pl.BlockSpec((1, 1, 6 * D_MODEL), lambda i: ((i * tm) // rows_per_b, 0, 0)),
                const((1, D_MODEL)), const(prm["w_in"].shape), const((1, Q_RANK)), const(prm["w_qb"].shape),
                const((1, KV_RANK)), const(prm["w_k"].shape), const(prm["w_v"].shape), const((1, hw)),
                const((1, LANES))]
    args = [x2d, mod3, prm["g_mix"], prm["w_in"], prm["g_q"], prm["w_qb"], prm["g_kv"], prm["w_k"], prm["w_v"],
            prm["v_ones"], prm["pe_mask"]]
    if rope:
        in_specs += [pl.BlockSpec((tm, LANES), lambda i: (i % tiles_per_seq, 0))] * 2
        args += list(tables)
    out_specs = [rows(hw), rows(hw), rows(hw)]
    out_shape = [jax.ShapeDtypeStruct((t, hw), BF16)] * 3
    if emit_cache:
        out_specs += [rows(KV_RANK), rows(LANES)]
        out_shape += [jax.ShapeDtypeStruct((t, KV_RANK), F32), jax.ShapeDtypeStruct((t, LANES), F32)]
    return pl.pallas_call(
        functools.partial(_mla_proj_kernel, rope=rope, emit_cache=emit_cache),
        grid=(t // tm,),
        in_specs=in_specs,
        out_specs=out_specs,
        out_shape=out_shape,
        compiler_params=_cparams(("arbitrary",)),
        name="l1_mla_proj",
    )(*args)


def _ctx_expand_kernel(ckv_ref, kpe_ref, wk_ref, wv_ref, vones_ref, k_ref, v_ref):
    ckvb = ckv_ref[...].astype(BF16)
    kn = _dot(ckvb, wk_ref[...])
    v_ref[...] = (_dot(ckvb, wv_ref[...]) + vones_ref[...]).astype(BF16)
    kpe = kpe_ref[...]
    for hd in range(N_HEADS):
        sl = slice(hd * HEAD_PAD, (hd + 1) * HEAD_PAD)
        k_ref[:, sl] = (kn[:, sl] + kpe).astype(BF16)


def _ctx_expand_call(ckv2d, kpe2d, prm, tm):
    hw = N_HEADS * HEAD_PAD
    t = ckv2d.shape[0]
    const = lambda a: pl.BlockSpec(a.shape, lambda i: (0, 0))
    rows = lambda n: pl.BlockSpec((tm, n), lambda i: (i, 0))
    return pl.pallas_call(
        _ctx_expand_kernel,
        grid=(t // tm,),
        in_specs=[rows(KV_RANK), rows(LANES), const(prm["w_k"]), const(prm["w_v"]), const(prm["v_ones"])],
        out_specs=[rows(hw), rows(hw)],
        out_shape=[jax.ShapeDtypeStruct((t, hw), BF16), jax.ShapeDtypeStruct((t, hw), BF16)],
        compiler_params=_cparams(("arbitrary",)),
        name="l1_ctx_expand",
    )(ckv2d, kpe2d, prm["w_k"], prm["w_v"], prm["v_ones"])


Q_PRESCALE = (QK_NOPE + QK_ROPE) ** -0.5 * math.log2(math.e)
_NT = (((1,), (1,)), ((), ()))
ROW_SUB = 256
CTX_SEQS_PER_STEP = 2


def _pair_out(acc_even, acc_odd):
    lane = lax.broadcasted_iota(jnp.int32, acc_even.shape, 1)
    first = lane < V_HEAD
    num = jnp.where(first, acc_even, acc_odd)
    den = jnp.where(first, pltpu.roll(acc_even, V_HEAD, axis=1), pltpu.roll(acc_odd, V_HEAD, axis=1))
    return (num / den).astype(BF16)


def _attn_ctx_kernel(q_ref, k_ref, v_ref, o_ref, s_scr, p_scr, *, pairs):
    heads = [slice(h * HEAD_PAD, (h + 1) * HEAD_PAD) for h in range(2 * pairs)]
    nseq = q_ref.shape[0]
    nh = 2 * pairs
    for b in range(nseq):
        for h, sl in enumerate(heads):
            s_scr[b * nh + h] = lax.dot_general(q_ref[b, :, sl], k_ref[b, :, sl], _NT, preferred_element_type=F32)
    for i in range(nseq * nh):
        s = s_scr[i]
        p_scr[i] = jnp.exp2(s - s.max(axis=-1, keepdims=True)).astype(BF16)
    for b in range(nseq):
        for p in range(pairs):
            accs = [_dot(p_scr[b * nh + 2 * p + e], v_ref[b, :, heads[2 * p + e]]) for e in range(2)]
            o_ref[b, :, p * LANES:(p + 1) * LANES] = _pair_out(accs[0], accs[1])


def _attn_ctx_call(q3, k3, v3):
    b, s, _ = q3.shape
    per = CTX_SEQS_PER_STEP
    spec = lambda w: pl.BlockSpec((per, s, w), lambda bi: (bi, 0, 0))
    hw = N_HEADS * HEAD_PAD
    return pl.pallas_call(
        functools.partial(_attn_ctx_kernel, pairs=N_HEADS // 2),
        grid=(b // per,),
        in_specs=[spec(hw), spec(hw), spec(hw)],
        out_specs=spec(N_HEADS * V_HEAD),
        out_shape=jax.ShapeDtypeStruct((b, s, N_HEADS * V_HEAD), BF16),
        scratch_shapes=[pltpu.VMEM((per * N_HEADS, s, s), F32), pltpu.VMEM((per * N_HEADS, s, s), BF16)],
        compiler_params=_cparams(("arbitrary",)),
        name="l1_attention_ctx",
    )(q3, k3, v3)


def _attn_lat_kernel(q_ref, k1_ref, k2_ref, v1c_ref, v2c_ref, v1l_ref, v2l_ref, o_ref,
                     s0, s1, m0, m1, p0, p1, acc0_save):
    n = pl.program_id(0)
    n1, n2 = k1_ref.shape[1], k2_ref.shape[1]
    h0 = slice(0, HEAD_PAD)
    h1 = slice(HEAD_PAD, 2 * HEAD_PAD)

    @pl.when(n == 0)
    def _():
        s1[...] = jnp.zeros(s1.shape, F32)
        m1[...] = jnp.zeros(m1.shape, F32)
        acc0_save[...] = jnp.ones(acc0_save.shape, F32)

    row_blocks = [slice(r, r + ROW_SUB) for r in range(0, q_ref.shape[1], ROW_SUB)]

    def scores(sl, s_w, m_w):
        for rows in row_blocks:
            q = q_ref[0, rows, sl]
            sa = lax.dot_general(q, k1_ref[0, :, sl], _NT, preferred_element_type=F32)
            sb = lax.dot_general(q, k2_ref[0, :, sl], _NT, preferred_element_type=F32)
            s_w[rows, 0:n1] = sa
            s_w[rows, n1:] = sb
            m = jnp.maximum(sa.max(axis=-1, keepdims=True), sb.max(axis=-1, keepdims=True))
            m_w[rows, :] = jnp.broadcast_to(m, (ROW_SUB, LANES))

    def values(sl, s_r, m_r, p_scr, va_ref, vb_ref):
        for rows in row_blocks:
            m = m_r[rows, :]
            for t in range((n1 + n2) // LANES):
                tl = slice(t * LANES, (t + 1) * LANES)
                p_scr[rows, tl] = jnp.exp2(s_r[rows, tl] - m).astype(BF16)
        return _dot(p_scr[:, 0:n1], va_ref[0, :, sl]) + _dot(p_scr[:, n1:], vb_ref[0, :, sl])

    @pl.when(n >= 0)
    def _():
        scores(h0, s0, m0)
        acc1 = values(h1, s1, m1, p1, v1l_ref, v2l_ref)
        o_ref[0] = _pair_out(acc0_save[...], acc1)

    @pl.when(n < pl.num_programs(0))
    def _():
        scores(h1, s1, m1)
        acc0_save[...] = values(h0, s0, m0, p0, v1c_ref, v2c_ref)


def _attn_lat_call(q3, k1, k2, v1, v2, *, tq):
    b, s, _ = q3.shape
    s1, s2 = k1.shape[1], k2.shape[1]
    sk = s1 + s2
    npair = N_HEADS // 2
    nq = s // tq
    n_items = b * npair * nq

    def item(n):
        return n // (npair * nq), n % nq, (n // nq) % npair

    def cur(n):
        return item(jnp.minimum(n, n_items - 1))

    def lag(n):
        return item(jnp.maximum(n - 1, 0))

    qw = 2 * HEAD_PAD
    kv_cur = lambda rows: pl.BlockSpec((1, rows, qw), lambda n: (cur(n)[0], 0, cur(n)[2]))
    kv_lag = lambda rows: pl.BlockSpec((1, rows, qw), lambda n: (lag(n)[0], 0, lag(n)[2]))
    in_specs = [pl.BlockSpec((1, tq, qw), lambda n: cur(n)),
                kv_cur(s1), kv_cur(s2), kv_cur(s1), kv_cur(s2), kv_lag(s1), kv_lag(s2)]
    return pl.pallas_call(
        _attn_lat_kernel,
        grid=(n_items + 1,),
        in_specs=in_specs,
        out_specs=pl.BlockSpec((1, tq, LANES), lambda n: lag(n)),
        out_shape=jax.ShapeDtypeStruct((b, s, N_HEADS * V_HEAD), BF16),
        scratch_shapes=[pltpu.VMEM((tq, sk), F32)] * 2 + [pltpu.VMEM((tq, LANES), F32)] * 2
        + [pltpu.VMEM((tq, sk), BF16)] * 2 + [pltpu.VMEM((tq, LANES), F32)],
        compiler_params=_cparams(("arbitrary",)),
        name="l1_attention_lat",
    )(q3, k1, k2, v1, v2, v1, v2)


def _block_diag_halves(w):
    nb = RNN_BLOCKS // 2
    w4 = w.astype(BF16).reshape(2, nb, RNN_BLOCK, RNN_BLOCK)
    eye = jnp.eye(nb, dtype=BF16)
    return (w4[:, :, :, None, :] * eye[None, :, None, :, None]).reshape(2, nb * RNN_BLOCK, nb * RNN_BLOCK)


def _rope_tables(seq_len):
    rows = seq_len // GRID_W
    row = np.repeat(np.arange(rows, dtype=np.float64), GRID_W)
    col = np.tile(np.arange(GRID_W, dtype=np.float64), rows)
    half = QK_ROPE // 2
    inv = ROPE_BASE ** (-np.arange(0, half, 2, dtype=np.float64) / half)
    ang_r, ang_c = row[:, None] * inv, col[:, None] * inv
    zeros = np.zeros((seq_len, 8))
    cr, sr, cc, sc = np.cos(ang_r), np.sin(ang_r), np.cos(ang_c), np.sin(ang_c)
    lead = np.ones((seq_len, QK_NOPE))
    tail = np.zeros((seq_len, HEAD_PAD - QK_NOPE - PE_WIDTH))
    cos = np.concatenate([lead, zeros, cr, cr, zeros, cc, cc, tail], axis=1)
    sin = np.concatenate([0 * lead, zeros, -sr, sr, zeros, -sc, sc, tail], axis=1)
    return tuple(jnp.asarray(t, dtype=F32) for t in (cos, sin))


def _pe_layout_cols(w):
    x1r, x2r, x1c, x2c = (w[..., 8 * i:8 * (i + 1)] for i in range(4))
    return jnp.concatenate([x2r, x1r, x2r, x2c, x1c, x2c], axis=-1)


def kernel(x_prompt, x_sample, state_l0_lru, cache_l1_ckv, cache_l1_kpe, c, c_ctx, l0_w_mod, l0_b_mod, l0_g_mix, l0_g_ffn, l0_w_in, l0_conv_w, l0_conv_b, l0_lru_w_a, l0_lru_b_a, l0_lru_w_i, l0_lru_b_i, l0_lru_lam, l0_pool_w, l0_pool_scale, l0_w_out, l0_ffn_w1, l0_ffn_w2, l1_w_mod, l1_b_mod, l1_g_mix, l1_g_ffn, l1_w_in, l1_g_q, l1_w_qb, l1_g_kv, l1_w_kvb, l1_w_out, l1_ffn_w1, l1_ffn_w2, g_final):
    bp, sp, d = x_prompt.shape
    bs, ss, _ = x_sample.shape
    past = cache_l1_ckv.shape[1]
    xp = x_prompt.reshape(bp * sp, d)
    xs = x_sample.reshape(bs * ss, d)

    cond8 = jnp.concatenate([c, c_ctx[None, :], jnp.zeros((SUBLANES - bs - 1, d), F32)], axis=0)
    m0 = _mod_call(cond8, l0_w_mod, l0_b_mod)
    m1 = _mod_call(cond8, l1_w_mod, l1_b_mod)
    mod_p = [m[bs:bs + 1].reshape(1, 1, 6 * d) for m in (m0, m1)]
    mod_s = [m[0:bs].reshape(bs, 1, 6 * d) for m in (m0, m1)]

    lru = dict(
        conv_w=l0_conv_w, conv_b=l0_conv_b.reshape(1, D_RNN),
        wa=[_block_diag_halves(l0_lru_w_a[i]) for i in range(2)],
        wi=[_block_diag_halves(l0_lru_w_i[i]) for i in range(2)],
        ba=[l0_lru_b_a[i].reshape(1, D_RNN) for i in range(2)],
        bi=[l0_lru_b_i[i].reshape(1, D_RNN) for i in range(2)],
        lam=[l0_lru_lam[i].reshape(1, D_RNN) for i in range(2)],
        pool_w=l0_pool_w.astype(BF16), pool_scale=l0_pool_scale.reshape(1, D_POOL),
        g_mix=l0_g_mix.reshape(1, d), w_in=l0_w_in.astype(BF16))
    wo0 = l0_w_out.astype(BF16)
    wos0 = [wo0[:D_RNN], wo0[D_RNN:]]
    w1_0, w2_0 = l0_ffn_w1.astype(BF16), l0_ffn_w2.astype(BF16)

    def layer0(x2d, mod3, nseq, seq_len, h0f, h0b, ts):
        x3 = x2d.reshape(nseq, seq_len, d)
        if ts == seq_len:
            yrnn, stf, ypool, stb = _lru_call(x3, mod3, h0f, None, lru, reverse=False, ts=ts, h0b=h0b)
        else:
            hf, stf, ypool, xc, gate = _lru_call(x3, mod3, h0f, None, lru, reverse=False, ts=ts)
            yrnn, stb = _lru_call(None, None, h0b, (hf, xc, gate), lru, reverse=True, ts=ts)
        ys = [yrnn.reshape(-1, D_RNN), ypool.reshape(-1, D_POOL)]
        x2 = _ffn_call(x2d, ys, mod3, wos0, l0_g_ffn, w1_0, w2_0, None, tm=FFN_ROWS)
        return x2, stf, stb

    zero_st = jnp.zeros((bp, 1, D_RNN), F32)
    xp, stf, stb = layer0(xp, mod_p[0], bp, sp, zero_st, zero_st, ts=min(sp, SCAN_TILE))
    xs, _, _ = layer0(xs, mod_s[0], bs, ss, state_l0_lru[:, 0:1], state_l0_lru[:, 1:2], ts=min(ss, SCAN_TILE))
    new_lru = jnp.concatenate([stf, stb], axis=1)

    pad_pe = HEAD_PAD - QK_NOPE - PE_WIDTH
    w_in1b = l1_w_in.astype(BF16)
    w_in1 = jnp.concatenate([w_in1b[:, :Q_RANK + KV_RANK], jnp.zeros((d, QK_NOPE), BF16),
                             _pe_layout_cols(w_in1b[:, Q_RANK + KV_RANK:]), jnp.zeros((d, pad_pe), BF16)], axis=1)
    wqb = l1_w_qb.astype(BF16).reshape(Q_RANK, N_HEADS, QK_NOPE + QK_ROPE)
    wqb = jnp.concatenate([wqb[:, :, :QK_NOPE], _pe_layout_cols(wqb[:, :, QK_NOPE:]),
                           jnp.zeros((Q_RANK, N_HEADS, pad_pe), BF16)], axis=2).reshape(Q_RANK, N_HEADS * HEAD_PAD)
    pe_mask = jnp.concatenate([jnp.ones((QK_NOPE,), F32)] + [m * jnp.ones((8,), F32) for m in (0, 1, 1, 0, 1, 1)]
                              + [jnp.zeros((pad_pe,), F32)]).reshape(1, HEAD_PAD)
    wkvb = l1_w_kvb.astype(BF16).reshape(KV_RANK, N_HEADS, QK_NOPE + V_HEAD)
    w_k = jnp.pad(wkvb[:, :, :QK_NOPE], ((0, 0), (0, 0), (0, HEAD_PAD - QK_NOPE)))
    w_k = w_k.reshape(KV_RANK, N_HEADS * HEAD_PAD)
    wv = wkvb[:, :, QK_NOPE:].reshape(KV_RANK, N_HEADS // 2, 2, V_HEAD)
    zv = jnp.zeros((KV_RANK, N_HEADS // 2, V_HEAD), BF16)
    w_v = jnp.stack([wv[:, :, 0], zv, zv, wv[:, :, 1]], axis=2).reshape(KV_RANK, N_HEADS * HEAD_PAD)
    one = jnp.ones((N_HEADS // 2, V_HEAD), F32)
    v_ones = jnp.stack([0 * one, one, one, 0 * one], axis=1).reshape(1, N_HEADS * HEAD_PAD)
    mla = dict(g_mix=l1_g_mix.reshape(1, d), w_in=w_in1, g_q=l1_g_q.reshape(1, Q_RANK) * Q_PRESCALE, w_qb=wqb,
               g_kv=l1_g_kv.reshape(1, KV_RANK), w_k=w_k, w_v=w_v, v_ones=v_ones, pe_mask=pe_mask)
    wo1 = [l1_w_out.astype(BF16)]
    w1_1, w2_1 = l1_ffn_w1.astype(BF16), l1_ffn_w2.astype(BF16)

    qp, kp, vp, ckv_new, kpe_new = _mla_proj_call(xp, mod_p[1], mla, None, seq_len=bp * sp, emit_cache=True,
                                                  tm=SEQ_TILE)
    op = _attn_ctx_call(qp.reshape(bp, sp, -1), kp.reshape(bp, sp, -1), vp.reshape(bp, sp, -1))
    y_prompt = _ffn_call(xp, [op.reshape(bp * sp, -1)], mod_p[1], wo1, l1_g_ffn, w1_1, w2_1, g_final, tm=FFN_ROWS)

    tables = _rope_tables(ss)
    qs, ks, vs = _mla_proj_call(xs, mod_s[1], mla, tables, seq_len=ss, emit_cache=False, tm=SEQ_TILE)
    kpe_c = cache_l1_kpe.reshape(bs * past, QK_ROPE)
    gap = jnp.zeros((bs * past, 8), F32)
    kpe_ctx = jnp.concatenate([jnp.zeros((bs * past, QK_NOPE), F32), gap, kpe_c[:, :16], gap, kpe_c[:, 16:],
                               jnp.zeros((bs * past, pad_pe), F32)], axis=1)
    kc, vc = _ctx_expand_call(cache_l1_ckv.reshape(bs * past, KV_RANK), kpe_ctx, mla, tm=past)
    os_ = _attn_lat_call(qs.reshape(bs, ss, -1), ks.reshape(bs, ss, -1), kc.reshape(bs, past, -1),
                         vs.reshape(bs, ss, -1), vc.reshape(bs, past, -1), tq=SEQ_TILE)
    y_sample = _ffn_call(xs, [os_.reshape(bs * ss, -1)], mod_s[1], wo1, l1_g_ffn, w1_1, w2_1, g_final, tm=FFN_ROWS)

    new_ckv = ckv_new.reshape(bp, sp, KV_RANK)
    new_kpe = jnp.concatenate([kpe_new[:, QK_NOPE + 8:QK_NOPE + 24], kpe_new[:, QK_NOPE + 32:QK_NOPE + 48]],
                              axis=1).reshape(bp, sp, QK_ROPE)
    return (y_prompt.reshape(bp, sp, d), y_sample.reshape(bs, ss, d), new_lru, new_ckv, new_kpe)
```
